```python
import math
import jax, jax.numpy as jnp
from jax import lax
import numpy as np

D_MODEL = 1024
BATCH = 32
SEQ = 256
DEPTH = 2
DEC_BATCH = 8
DEC_SEQ = 1024
PAST_LEN = 256

GRID_W = 64
HEAD_DIM = 64
BLK = 128
A_HEADS = 4
A_KV = 2
B_HEADS = 4
B_HALF = HEAD_DIM // 2
C_HEADS = 4
C_KV = 2
C_WINDOW = 128
D_HEADS = 4
NA_ROWS = 8
NA_COLS = 16
N_BRANCH = 4
BRANCH_W = 4 * HEAD_DIM
D_FF = -(-8 * D_MODEL // (3 * 256)) * 256
ROPE_THETA = 10000.0
NORM_EPS = 1e-6
SUBLN_EPS = 1e-5
NEG_INF = -1e30
IN_SPLITS = (A_HEADS * HEAD_DIM, A_KV * HEAD_DIM, A_KV * HEAD_DIM,
             B_HEADS * HEAD_DIM, B_HEADS * HEAD_DIM, B_HEADS * HEAD_DIM,
             C_HEADS * HEAD_DIM, C_KV * HEAD_DIM, C_KV * HEAD_DIM,
             D_HEADS * HEAD_DIM, D_HEADS * HEAD_DIM, D_HEADS * HEAD_DIM,
             N_BRANCH * D_MODEL)
IN_COLS = sum(IN_SPLITS)

kernel_name = 'hybrid_dit_prefix_step'


def rms_norm(x, g, eps=NORM_EPS):
    xf = x.astype(jnp.float32)
    y = xf * lax.rsqrt(jnp.mean(xf * xf, axis=-1, keepdims=True) + eps)
    return (y * g.astype(jnp.float32)).astype(x.dtype)


def axial_rope(x):
    t_len, d = x.shape[-2], x.shape[-1]
    quarter = d // 4
    inv = jnp.power(jnp.float32(ROPE_THETA), -jnp.arange(quarter, dtype=jnp.float32) / quarter)
    t = jnp.arange(t_len)
    ang_r = (t // GRID_W).astype(jnp.float32)[:, None] * inv
    ang_c = (t % GRID_W).astype(jnp.float32)[:, None] * inv

    def rot(xp, ang):
        cos = jnp.cos(ang).astype(x.dtype)
        sin = jnp.sin(ang).astype(x.dtype)
        x1, x2 = jnp.split(xp, 2, axis=-1)
        return jnp.concatenate([x1 * cos - x2 * sin, x1 * sin + x2 * cos], axis=-1)

    xr, xc = jnp.split(x, 2, axis=-1)
    return jnp.concatenate([rot(xr, ang_r), rot(xc, ang_c)], axis=-1)


def to_heads(x, n):
    b, t, _ = x.shape
    return x.reshape(b, t, n, -1).transpose(0, 2, 1, 3)


def from_heads(x):
    b, h, t, d = x.shape
    return x.transpose(0, 2, 1, 3).reshape(b, t, h * d)


def group_q(q, n_kv):
    b, h, t, d = q.shape
    return q.reshape(b, n_kv, h // n_kv, t, d)


def blocked_attention(q, k, v, sink=None):
    b, hk, g, t, d = q.shape
    nb = t // BLK
    scale = d ** -0.5
    qb = jnp.moveaxis(q.reshape(b, hk, g, nb, BLK, d), 3, 0)

    def one_block(qi):
        s = jnp.einsum('bkgqd,bksd->bkgqs', qi, k).astype(jnp.float32) * scale
        if sink is not None:
            s_sink = jnp.broadcast_to(sink.astype(jnp.float32)[None, :, :, None, None], s.shape[:-1] + (1,))
            s = jnp.concatenate([s, s_sink], axis=-1)
        w = jax.nn.softmax(s, axis=-1)
        if sink is not None:
            w = w[..., :-1]
        return jnp.einsum('bkgqs,bksd->bkgqd', w.astype(v.dtype), v)

    o = lax.map(one_block, qb)
    return jnp.moveaxis(o, 0, 3).reshape(b, hk * g, t, d)


def diff_attention(q, k, v, lam):
    b, h, t, dq = q.shape
    half = dq // 2
    scale = half ** -0.5
    nb = t // BLK
    k1, k2 = k[..., :half], k[..., half:]
    qb = jnp.moveaxis(q.reshape(b, h, nb, BLK, dq), 2, 0)

    def one_block(qi):
        s1 = jnp.einsum('bhqd,bhsd->bhqs', qi[..., :half], k1).astype(jnp.float32) * scale
        s2 = jnp.einsum('bhqd,bhsd->bhqs', qi[..., half:], k2).astype(jnp.float32) * scale
        p = jax.nn.softmax(s1, axis=-1) - lam * jax.nn.softmax(s2, axis=-1)
        return jnp.einsum('bhqs,bhsd->bhqd', p.astype(v.dtype), v)

    o = lax.map(one_block, qb)
    return jnp.moveaxis(o, 0, 2).reshape(b, h, t, v.shape[-1])


def banded_attention(q, k, v, k_ctx, v_ctx, sink):
    b, hk, g, t, d = q.shape
    nb = t // BLK
    scale = d ** -0.5

    def windows(x):
        xp = jnp.pad(x, ((0, 0), (0, 0), (BLK, BLK), (0, 0))).reshape(b, hk, nb + 2, BLK, d)
        return jnp.concatenate([xp[:, :, :nb], xp[:, :, 1:nb + 1], xp[:, :, 2:]], axis=3)

    kw, vw = windows(k), windows(v)
    qb = q.reshape(b, hk, g, nb, BLK, d)
    s_band = jnp.einsum('bkgnqd,bknsd->bkgnqs', qb, kw).astype(jnp.float32) * scale
    blk = jnp.arange(nb)[:, None]
    qpos = blk * BLK + jnp.arange(BLK)[None, :]
    kpos = (blk - 1) * BLK + jnp.arange(3 * BLK)[None, :]
    valid = ((jnp.abs(qpos[:, :, None] - kpos[:, None, :]) <= C_WINDOW)
             & (kpos[:, None, :] >= 0) & (kpos[:, None, :] < t))
    s_band = jnp.where(valid, s_band, NEG_INF)
    s_ctx = jnp.einsum('bkgnqd,bksd->bkgnqs', qb, k_ctx).astype(jnp.float32) * scale
    s_sink = jnp.broadcast_to(sink.astype(jnp.float32)[None, :, :, None, None, None], s_ctx.shape[:-1] + (1,))
    w = jax.nn.softmax(jnp.concatenate([s_band, s_ctx, s_sink], axis=-1), axis=-1).astype(v.dtype)
    nw = 3 * BLK
    n_ctx = k_ctx.shape[2]
    o = (jnp.einsum('bkgnqs,bknsd->bkgnqd', w[..., :nw], vw)
         + jnp.einsum('bkgnqs,bksd->bkgnqd', w[..., nw:nw + n_ctx], v_ctx))
    return o.reshape(b, hk * g, t, d)


def neighbourhood_attention(q, k, v, k_ctx, v_ctx, rpb):
    b, h, t, d = q.shape
    rows = t // GRID_W
    kr = min(NA_ROWS, rows)
    scale = d ** -0.5
    qg = q.reshape(b, h, rows, GRID_W, d)
    kg = k.reshape(b, h, rows, GRID_W, d)
    vg = v.reshape(b, h, rows, GRID_W, d)
    r = jnp.arange(rows)
    row_idx = jnp.clip(r - kr // 2, 0, rows - kr)[:, None] + jnp.arange(kr)[None, :]
    kn = kg[:, :, row_idx]
    vn = vg[:, :, row_idx]
    s_nb = jnp.einsum('bhrcd,bhrjwd->bhrcjw', qg, kn).astype(jnp.float32) * scale
    col = jnp.arange(GRID_W)
    start_c = jnp.clip(col - NA_COLS // 2, 0, GRID_W - NA_COLS)
    col_valid = (col[None, :] >= start_c[:, None]) & (col[None, :] < start_c[:, None] + NA_COLS)
    dr = row_idx - r[:, None] + (NA_ROWS - 1)
    dc = jnp.clip(col[None, :] - col[:, None], -(NA_COLS - 1), NA_COLS - 1) + (NA_COLS - 1)
    bias = rpb[:, dr[:, None, :, None], dc[None, :, None, :]].astype(jnp.float32)
    s_nb = jnp.where(col_valid[:, None, :], s_nb + bias[None], NEG_INF)
    nk = kr * GRID_W
    s_nb = s_nb.reshape(b, h, rows, GRID_W, nk)
    s_ctx = jnp.einsum('bhrcd,bhsd->bhrcs', qg, k_ctx).astype(jnp.float32) * scale
    w = jax.nn.softmax(jnp.concatenate([s_nb, s_ctx], axis=-1), axis=-1).astype(v.dtype)
    w_nb = w[..., :nk].reshape(b, h, rows, GRID_W, kr, GRID_W)
    o = (jnp.einsum('bhrcjw,bhrjwd->bhrcd', w_nb, vn)
         + jnp.einsum('bhrcs,bhsd->bhrcd', w[..., nk:], v_ctx))
    return o.reshape(b, h, t, d)


def halves_rope(x):
    return jnp.concatenate([axial_rope(x[..., :B_HALF]), axial_rope(x[..., B_HALF:])], axis=-1)


def mixers(h, ctx_kv, lam_init, w_in, g_q_a, g_k_a, lam_b, g_subln_b, sink_c, rpb_d, w_branch, w_out):
    b, t, _ = h.shape
    points = np.cumsum(IN_SPLITS)[:-1].tolist()
    aq, ak, av, bq, bk, bv, cq, ck, cv, dq, dk, dv, gates = jnp.split(h @ w_in, points, axis=-1)
    qa = rms_norm(to_heads(aq, A_HEADS), g_q_a)
    ka = rms_norm(to_heads(ak, A_KV), g_k_a)
    va = to_heads(av, A_KV)
    qb, kb, vb = to_heads(bq, B_HEADS), to_heads(bk, B_HEADS), to_heads(bv, B_HEADS)
    qc, kc, vc = to_heads(cq, C_HEADS), to_heads(ck, C_KV), to_heads(cv, C_KV)
    qd, kd, vd = to_heads(dq, D_HEADS), to_heads(dk, D_HEADS), to_heads(dv, D_HEADS)
    lp = lam_b.astype(jnp.float32)
    lam = jnp.exp(jnp.sum(lp[0] * lp[1])) - jnp.exp(jnp.sum(lp[2] * lp[3])) + lam_init
    sink = sink_c.reshape(C_KV, C_HEADS // C_KV)
    if ctx_kv is None:
        oa = blocked_attention(group_q(qa, A_KV), ka, va)
        ob = diff_attention(qb, kb, vb, lam)
        oc = blocked_attention(group_q(qc, C_KV), kc, vc, sink)
        od = blocked_attention(group_q(qd, D_HEADS), kd, vd)
        new_kv = (ka, va, kb, vb, kc, vc, kd, vd)
    else:
        ka_x, va_x, kb_x, vb_x, kc_x, vc_x, kd_x, vd_x = ctx_kv
        oa = blocked_attention(group_q(axial_rope(qa), A_KV),
                               jnp.concatenate([ka_x, axial_rope(ka)], axis=2),
                               jnp.concatenate([va_x, va], axis=2))
        ob = diff_attention(halves_rope(qb),
                            jnp.concatenate([kb_x, halves_rope(kb)], axis=2),
                            jnp.concatenate([vb_x, vb], axis=2), lam)
        oc = banded_attention(group_q(axial_rope(qc), C_KV), axial_rope(kc), vc, kc_x, vc_x, sink)
        od = neighbourhood_attention(qd, kd, vd, kd_x, vd_x, rpb_d)
        new_kv = None
    ob = rms_norm(ob, g_subln_b, SUBLN_EPS) * (1.0 - lam_init)
    branches = jnp.stack([from_heads(oa), from_heads(ob), from_heads(oc), from_heads(od)], axis=2)
    proj = jnp.einsum('btkc,kcd->btkd', branches, w_branch)
    gate = jax.nn.sigmoid(gates.reshape(b, t, N_BRANCH, D_MODEL))
    merged = jnp.sum(gate * proj, axis=2)
    return merged @ w_out, new_kv


def trunk_layer(x, cond, ctx_kv, lam_init, w_ada, b_ada, g_norm1, w_in, g_q_a, g_k_a, lam_b, g_subln_b,
                sink_c, rpb_d, w_branch, w_out, g_norm2, w_ffn_in, w_ffn_out):
    sh1, sc1, gt1, sh2, sc2, gt2 = jnp.split(jax.nn.silu(cond) @ w_ada + b_ada, 6, axis=-1)
    h = rms_norm(x, g_norm1) * (1 + sc1) + sh1
    mix, new_kv = mixers(h, ctx_kv, lam_init, w_in, g_q_a, g_k_a, lam_b, g_subln_b, sink_c, rpb_d, w_branch, w_out)
    x = x + gt1 * mix
    h = rms_norm(x, g_norm2) * (1 + sc2) + sh2
    a, u = jnp.split(h @ w_ffn_in, 2, axis=-1)
    x = x + gt2 * ((jax.nn.silu(a) * u) @ w_ffn_out)
    return x, new_kv


def setup_inputs(seed: int = 0) -> dict:
    key = jax.random.key(seed)
    ks = iter(jax.random.split(key, 32))

    def nrm(shape, scale=1.0):
        return jax.random.normal(next(ks), shape, jnp.float32) * scale

    def gain(shape):
        return 1.0 + nrm(shape, 0.05)

    d = D_MODEL
    return {
        'x_prompt': nrm((BATCH, SEQ, d)),
        'x_sample': nrm((DEC_BATCH, DEC_SEQ, d)),
        'cache_a_k': nrm((DEC_BATCH, DEPTH, A_KV, PAST_LEN, HEAD_DIM)),
        'cache_a_v': nrm((DEC_BATCH, DEPTH, A_KV, PAST_LEN, HEAD_DIM)),
        'cache_b_k': nrm((DEC_BATCH, DEPTH, B_HEADS, PAST_LEN, HEAD_DIM)),
        'cache_b_v': nrm((DEC_BATCH, DEPTH, B_HEADS, PAST_LEN, HEAD_DIM)),
        'cache_c_k': nrm((DEC_BATCH, DEPTH, C_KV, PAST_LEN, HEAD_DIM)),
        'cache_c_v': nrm((DEC_BATCH, DEPTH, C_KV, PAST_LEN, HEAD_DIM)),
        'cache_d_k': nrm((DEC_BATCH, DEPTH, D_HEADS, PAST_LEN, HEAD_DIM)),
        'cache_d_v': nrm((DEC_BATCH, DEPTH, D_HEADS, PAST_LEN, HEAD_DIM)),
        'c': nrm((DEC_BATCH, d)),
        'c_ctx': nrm((d,)),
        'w_ada': nrm((DEPTH, d, 6 * d), 0.5 * d ** -0.5),
        'b_ada': nrm((DEPTH, 6 * d), 0.02),
        'g_norm1': gain((DEPTH, d)),
        'w_in': nrm((DEPTH, d, IN_COLS), d ** -0.5),
        'g_q_a': gain((DEPTH, HEAD_DIM)),
        'g_k_a': gain((DEPTH, HEAD_DIM)),
        'lam_b': nrm((DEPTH, 4, B_HALF), 0.1),
        'g_subln_b': gain((DEPTH, 2 * B_HALF)),
        'sink_c': nrm((DEPTH, C_HEADS), 0.5),
        'rpb_d': nrm((DEPTH, D_HEADS, 2 * NA_ROWS - 1, 2 * NA_COLS - 1), 0.2),
        'w_branch': nrm((DEPTH, N_BRANCH, BRANCH_W, d), BRANCH_W ** -0.5),
        'w_out': nrm((DEPTH, d, d), d ** -0.5),
        'g_norm2': gain((DEPTH, d)),
        'w_ffn_in': nrm((DEPTH, d, 2 * D_FF), d ** -0.5),
        'w_ffn_out': nrm((DEPTH, D_FF, d), D_FF ** -0.5),
        'g_final': gain((d,)),
    }


def reference(x_prompt, x_sample, cache_a_k, cache_a_v, cache_b_k, cache_b_v, cache_c_k, cache_c_v,
              cache_d_k, cache_d_v, c, c_ctx, w_ada, b_ada, g_norm1, w_in, g_q_a, g_k_a, lam_b, g_subln_b,
              sink_c, rpb_d, w_branch, w_out, g_norm2, w_ffn_in, w_ffn_out, g_final):
    xp = x_prompt
    xs = x_sample
    cond_lat = c[:, None, :]
    ctx_layers = []
    for l in range(DEPTH):
        lam_init = 0.8 - 0.6 * math.exp(-0.3 * l)
        lw = (w_ada[l], b_ada[l], g_norm1[l], w_in[l], g_q_a[l], g_k_a[l], lam_b[l], g_subln_b[l],
              sink_c[l], rpb_d[l], w_branch[l], w_out[l], g_norm2[l], w_ffn_in[l], w_ffn_out[l])
        xp, kv = trunk_layer(xp, c_ctx, None, lam_init, *lw)
        ctx_layers.append(kv)
        cached = (cache_a_k[:, l], cache_a_v[:, l], cache_b_k[:, l], cache_b_v[:, l],
                  cache_c_k[:, l], cache_c_v[:, l], cache_d_k[:, l], cache_d_v[:, l])
        xs, _ = trunk_layer(xs, cond_lat, cached, lam_init, *lw)
    y_prompt = rms_norm(xp, g_final)
    y_sample = rms_norm(xs, g_final)
    new = [jnp.stack([kv[i] for kv in ctx_layers], axis=1) for i in range(8)]
    return (y_prompt, y_sample, new[0], new[1], new[2], new[3], new[4], new[5], new[6], new[7])
```

```python
import functools
import math

import numpy as np
import jax
import jax.numpy as jnp
from jax import lax
from jax.experimental import pallas as pl
from jax.experimental.pallas import tpu as pltpu

F32 = jnp.float32
BF16 = jnp.bfloat16

D_MODEL = 1024
BATCH = 32
SEQ = 256
DEPTH = 2
DEC_BATCH = 8
DEC_SEQ = 1024
PAST_LEN = 256
GRID_W = 64
HEAD_DIM = 64
B_HALF = HEAD_DIM // 2
C_WINDOW = 128
NA_ROWS = 8
NA_COLS = 16
D_FF = 2816
ROPE_THETA = 10000.0
NORM_EPS = 1e-6
SUBLN_EPS = 1e-5
NEG_INF = -1e30

QKV_W = 2560
GATE_W = 4 * D_MODEL
IN_COLS = QKV_W + GATE_W
MOD_ROWS = 16
CTX_ROW = DEC_BATCH
LANES = 128
ATT_BLK = 256
NA_BLK_ROWS = ATT_BLK // GRID_W
NA_WIN_ROWS = 12
NA_WIN = NA_WIN_ROWS * GRID_W
NA_WIN_START = (0, 0, 4, 4)
C_WIN = ATT_BLK + 2 * C_WINDOW
VMEM_LIMIT = 56 * 1024 * 1024

A_QE, A_QO, A_K, A_V = 0, 128, 256, 384
B_Q, B_K, B_V = 512, 768, 1024
C_QE, C_QO, C_K, C_V = 1280, 1408, 1536, 1664
D_Q, D_K, D_V = 1792, 2048, 2304
X_AK, X_AV, X_BK, X_BV, X_CK, X_CV, X_DK, X_DV = 0, 128, 256, 512, 768, 896, 1024, 1280
CACHE_W = 1536


def _dot(a, b):
    return jnp.dot(a, b, preferred_element_type=F32)


def _dot_nt(a, b):
    return lax.dot_general(a, b, (((1,), (1,)), ((), ())), preferred_element_type=F32)


def _sigmoid(x):
    return 0.5 * jnp.tanh(0.5 * x) + 0.5


def _rms(x, g, eps):
    ms = jnp.mean(x * x, axis=-1, keepdims=True)
    return x * lax.rsqrt(ms + eps) * g


def _head_rms(v, bd, g, eps):
    v2 = v * v
    hi = v2.astype(BF16)
    lo = (v2 - hi.astype(F32)).astype(BF16)
    ms = _dot(hi, bd) + _dot(lo, bd)
    return v * lax.rsqrt(ms + eps) * g


def _lane_mask(shape, lo, hi):
    lane = lax.broadcasted_iota(jnp.int32, shape, 1)
    return (lane >= lo) & (lane < hi)


def _mask_q(qf, lo, hi):
    return jnp.where(_lane_mask(qf.shape, lo, hi), qf, 0.0).astype(BF16)


def _ada_kernel(cond_ref, w_ref, b_ref, o_ref):
    c = cond_ref[...]
    s = (c * _sigmoid(c)).astype(BF16)
    o_ref[...] = _dot(s, w_ref[...].astype(BF16)) + b_ref[...]


def _ada_call(cond, w_ada, b_ada):
    tn = 1536
    return pl.pallas_call(
        _ada_kernel,
        out_shape=jax.ShapeDtypeStruct((DEPTH, MOD_ROWS, 6 * D_MODEL), F32),
        grid=(DEPTH, 6 * D_MODEL // tn),
        in_specs=[
            pl.BlockSpec((MOD_ROWS, D_MODEL), lambda l, j: (0, 0)),
            pl.BlockSpec((None, D_MODEL, tn), lambda l, j: (l, 0, j)),
            pl.BlockSpec((None, 1, tn), lambda l, j: (l, 0, j)),
        ],
        out_specs=pl.BlockSpec((None, MOD_ROWS, tn), lambda l, j: (l, 0, j)),
        compiler_params=pltpu.CompilerParams(vmem_limit_bytes=VMEM_LIMIT),
        name="ada",
    )(cond, w_ada, b_ada.reshape(DEPTH, 1, 6 * D_MODEL))


def _dbias_kernel(rpb_ref, o_ref):
    n_dr, n_dc = 2 * NA_ROWS - 1, 2 * NA_COLS - 1
    base = (pl.program_id(0) * 4 + pl.program_id(1)) * (n_dr * n_dc)
    shape = (GRID_W, LANES)
    cq = lax.broadcasted_iota(jnp.int32, shape, 0)
    lane = lax.broadcasted_iota(jnp.int32, shape, 1)
    ck = lane & (GRID_W - 1)
    dc = jnp.clip(ck - cq, -(NA_COLS - 1), NA_COLS - 1) + (NA_COLS - 1)
    start_c = jnp.clip(cq - NA_COLS // 2, 0, GRID_W - NA_COLS)
    col_valid = (ck >= start_c) & (ck < start_c + NA_COLS)
    neg = jnp.full(shape, NEG_INF, F32)
    toeplitz = []
    for dr in range(n_dr):
        t = jnp.zeros(shape, F32)
        for m in range(n_dc):
            t = jnp.where(dc == m, rpb_ref[base + dr * n_dc + m], t)
        toeplitz.append(jnp.where(col_valid, t, neg))
    left = lane < GRID_W
    rows = DEC_SEQ // GRID_W
    for n in range(DEC_SEQ // ATT_BLK):
        for rq in range(NA_BLK_ROWS):
            r = NA_BLK_ROWS * n + rq
            start_r = min(max(r - NA_ROWS // 2, 0), rows - NA_ROWS)
            for jp in range(NA_WIN_ROWS // 2):
                pair = []
                for j in (2 * jp, 2 * jp + 1):
                    key_row = NA_WIN_START[n] + j
                    valid = start_r <= key_row < start_r + NA_ROWS
                    pair.append(toeplitz[key_row - r + NA_ROWS - 1] if valid else neg)
                tile = pair[0] if pair[0] is pair[1] else jnp.where(left, pair[0], pair[1])
                o_ref[n, rq * GRID_W:(rq + 1) * GRID_W, jp * LANES:(jp + 1) * LANES] = tile


def _dbias_call(rpb_d):
    nblk = DEC_SEQ // ATT_BLK
    return pl.pallas_call(
        _dbias_kernel,
        out_shape=jax.ShapeDtypeStruct((DEPTH, 4, nblk, ATT_BLK, NA_WIN), F32),
        grid=(DEPTH, 4),
        in_specs=[pl.BlockSpec(memory_space=pltpu.SMEM)],
        out_specs=pl.BlockSpec((None, None, nblk, ATT_BLK, NA_WIN), lambda l, h: (l, h, 0, 0, 0)),
        name="dbias",
    )(rpb_d.reshape(-1))


def _swap_halves(v, half):
    lane = lax.broadcasted_iota(jnp.int32, v.shape, 1)
    up = pltpu.roll(v, LANES - half, 1)
    dn = pltpu.roll(v, half, 1)
    return jnp.where((lane & (2 * half - 1)) < half, up, dn)


def _rope(v, cos, sin, half):
    return v * cos + _swap_halves(v, half) * sin


def _pre_kernel(*refs, prompt, tq):
    if prompt:
        (x_ref, sh_ref, sc_ref, g1_ref, w_ref, gq_ref, gk_ref, bd_ref,
         qkv_ref, gates_ref, ka_ref, va_ref, kb_ref, vb_ref, kc_ref, vc_ref, kd_ref, vd_ref) = refs
    else:
        (x_ref, sh_ref, sc_ref, g1_ref, w_ref, gq_ref, gk_ref, bd_ref,
         ca_ref, sa_ref, cb_ref, sb_ref, qkv_ref, gates_ref) = refs

    h = (_rms(x_ref[...], g1_ref[...], NORM_EPS) * (1.0 + sc_ref[...]) + sh_ref[...]).astype(BF16)
    bd = bd_ref[...]

    def rope_a(v):
        return v if prompt else _rope(v, ca_ref[...], sa_ref[...], 16)

    def rope_b(v):
        return v if prompt else _rope(v, cb_ref[...], sb_ref[...], 8)

    def put(col, v):
        qkv_ref[:, col:col + v.shape[1]] = v.astype(BF16)

    def put_heads(ref, v):
        for hh in range(v.shape[1] // HEAD_DIM):
            ref[hh, :, :] = v[:, hh * HEAD_DIM:(hh + 1) * HEAD_DIM]

    scale = HEAD_DIM ** -0.5
    acc = _dot(h, w_ref[:, A_QE:A_QE + 512])
    gq = gq_ref[...]
    put(A_QE, rope_a(_head_rms(acc[:, 0:128], bd, gq, NORM_EPS)) * scale)
    put(A_QO, rope_a(_head_rms(acc[:, 128:256], bd, gq, NORM_EPS)) * scale)
    k_a = _head_rms(acc[:, 256:384], bd, gk_ref[...], NORM_EPS)
    put(A_K, rope_a(k_a))
    put(A_V, acc[:, 384:512])
    if prompt:
        put_heads(ka_ref, k_a)
        put_heads(va_ref, acc[:, 384:512])
    acc = _dot(h, w_ref[:, B_Q:B_Q + 768])
    scale_b = B_HALF ** -0.5
    for c in range(2):
        put(B_Q + c * 128, rope_b(acc[:, c * 128:(c + 1) * 128]) * scale_b)
        put(B_K + c * 128, rope_b(acc[:, 256 + c * 128:256 + (c + 1) * 128]))
    put(B_V, acc[:, 512:768])
    if prompt:
        put_heads(kb_ref, acc[:, 256:512])
        put_heads(vb_ref, acc[:, 512:768])
    acc = _dot(h, w_ref[:, C_QE:C_QE + 512])
    put(C_QE, rope_a(acc[:, 0:128]) * scale)
    put(C_QO, rope_a(acc[:, 128:256]) * scale)
    put(C_K, rope_a(acc[:, 256:384]))
    put(C_V, acc[:, 384:512])
    if prompt:
        put_heads(kc_ref, acc[:, 256:384])
        put_heads(vc_ref, acc[:, 384:512])
    acc = _dot(h, w_ref[:, D_Q:D_Q + 768])
    put(D_Q, acc[:, 0:256] * scale)
    put(D_K, acc[:, 256:512])
    put(D_V, acc[:, 512:768])
    if prompt:
        put_heads(kd_ref, acc[:, 256:512])
        put_heads(vd_ref, acc[:, 512:768])
    for j in range(GATE_W // 512):
        gates_ref[:, j * 512:(j + 1) * 512] = _dot(h, w_ref[:, QKV_W + j * 512:QKV_W + (j + 1) * 512]).astype(BF16)


def _pre_call(l, x, mod, g1, w_in, gq, gk, bd128, rope_tabs, *, prompt):
    rows = x.shape[0]
    tm = SEQ if prompt else 512
    tq = tm
    if prompt:
        mod_row = lambda i: CTX_ROW
    else:
        mod_row = lambda i: i // (DEC_SEQ // tm)
    in_specs = [
        pl.BlockSpec((tm, D_MODEL), lambda i: (i, 0)),
        pl.BlockSpec((None, None, 1, D_MODEL), lambda i: (l, mod_row(i), 0, 0)),
        pl.BlockSpec((None, None, 1, D_MODEL), lambda i: (l, mod_row(i), 0, 1)),
        pl.BlockSpec((None, 1, D_MODEL), lambda i: (l, 0, 0)),
        pl.BlockSpec((None, D_MODEL, IN_COLS), lambda i: (l, 0, 0), pipeline_mode=pl.Buffered(1)),
        pl.BlockSpec((None, 1, LANES), lambda i: (l, 0, 0)),
        pl.BlockSpec((None, 1, LANES), lambda i: (l, 0, 0)),
        pl.BlockSpec((LANES, LANES), lambda i: (0, 0)),
    ]
    args = [x, mod, mod, g1, w_in, gq, gk, bd128]
    out_shape = [jax.ShapeDtypeStruct((rows, QKV_W), BF16), jax.ShapeDtypeStruct((rows, GATE_W), BF16)]
    out_specs = [pl.BlockSpec((tm, QKV_W), lambda i: (i, 0)), pl.BlockSpec((tm, GATE_W), lambda i: (i, 0))]
    if prompt:
        for nh in (2, 2, 4, 4, 2, 2, 4, 4):
            out_shape.append(jax.ShapeDtypeStruct((BATCH, nh, SEQ, HEAD_DIM), F32))
            out_specs.append(pl.BlockSpec((None, nh, SEQ, HEAD_DIM), lambda i: (i, 0, 0, 0)))
    else:
        nt = DEC_SEQ // tq
        for t in rope_tabs:
            in_specs.append(pl.BlockSpec((tq, LANES), lambda i: (i % nt, 0)))
            args.append(t)
    return pl.pallas_call(
        functools.partial(_pre_kernel, prompt=prompt, tq=tq),
        out_shape=out_shape,
        grid=(rows // tm,),
        in_specs=in_specs,
        out_specs=out_specs,
        compiler_params=pltpu.CompilerParams(vmem_limit_bytes=VMEM_LIMIT),
        name="pre_prompt" if prompt else "pre_latent",
    )(*args)


def _scores(qs, keys, biases):
    out = []
    for k, b in zip(keys, biases):
        s = _dot_nt(qs, k)
        out.append(s if b is None else s + b)
    return out


def _row_max(s_list, sink):
    m = None
    for s in s_list:
        mi = jnp.max(s, axis=-1, keepdims=True)
        m = mi if m is None else jnp.maximum(m, mi)
    return m if sink is None else jnp.maximum(m, sink)


def _softmax_pv(s_list, vals, sink=None):
    m = _row_max(s_list, sink)
    l, r = None, None
    for s, v in zip(s_list, vals):
        p = jnp.exp(s - m)
        li = jnp.sum(p, axis=-1, keepdims=True)
        ri = _dot(p.astype(BF16), v)
        l = li if l is None else l + li
        r = ri if r is None else r + ri
    if sink is not None:
        l = l + jnp.exp(sink - m)
    return r / l


def _gqa(qe, qo, keys, vals, biases, sinks):
    mq = qe.shape[0]
    oe = jnp.zeros((mq, LANES), F32)
    oo = jnp.zeros((mq, LANES), F32)
    for g in range(2):
        lo, hi = HEAD_DIM * g, HEAD_DIM * (g + 1)
        qs = jnp.concatenate([_mask_q(qe, lo, hi), _mask_q(qo, lo, hi)], axis=0)
        sink = None
        if sinks is not None:
            row = lax.broadcasted_iota(jnp.int32, (2 * mq, 1), 0)
            sink = jnp.where(row < mq, sinks[2 * g], sinks[2 * g + 1])
        r = _softmax_pv(_scores(qs, keys, biases), vals, sink)
        msk = _lane_mask((mq, LANES), lo, hi)
        oe = jnp.where(msk, r[:mq], oe)
        oo = jnp.where(msk, r[mq:], oo)
    return oe, oo


def _diff(q, keys, vals, lam):
    mq = q.shape[0]
    out = jnp.zeros((mq, 4 * HEAD_DIM), F32)
    for hd in range(4):
        lo = HEAD_DIM * hd
        qs = jnp.concatenate([_mask_q(q, lo, lo + B_HALF), _mask_q(q, lo + B_HALF, lo + HEAD_DIM)], axis=0)
        s_list = _scores(qs, keys, [None] * len(keys))
        m = _row_max(s_list, None)
        p_list = [jnp.exp(s - m) for s in s_list]
        l = None
        for p in p_list:
            li = jnp.sum(p, axis=-1, keepdims=True)
            l = li if l is None else l + li
        rl = 1.0 / l
        r = None
        for p, v in zip(p_list, vals):
            pn = p * rl
            pd = pn[:mq] - lam * pn[mq:]
            ri = _dot(pd.astype(BF16), v)
            r = ri if r is None else r + ri
        out = jnp.where(_lane_mask(out.shape, lo, lo + HEAD_DIM), r, out)
    return out


def _mha(q, keys, vals, bias_fn):
    mq = q.shape[0]
    out = jnp.zeros((mq, 4 * HEAD_DIM), F32)
    for hd in range(4):
        lo = HEAD_DIM * hd
        qs = _mask_q(q, lo, lo + HEAD_DIM)
        r = _softmax_pv(_scores(qs, keys, bias_fn(hd)), vals)
        out = jnp.where(_lane_mask(out.shape, lo, lo + HEAD_DIM), r, out)
    return out


def _lambda(lam_ref, lam_init):
    lp = lam_ref[...]
    a = jnp.sum(lp[0:1, :] * lp[1:2, :], axis=-1, keepdims=True)
    b = jnp.sum(lp[2:3, :] * lp[3:4, :], axis=-1, keepdims=True)
    return jnp.exp(a) - jnp.exp(b) + lam_init


def _finish_diff(ob, bd_ref, gsub_ref, lam_init):
    return _head_rms(ob, bd_ref[...], gsub_ref[...], SUBLN_EPS) * (1.0 - lam_init)


def _store_branches(br_ref, oa, ob, oc, od):
    br_ref[:, 0:128] = oa[0].astype(BF16)
    br_ref[:, 128:256] = oa[1].astype(BF16)
    br_ref[:, 256:512] = ob.astype(BF16)
    br_ref[:, 512:640] = oc[0].astype(BF16)
    br_ref[:, 640:768] = oc[1].astype(BF16)
    br_ref[:, 768:1024] = od.astype(BF16)


def _attn_prompt_kernel(qkv_ref, sink_ref, lam_ref, gsub_ref, bd_ref, br_ref, *, l, lam_init):
    def cols(c, w):
        return qkv_ref[:, c:c + w]

    def qcols(c, w):
        return cols(c, w).astype(F32)

    sinks = [sink_ref[l, i] for i in range(4)]
    oa = _gqa(qcols(A_QE, 128), qcols(A_QO, 128), [cols(A_K, 128)], [cols(A_V, 128)], [None], None)
    lam = _lambda(lam_ref, lam_init)
    ob = _diff(qcols(B_Q, 256), [cols(B_K, 256)], [cols(B_V, 256)], lam)
    ob = _finish_diff(ob, bd_ref, gsub_ref, lam_init)
    oc = _gqa(qcols(C_QE, 128), qcols(C_QO, 128), [cols(C_K, 128)], [cols(C_V, 128)], [None], sinks)
    od = _mha(qcols(D_Q, 256), [cols(D_K, 256)], [cols(D_V, 256)], lambda hd: [None])
    _store_branches(br_ref, oa, ob, oc, od)


def _attn_prompt_call(l, lam_init, qkv, sink_c, lam_b, gsub, bd256):
    rows = qkv.shape[0]
    return pl.pallas_call(
        functools.partial(_attn_prompt_kernel, l=l, lam_init=lam_init),
        out_shape=jax.ShapeDtypeStruct((rows, D_MODEL), BF16),
        grid=(rows // SEQ,),
        in_specs=[
            pl.BlockSpec((SEQ, QKV_W), lambda b: (b, 0)),
            pl.BlockSpec(memory_space=pltpu.SMEM),
            pl.BlockSpec((None, 4, B_HALF), lambda b: (l, 0, 0)),
            pl.BlockSpec((None, 1, 4 * HEAD_DIM), lambda b: (l, 0, 0)),
            pl.BlockSpec((4 * HEAD_DIM, 4 * HEAD_DIM), lambda b: (0, 0)),
        ],
        out_specs=pl.BlockSpec((SEQ, D_MODEL), lambda b: (b, 0)),
        compiler_params=pltpu.CompilerParams(vmem_limit_bytes=VMEM_LIMIT),
        name="attn_prompt",
    )(qkv, sink_c, lam_b, gsub, bd256)


def _attn_latent_kernel(kv_ref, ctx_ref, bias_ref, sink_ref, lam_ref, gsub_ref, bd_ref, br_ref, *, l, lam_init):
    n = pl.program_id(1)
    q0 = pl.multiple_of(n * ATT_BLK, ATT_BLK)

    def q(c, w):
        return kv_ref[pl.ds(q0, ATT_BLK), c:c + w].astype(F32)

    def lat(c, w):
        return kv_ref[:, c:c + w]

    def ctx(c, w):
        return ctx_ref[:, c:c + w]

    sinks = [sink_ref[l, i] for i in range(4)]
    oa = _gqa(q(A_QE, 128), q(A_QO, 128), [ctx(X_AK, 128), lat(A_K, 128)], [ctx(X_AV, 128), lat(A_V, 128)],
              [None, None], None)
    lam = _lambda(lam_ref, lam_init)
    ob = _diff(q(B_Q, 256), [ctx(X_BK, 256), lat(B_K, 256)], [ctx(X_BV, 256), lat(B_V, 256)], lam)
    ob = _finish_diff(ob, bd_ref, gsub_ref, lam_init)
    start_c = pl.multiple_of(jnp.clip(q0 - C_WINDOW, 0, DEC_SEQ - C_WIN), C_WINDOW)
    rowq = lax.broadcasted_iota(jnp.int32, (2 * ATT_BLK, C_WIN), 0) & (ATT_BLK - 1)
    colk = lax.broadcasted_iota(jnp.int32, (2 * ATT_BLK, C_WIN), 1)
    band = jnp.where(jnp.abs(rowq - colk + (q0 - start_c)) <= C_WINDOW, 0.0, NEG_INF)
    oc = _gqa(q(C_QE, 128), q(C_QO, 128),
              [ctx(X_CK, 128), kv_ref[pl.ds(start_c, C_WIN), C_K:C_K + 128]],
              [ctx(X_CV, 128), kv_ref[pl.ds(start_c, C_WIN), C_V:C_V + 128]],
              [None, band], sinks)
    start_d = pl.multiple_of(jnp.where(n >= 2, NA_WIN_START[2] * GRID_W, 0), ATT_BLK)
    kwin = kv_ref[pl.ds(start_d, NA_WIN), D_K:D_K + 256]
    vwin = kv_ref[pl.ds(start_d, NA_WIN), D_V:D_V + 256]
    od = _mha(q(D_Q, 256), [ctx(X_DK, 256), kwin], [ctx(X_DV, 256), vwin],
              lambda hd: [None, bias_ref[hd, n]])
    _store_branches(br_ref, oa, ob, oc, od)


def _attn_latent_call(l, lam_init, qkv, ctx_dense, dbias, sink_c, lam_b, gsub, bd256):
    rows = qkv.shape[0]
    nblk = DEC_SEQ // ATT_BLK
    return pl.pallas_call(
        functools.partial(_attn_latent_kernel, l=l, lam_init=lam_init),
        out_shape=jax.ShapeDtypeStruct((rows, D_MODEL), BF16),
        grid=(DEC_BATCH, nblk),
        in_specs=[
            pl.BlockSpec((DEC_SEQ, QKV_W), lambda b, n: (b, 0)),
            pl.BlockSpec((None, None, PAST_LEN, CACHE_W), lambda b, n: (b, l, 0, 0)),
            pl.BlockSpec((None, 4, nblk, ATT_BLK, NA_WIN), lambda b, n: (l, 0, 0, 0, 0), pipeline_mode=pl.Buffered(1)),
            pl.BlockSpec(memory_space=pltpu.SMEM),
            pl.BlockSpec((None, 4, B_HALF), lambda b, n: (l, 0, 0)),
            pl.BlockSpec((None, 1, 4 * HEAD_DIM), lambda b, n: (l, 0, 0)),
            pl.BlockSpec((4 * HEAD_DIM, 4 * HEAD_DIM), lambda b, n: (0, 0)),
        ],
        out_specs=pl.BlockSpec((ATT_BLK, D_MODEL), lambda b, n: (b * nblk + n, 0)),
        compiler_params=pltpu.CompilerParams(vmem_limit_bytes=VMEM_LIMIT),
        name="attn_latent",
    )(qkv, ctx_dense, dbias, sink_c, lam_b, gsub, bd256)


FFN_CHUNK = D_FF // 2


def _post_kernel(x_ref, br_ref, gates_ref, gt1_ref, sh2_ref, sc2_ref, gt2_ref, g2_ref,
                 wb_ref, wo_ref, wfi_ref, wfo_ref, gf_ref, o_ref, *, final):
    merged = None
    for k in range(4):
        proj = _dot(br_ref[:, k * 256:(k + 1) * 256], wb_ref[k])
        t = _sigmoid(gates_ref[:, k * D_MODEL:(k + 1) * D_MODEL].astype(F32)) * proj
        merged = t if merged is None else merged + t
    x1 = x_ref[...] + gt1_ref[...] * _dot(merged.astype(BF16), wo_ref[...])
    h2 = (_rms(x1, g2_ref[...], NORM_EPS) * (1.0 + sc2_ref[...]) + sh2_ref[...]).astype(BF16)
    acc = None
    for c in range(D_FF // FFN_CHUNK):
        c0, c1 = c * FFN_CHUNK, (c + 1) * FFN_CHUNK
        a = _dot(h2, wfi_ref[:, c0:c1])
        u = _dot(h2, wfi_ref[:, D_FF + c0:D_FF + c1])
        g = ((a * _sigmoid(a)) * u).astype(BF16)
        t = _dot(g, wfo_ref[c0:c1, :])
        acc = t if acc is None else acc + t
    xo = x1 + gt2_ref[...] * acc
    if final:
        xo = _rms(xo, gf_ref[...], NORM_EPS)
    o_ref[...] = xo


def _post_call(l, x, br, gates, mod, g2, wb, wo, wfi, wfo, gf, *, prompt, final):
    rows = x.shape[0]
    tm = 256
    if prompt:
        mod_row = lambda i: CTX_ROW
    else:
        mod_row = lambda i: i // (DEC_SEQ // tm)

    def mod_spec(chunk):
        return pl.BlockSpec((None, None, 1, D_MODEL), lambda i: (l, mod_row(i), 0, chunk))

    def resident(shape):
        nd = len(shape)
        return pl.BlockSpec((None,) + shape, lambda i: (l,) + (0,) * nd, pipeline_mode=pl.Buffered(1))

    return pl.pallas_call(
        functools.partial(_post_kernel, final=final),
        out_shape=jax.ShapeDtypeStruct((rows, D_MODEL), F32),
        grid=(rows // tm,),
        in_specs=[
            pl.BlockSpec((tm, D_MODEL), lambda i: (i, 0)),
            pl.BlockSpec((tm, D_MODEL), lambda i: (i, 0)),
            pl.BlockSpec((tm, GATE_W), lambda i: (i, 0)),
            mod_spec(2), mod_spec(3), mod_spec(4), mod_spec(5),
            pl.BlockSpec((None, 1, D_MODEL), lambda i: (l, 0, 0)),
            resident((4, 256, D_MODEL)),
            resident((D_MODEL, D_MODEL)),
            resident((D_MODEL, 2 * D_FF)),
            resident((D_FF, D_MODEL)),
            pl.BlockSpec((1, D_MODEL), lambda i: (0, 0)),
        ],
        out_specs=pl.BlockSpec((tm, D_MODEL), lambda i: (i, 0)),
        compiler_params=pltpu.CompilerParams(vmem_limit_bytes=VMEM_LIMIT),
        name="post_prompt" if prompt else "post_latent",
    )(x, br, gates, mod, mod, mod, mod, g2, wb, wo, wfi, wfo, gf)


def _gqa_head_perm():
    return np.concatenate([np.arange(64) + 64 * hh for hh in (0, 2, 1, 3)])


def _in_col_perm():
    perm = np.arange(IN_COLS)
    hp = _gqa_head_perm()
    perm[0:256] = hp
    perm[1280:1536] = 1280 + hp
    return perm


def _rope_tables():
    t = jnp.arange(DEC_SEQ)
    row = (t // GRID_W).astype(F32)[:, None]
    col = (t % GRID_W).astype(F32)[:, None]
    tabs = []
    for d in (HEAD_DIM, B_HALF):
        quarter = d // 4
        inv = jnp.power(jnp.float32(ROPE_THETA), -jnp.arange(quarter, dtype=F32) / quarter)
        ar, ac = row * inv, col * inv
        cos = jnp.concatenate([jnp.cos(ar), jnp.cos(ar), jnp.cos(ac), jnp.cos(ac)], axis=-1)
        sin = jnp.concatenate([-jnp.sin(ar), jnp.sin(ar), -jnp.sin(ac), jnp.sin(ac)], axis=-1)
        reps = LANES // d
        tabs += [jnp.tile(cos, (1, reps)), jnp.tile(sin, (1, reps))]
    return tabs


def _block_diag_mean(width):
    idx = np.arange(width) // HEAD_DIM
    return jnp.asarray((idx[:, None] == idx[None, :]).astype(np.float32) / HEAD_DIM, dtype=BF16)


def _dense_cache(c):
    b, dpt, nh, s, d = c.shape
    return jnp.transpose(c, (0, 1, 3, 2, 4)).reshape(b, dpt, s, nh * d)


def kernel(x_prompt, x_sample, cache_a_k, cache_a_v, cache_b_k, cache_b_v, cache_c_k, cache_c_v, cache_d_k, cache_d_v, c, c_ctx, w_ada, b_ada, g_norm1, w_in, g_q_a, g_k_a, lam_b, g_subln_b, sink_c, rpb_d, w_branch, w_out, g_norm2, w_ffn_in, w_ffn_out, g_final):
    w_in_p = jnp.take(w_in, jnp.asarray(_in_col_perm()), axis=2).astype(BF16)
    hp = jnp.asarray(_gqa_head_perm())
    wb = jnp.stack([jnp.take(w_branch[:, 0], hp, axis=1), w_branch[:, 1],
                    jnp.take(w_branch[:, 2], hp, axis=1), w_branch[:, 3]], axis=1).astype(BF16)
    wo = w_out.astype(BF16)
    wfi = w_ffn_in.astype(BF16)
    wfo = w_ffn_out.astype(BF16)
    gq = jnp.tile(g_q_a, (1, 2)).reshape(DEPTH, 1, LANES)
    gk = jnp.tile(g_k_a, (1, 2)).reshape(DEPTH, 1, LANES)
    gsub = jnp.tile(g_subln_b, (1, 4)).reshape(DEPTH, 1, 4 * HEAD_DIM)
    g1 = g_norm1.reshape(DEPTH, 1, D_MODEL)
    g2 = g_norm2.reshape(DEPTH, 1, D_MODEL)
    gf = g_final.reshape(1, D_MODEL)
    bd128 = _block_diag_mean(LANES)
    bd256 = _block_diag_mean(4 * HEAD_DIM)
    rope_tabs = _rope_tables()
    ctx_dense = jnp.concatenate(
        [_dense_cache(t) for t in (cache_a_k, cache_a_v, cache_b_k, cache_b_v,
                                   cache_c_k, cache_c_v, cache_d_k, cache_d_v)], axis=-1).astype(BF16)

    cond = jnp.concatenate([c, c_ctx[None, :], jnp.zeros((MOD_ROWS - DEC_BATCH - 1, D_MODEL), F32)], axis=0)
    mod = _ada_call(cond, w_ada, b_ada).reshape(DEPTH, MOD_ROWS, 1, 6 * D_MODEL)
    dbias = _dbias_call(rpb_d)

    xp = x_prompt.reshape(BATCH * SEQ, D_MODEL)
    xs = x_sample.reshape(DEC_BATCH * DEC_SEQ, D_MODEL)
    new_kv = []
    for l in range(DEPTH):
        lam_init = 0.8 - 0.6 * math.exp(-0.3 * l)
        final = l == DEPTH - 1
        outs = _pre_call(l, xp, mod, g1, w_in_p, gq, gk, bd128, None, prompt=True)
        qkv_p, gates_p = outs[0], outs[1]
        new_kv.append(outs[2:])
        br_p = _attn_prompt_call(l, lam_init, qkv_p, sink_c, lam_b, gsub, bd256)
        xp = _post_call(l, xp, br_p, gates_p, mod, g2, wb, wo, wfi, wfo, gf, prompt=True, final=final)

        qkv_s, gates_s = _pre_call(l, xs, mod, g1, w_in_p, gq, gk, bd128, rope_tabs, prompt=False)
        br_s = _attn_latent_call(l, lam_init, qkv_s, ctx_dense, dbias, sink_c, lam_b, gsub, bd256)
        xs = _post_call(l, xs, br_s, gates_s, mod, g2, wb, wo, wfi, wfo, gf, prompt=False, final=final)

    y_prompt = xp.reshape(BATCH, SEQ, D_MODEL)
    y_sample = xs.reshape(DEC_BATCH, DEC_SEQ, D_MODEL)
    new = [jnp.stack([new_kv[l][i] for l in range(DEPTH)], axis=1) for i in range(8)]
    return (y_prompt, y_sample, new[0], new[1], new[2], new[3], new[4], new[5], new[6], new[7])
```

```python
import functools
import math

import numpy as np
import jax
import jax.numpy as jnp
from jax import lax
from jax.experimental import pallas as pl
from jax.experimental.pallas import tpu as pltpu

F32 = jnp.float32
BF16 = jnp.bfloat16

D_MODEL = 1024
BATCH = 32
SEQ = 256
DEPTH = 2
DEC_BATCH = 8
DEC_SEQ = 1024
PAST_LEN = 256
GRID_W = 64
HEAD_DIM = 64
B_HALF = HEAD_DIM // 2
C_WINDOW = 128
NA_ROWS = 8
NA_COLS = 16
D_FF = 2816
ROPE_THETA = 10000.0
NORM_EPS = 1e-6
SUBLN_EPS = 1e-5
NEG_INF = -1e30

QKV_W = 2560
GATE_W = 4 * D_MODEL
IN_COLS = QKV_W + GATE_W
MOD_ROWS = 16
CTX_ROW = DEC_BATCH
LANES = 128
ATT_BLK = 256
NA_BLK_ROWS = ATT_BLK // GRID_W
NA_WIN_ROWS = 12
NA_WIN = NA_WIN_ROWS * GRID_W
NA_WIN_START = (0, 0, 4, 4)
C_WIN = ATT_BLK + 2 * C_WINDOW
VMEM_LIMIT = 56 * 1024 * 1024

A_QE, A_QO, A_K, A_V = 0, 128, 256, 384
B_Q, B_K, B_V = 512, 768, 1024
C_QE, C_QO, C_K, C_V = 1280, 1408, 1536, 1664
D_Q, D_K, D_V = 1792, 2048, 2304
X_AK, X_AV, X_BK, X_BV, X_CK, X_CV, X_DK, X_DV = 0, 128, 256, 512, 768, 896, 1024, 1280
CACHE_W = 1536


def _dot(a, b):
    return jnp.dot(a, b, preferred_element_type=F32)


def _dot_nt(a, b):
    return lax.dot_general(a, b, (((1,), (1,)), ((), ())), preferred_element_type=F32)


def _sigmoid(x):
    return 0.5 * jnp.tanh(0.5 * x) + 0.5


def _rms(x, g, eps):
    ms = jnp.mean(x * x, axis=-1, keepdims=True)
    return x * lax.rsqrt(ms + eps) * g


def _head_rms(v, bd, g, eps):
    v2 = v * v
    hi = v2.astype(BF16)
    lo = (v2 - hi.astype(F32)).astype(BF16)
    ms = _dot(hi, bd) + _dot(lo, bd)
    return v * lax.rsqrt(ms + eps) * g


def _lane_mask(shape, lo, hi):
    lane = lax.broadcasted_iota(jnp.int32, shape, 1)
    return (lane >= lo) & (lane < hi)


def _mask_q(qf, lo, hi):
    return jnp.where(_lane_mask(qf.shape, lo, hi), qf, 0.0).astype(BF16)


def _ada_kernel(cond_ref, w_ref, b_ref, o_ref):
    c = cond_ref[...]
    s = (c * _sigmoid(c)).astype(BF16)
    o_ref[...] = _dot(s, w_ref[...].astype(BF16)) + b_ref[...]


def _ada_call(cond, w_ada, b_ada):
    tn = 1536
    return pl.pallas_call(
        _ada_kernel,
        out_shape=jax.ShapeDtypeStruct((DEPTH, MOD_ROWS, 6 * D_MODEL), F32),
        grid=(DEPTH, 6 * D_MODEL // tn),
        in_specs=[
            pl.BlockSpec((MOD_ROWS, D_MODEL), lambda l, j: (0, 0)),
            pl.BlockSpec((None, D_MODEL, tn), lambda l, j: (l, 0, j)),
            pl.BlockSpec((None, 1, tn), lambda l, j: (l, 0, j)),
        ],
        out_specs=pl.BlockSpec((None, MOD_ROWS, tn), lambda l, j: (l, 0, j)),
        compiler_params=pltpu.CompilerParams(vmem_limit_bytes=VMEM_LIMIT),
        name="ada",
    )(cond, w_ada, b_ada.reshape(DEPTH, 1, 6 * D_MODEL))


def _dbias_kernel(rpb_ref, o_ref):
    n_dr, n_dc = 2 * NA_ROWS - 1, 2 * NA_COLS - 1
    base = (pl.program_id(0) * 4 + pl.program_id(1)) * (n_dr * n_dc)
    shape = (GRID_W, LANES)
    cq = lax.broadcasted_iota(jnp.int32, shape, 0)
    lane = lax.broadcasted_iota(jnp.int32, shape, 1)
    ck = lane & (GRID_W - 1)
    dc = jnp.clip(ck - cq, -(NA_COLS - 1), NA_COLS - 1) + (NA_COLS - 1)
    start_c = jnp.clip(cq - NA_COLS // 2, 0, GRID_W - NA_COLS)
    col_valid = (ck >= start_c) & (ck < start_c + NA_COLS)
    neg = jnp.full(shape, NEG_INF, F32)
    toeplitz = []
    for dr in range(n_dr):
        t = jnp.zeros(shape, F32)
        for m in range(n_dc):
            t = jnp.where(dc == m, rpb_ref[base + dr * n_dc + m], t)
        toeplitz.append(jnp.where(col_valid, t, neg))
    left = lane < GRID_W
    rows = DEC_SEQ // GRID_W
    for n in range(DEC_SEQ // ATT_BLK):
        for rq in range(NA_BLK_ROWS):
            r = NA_BLK_ROWS * n + rq
            start_r = min(max(r - NA_ROWS // 2, 0), rows - NA_ROWS)
            for jp in range(NA_WIN_ROWS // 2):
                pair = []
                for j in (2 * jp, 2 * jp + 1):
                    key_row = NA_WIN_START[n] + j
                    valid = start_r <= key_row < start_r + NA_ROWS
                    pair.append(toeplitz[key_row - r + NA_ROWS - 1] if valid else neg)
                tile = pair[0] if pair[0] is pair[1] else jnp.where(left, pair[0], pair[1])
                o_ref[n, rq * GRID_W:(rq + 1) * GRID_W, jp * LANES:(jp + 1) * LANES] = tile


def _dbias_call(rpb_d):
    nblk = DEC_SEQ // ATT_BLK
    return pl.pallas_call(
        _dbias_kernel,
        out_shape=jax.ShapeDtypeStruct((DEPTH, 4, nblk, ATT_BLK, NA_WIN), F32),
        grid=(DEPTH, 4),
        in_specs=[pl.BlockSpec(memory_space=pltpu.SMEM)],
        out_specs=pl.BlockSpec((None, None, nblk, ATT_BLK, NA_WIN), lambda l, h: (l, h, 0, 0, 0)),
        name="dbias",
    )(rpb_d.reshape(-1))


def _swap_halves(v, half):
    lane = lax.broadcasted_iota(jnp.int32, v.shape, 1)
    up = pltpu.roll(v, LANES - half, 1)
    dn = pltpu.roll(v, half, 1)
    return jnp.where((lane & (2 * half - 1)) < half, up, dn)


def _rope(v, cos, sin, half):
    return v * cos + _swap_halves(v, half) * sin


def _pre_kernel(*refs, prompt, first):
    if prompt:
        (x_ref, sh_ref, sc_ref, g1_ref, w_ref, gq_ref, gk_ref, bd_ref) = refs[:8]
        (qkv_ref, gates_ref, ka_ref, va_ref, kb_ref, vb_ref, kc_ref, vc_ref, kd_ref, vd_ref) = refs[-10:]
    else:
        (x_ref, sh_ref, sc_ref, g1_ref, w_ref, gq_ref, gk_ref, bd_ref,
         ca_ref, sa_ref, cb_ref, sb_ref, qkv_ref, gates_ref) = refs

    h = (_rms(x_ref[...], g1_ref[...], NORM_EPS) * (1.0 + sc_ref[...]) + sh_ref[...]).astype(BF16)
    bd = bd_ref[...]

    def rope_a(v):
        return v if prompt else _rope(v, ca_ref[...], sa_ref[...], 16)

    def rope_b(v):
        return v if prompt else _rope(v, cb_ref[...], sb_ref[...], 8)

    def put(col, v):
        qkv_ref[:, col:col + v.shape[1]] = v.astype(BF16)

    def put_heads(ref, v):
        for hh in range(v.shape[1] // HEAD_DIM):
            piece = v[:, hh * HEAD_DIM:(hh + 1) * HEAD_DIM]
            if first:
                ref[0, hh, :, :] = piece
                for later in range(1, DEPTH):
                    ref[later, hh, :, :] = jnp.zeros_like(piece)
            else:
                ref[hh, :, :] = piece

    scale = HEAD_DIM ** -0.5
    acc = _dot(h, w_ref[:, A_QE:A_QE + 512])
    gq = gq_ref[...]
    put(A_QE, rope_a(_head_rms(acc[:, 0:128], bd, gq, NORM_EPS)) * scale)
    put(A_QO, rope_a(_head_rms(acc[:, 128:256], bd, gq, NORM_EPS)) * scale)
    k_a = _head_rms(acc[:, 256:384], bd, gk_ref[...], NORM_EPS)
    put(A_K, rope_a(k_a))
    put(A_V, acc[:, 384:512])
    if prompt:
        put_heads(ka_ref, k_a)
        put_heads(va_ref, acc[:, 384:512])
    acc = _dot(h, w_ref[:, B_Q:B_Q + 768])
    scale_b = B_HALF ** -0.5
    for c in range(2):
        put(B_Q + c * 128, rope_b(acc[:, c * 128:(c + 1) * 128]) * scale_b)
        put(B_K + c * 128, rope_b(acc[:, 256 + c * 128:256 + (c + 1) * 128]))
    put(B_V, acc[:, 512:768])
    if prompt:
        put_heads(kb_ref, acc[:, 256:512])
        put_heads(vb_ref, acc[:, 512:768])
    acc = _dot(h, w_ref[:, C_QE:C_QE + 512])
    put(C_QE, rope_a(acc[:, 0:128]) * scale)
    put(C_QO, rope_a(acc[:, 128:256]) * scale)
    put(C_K, rope_a(acc[:, 256:384]))
    put(C_V, acc[:, 384:512])
    if prompt:
        put_heads(kc_ref, acc[:, 256:384])
        put_heads(vc_ref, acc[:, 384:512])
    acc = _dot(h, w_ref[:, D_Q:D_Q + 768])
    put(D_Q, acc[:, 0:256] * scale)
    put(D_K, acc[:, 256:512])
    put(D_V, acc[:, 512:768])
    if prompt:
        put_heads(kd_ref, acc[:, 256:512])
        put_heads(vd_ref, acc[:, 512:768])
    for j in range(GATE_W // 512):
        gates_ref[:, j * 512:(j + 1) * 512] = _dot(h, w_ref[:, QKV_W + j * 512:QKV_W + (j + 1) * 512]).astype(BF16)


def _pre_call(l, x, mod, g1, w_in, gq, gk, bd128, rope_tabs, kv_prev, *, prompt):
    rows = x.shape[0]
    tm = SEQ if prompt else 512
    tq = tm
    aliases = {}
    if prompt:
        mod_row = lambda i: CTX_ROW
    else:
        mod_row = lambda i: i // (DEC_SEQ // tm)
    in_specs = [
        pl.BlockSpec((tm, D_MODEL), lambda i: (i, 0)),
        pl.BlockSpec((None, None, 1, D_MODEL), lambda i: (l, mod_row(i), 0, 0)),
        pl.BlockSpec((None, None, 1, D_MODEL), lambda i: (l, mod_row(i), 0, 1)),
        pl.BlockSpec((None, 1, D_MODEL), lambda i: (l, 0, 0)),
        pl.BlockSpec((None, D_MODEL, IN_COLS), lambda i: (l, 0, 0), pipeline_mode=pl.Buffered(1)),
        pl.BlockSpec((None, 1, LANES), lambda i: (l, 0, 0)),
        pl.BlockSpec((None, 1, LANES), lambda i: (l, 0, 0)),
        pl.BlockSpec((LANES, LANES), lambda i: (0, 0)),
    ]
    args = [x, mod, mod, g1, w_in, gq, gk, bd128]
    out_shape = [jax.ShapeDtypeStruct((rows, QKV_W), BF16), jax.ShapeDtypeStruct((rows, GATE_W), BF16)]
    out_specs = [pl.BlockSpec((tm, QKV_W), lambda i: (i, 0)), pl.BlockSpec((tm, GATE_W), lambda i: (i, 0))]
    if prompt:
        for j, nh in enumerate((2, 2, 4, 4, 2, 2, 4, 4)):
            out_shape.append(jax.ShapeDtypeStruct((BATCH, DEPTH, nh, SEQ, HEAD_DIM), F32))
            if kv_prev is None:
                out_specs.append(pl.BlockSpec((None, DEPTH, nh, SEQ, HEAD_DIM), lambda i: (i, 0, 0, 0, 0)))
            else:
                out_specs.append(pl.BlockSpec((None, None, nh, SEQ, HEAD_DIM), lambda i: (i, l, 0, 0, 0)))
                aliases[len(args)] = 2 + j
                in_specs.append(pl.BlockSpec(memory_space=pl.ANY))
                args.append(kv_prev[j])
    else:
        nt = DEC_SEQ // tq
        for t in rope_tabs:
            in_specs.append(pl.BlockSpec((tq, LANES), lambda i: (i % nt, 0)))
            args.append(t)
    return pl.pallas_call(
        functools.partial(_pre_kernel, prompt=prompt, first=kv_prev is None),
        out_shape=out_shape,
        grid=(rows // tm,),
        in_specs=in_specs,
        out_specs=out_specs,
        input_output_aliases=aliases,
        compiler_params=pltpu.CompilerParams(vmem_limit_bytes=VMEM_LIMIT),
        name="pre_prompt" if prompt else "pre_latent",
    )(*args)


def _scores(qs, keys, biases):
    out = []
    for k, b in zip(keys, biases):
        s = _dot_nt(qs, k)
        out.append(s if b is None else s + b)
    return out


def _row_max(s_list, sink):
    m = None
    for s in s_list:
        mi = jnp.max(s, axis=-1, keepdims=True)
        m = mi if m is None else jnp.maximum(m, mi)
    return m if sink is None else jnp.maximum(m, sink)


def _softmax_pv(s_list, vals, sink=None):
    m = _row_max(s_list, sink)
    l, r = None, None
    for s, v in zip(s_list, vals):
        p = jnp.exp(s - m)
        li = jnp.sum(p, axis=-1, keepdims=True)
        ri = _dot(p.astype(BF16), v)
        l = li if l is None else l + li
        r = ri if r is None else r + ri
    if sink is not None:
        l = l + jnp.exp(sink - m)
    return r / l


def _gqa(qe, qo, keys, vals, biases, sinks):
    mq = qe.shape[0]
    oe = jnp.zeros((mq, LANES), F32)
    oo = jnp.zeros((mq, LANES), F32)
    for g in range(2):
        lo, hi = HEAD_DIM * g, HEAD_DIM * (g + 1)
        qs = jnp.concatenate([_mask_q(qe, lo, hi), _mask_q(qo, lo, hi)], axis=0)
        sink = None
        if sinks is not None:
            row = lax.broadcasted_iota(jnp.int32, (2 * mq, 1), 0)
            sink = jnp.where(row < mq, sinks[2 * g], sinks[2 * g + 1])
        r = _softmax_pv(_scores(qs, keys, biases), vals, sink)
        msk = _lane_mask((mq, LANES), lo, hi)
        oe = jnp.where(msk, r[:mq], oe)
        oo = jnp.where(msk, r[mq:], oo)
    return oe, oo


def _diff(q, keys, vals, lam):
    mq = q.shape[0]
    out = jnp.zeros((mq, 4 * HEAD_DIM), F32)
    for hd in range(4):
        lo = HEAD_DIM * hd
        qs = jnp.concatenate([_mask_q(q, lo, lo + B_HALF), _mask_q(q, lo + B_HALF, lo + HEAD_DIM)], axis=0)
        s_list = _scores(qs, keys, [None] * len(keys))
        m = _row_max(s_list, None)
        p_list = [jnp.exp(s - m) for s in s_list]
        l = None
        for p in p_list:
            li = jnp.sum(p, axis=-1, keepdims=True)
            l = li if l is None else l + li
        rl = 1.0 / l
        r = None
        for p, v in zip(p_list, vals):
            pn = p * rl
            pd = pn[:mq] - lam * pn[mq:]
            ri = _dot(pd.astype(BF16), v)
            r = ri if r is None else r + ri
        out = jnp.where(_lane_mask(out.shape, lo, lo + HEAD_DIM), r, out)
    return out


def _mha(q, keys, vals, bias_fn):
    mq = q.shape[0]
    out = jnp.zeros((mq, 4 * HEAD_DIM), F32)
    for hd in range(4):
        lo = HEAD_DIM * hd
        qs = _mask_q(q, lo, lo + HEAD_DIM)
        r = _softmax_pv(_scores(qs, keys, bias_fn(hd)), vals)
        out = jnp.where(_lane_mask(out.shape, lo, lo + HEAD_DIM), r, out)
    return out


def _lambda(lam_ref, lam_init):
    lp = lam_ref[...]
    a = jnp.sum(lp[0:1, :] * lp[1:2, :], axis=-1, keepdims=True)
    b = jnp.sum(lp[2:3, :] * lp[3:4, :], axis=-1, keepdims=True)
    return jnp.exp(a) - jnp.exp(b) + lam_init


def _finish_diff(ob, bd_ref, gsub_ref, lam_init):
    return _head_rms(ob, bd_ref[...], gsub_ref[...], SUBLN_EPS) * (1.0 - lam_init)


def _store_branches(br_ref, oa, ob, oc, od):
    br_ref[:, 0:128] = oa[0].astype(BF16)
    br_ref[:, 128:256] = oa[1].astype(BF16)
    br_ref[:, 256:512] = ob.astype(BF16)
    br_ref[:, 512:640] = oc[0].astype(BF16)
    br_ref[:, 640:768] = oc[1].astype(BF16)
    br_ref[:, 768:1024] = od.astype(BF16)


def _attn_prompt_kernel(qkv_ref, sink_ref, lam_ref, gsub_ref, bd_ref, br_ref, *, l, lam_init):
    def cols(c, w):
        return qkv_ref[:, c:c + w]

    def qcols(c, w):
        return cols(c, w).astype(F32)

    sinks = [sink_ref[l, i] for i in range(4)]
    oa = _gqa(qcols(A_QE, 128), qcols(A_QO, 128), [cols(A_K, 128)], [cols(A_V, 128)], [None], None)
    lam = _lambda(lam_ref, lam_init)
    ob = _diff(qcols(B_Q, 256), [cols(B_K, 256)], [cols(B_V, 256)], lam)
    ob = _finish_diff(ob, bd_ref, gsub_ref, lam_init)
    oc = _gqa(qcols(C_QE, 128), qcols(C_QO, 128), [cols(C_K, 128)], [cols(C_V, 128)], [None], sinks)
    od = _mha(qcols(D_Q, 256), [cols(D_K, 256)], [cols(D_V, 256)], lambda hd: [None])
    _store_branches(br_ref, oa, ob, oc, od)


def _attn_prompt_call(l, lam_init, qkv, sink_c, lam_b, gsub, bd256):
    rows = qkv.shape[0]
    return pl.pallas_call(
        functools.partial(_attn_prompt_kernel, l=l, lam_init=lam_init),
        out_shape=jax.ShapeDtypeStruct((rows, D_MODEL), BF16),
        grid=(rows // SEQ,),
        in_specs=[
            pl.BlockSpec((SEQ, QKV_W), lambda b: (b, 0)),
            pl.BlockSpec(memory_space=pltpu.SMEM),
            pl.BlockSpec((None, 4, B_HALF), lambda b: (l, 0, 0)),
            pl.BlockSpec((None, 1, 4 * HEAD_DIM), lambda b: (l, 0, 0)),
            pl.BlockSpec((4 * HEAD_DIM, 4 * HEAD_DIM), lambda b: (0, 0)),
        ],
        out_specs=pl.BlockSpec((SEQ, D_MODEL), lambda b: (b, 0)),
        compiler_params=pltpu.CompilerParams(vmem_limit_bytes=VMEM_LIMIT),
        name="attn_prompt",
    )(qkv, sink_c, lam_b, gsub, bd256)


def _attn_latent_kernel(kv_ref, xak_ref, xav_ref, xbk_ref, xbv_ref, xck_ref, xcv_ref, xdk_ref, xdv_ref,
                        bias_ref, sink_ref, lam_ref, gsub_ref, bd_ref, br_ref, ctx_ref, *, l, lam_init):
    n = pl.program_id(1)
    q0 = pl.multiple_of(n * ATT_BLK, ATT_BLK)

    @pl.when(n == 0)
    def _():
        for col, ref in ((X_AK, xak_ref), (X_AV, xav_ref), (X_BK, xbk_ref), (X_BV, xbv_ref),
                         (X_CK, xck_ref), (X_CV, xcv_ref), (X_DK, xdk_ref), (X_DV, xdv_ref)):
            for hh in range(ref.shape[0]):
                ctx_ref[:, col + hh * HEAD_DIM:col + (hh + 1) * HEAD_DIM] = ref[hh]

    def q(c, w):
        return kv_ref[pl.ds(q0, ATT_BLK), c:c + w].astype(F32)

    def lat(c, w):
        return kv_ref[:, c:c + w]

    def ctx(c, w):
        return ctx_ref[:, c:c + w].astype(BF16)

    sinks = [sink_ref[l, i] for i in range(4)]
    oa = _gqa(q(A_QE, 128), q(A_QO, 128), [ctx(X_AK, 128), lat(A_K, 128)], [ctx(X_AV, 128), lat(A_V, 128)],
              [None, None], None)
    lam = _lambda(lam_ref, lam_init)
    ob = _diff(q(B_Q, 256), [ctx(X_BK, 256), lat(B_K, 256)], [ctx(X_BV, 256), lat(B_V, 256)], lam)
    ob = _finish_diff(ob, bd_ref, gsub_ref, lam_init)
    start_c = pl.multiple_of(jnp.clip(q0 - C_WINDOW, 0, DEC_SEQ - C_WIN), C_WINDOW)
    rowq = lax.broadcasted_iota(jnp.int32, (2 * ATT_BLK, C_WIN), 0) & (ATT_BLK - 1)
    colk = lax.broadcasted_iota(jnp.int32, (2 * ATT_BLK, C_WIN), 1)
    band = jnp.where(jnp.abs(rowq - colk + (q0 - start_c)) <= C_WINDOW, 0.0, NEG_INF)
    oc = _gqa(q(C_QE, 128), q(C_QO, 128),
              [ctx(X_CK, 128), kv_ref[pl.ds(start_c, C_WIN), C_K:C_K + 128]],
              [ctx(X_CV, 128), kv_ref[pl.ds(start_c, C_WIN), C_V:C_V + 128]],
              [None, band], sinks)
    start_d = pl.multiple_of(jnp.where(n >= 2, NA_WIN_START[2] * GRID_W, 0), ATT_BLK)
    kwin = kv_ref[pl.ds(start_d, NA_WIN), D_K:D_K + 256]
    vwin = kv_ref[pl.ds(start_d, NA_WIN), D_V:D_V + 256]
    od = _mha(q(D_Q, 256), [ctx(X_DK, 256), kwin], [ctx(X_DV, 256), vwin],
              lambda hd: [None, bias_ref[hd, n]])
    _store_branches(br_ref, oa, ob, oc, od)


def _attn_latent_call(l, lam_init, qkv, caches, dbias, sink_c, lam_b, gsub, bd256):
    rows = qkv.shape[0]
    nblk = DEC_SEQ // ATT_BLK
    cache_specs = [pl.BlockSpec((None, None, t.shape[2], PAST_LEN, HEAD_DIM), lambda b, n: (b, l, 0, 0, 0))
                   for t in caches]
    return pl.pallas_call(
        functools.partial(_attn_latent_kernel, l=l, lam_init=lam_init),
        out_shape=jax.ShapeDtypeStruct((rows, D_MODEL), BF16),
        grid=(DEC_BATCH, nblk),
        in_specs=[
            pl.BlockSpec((DEC_SEQ, QKV_W), lambda b, n: (b, 0)),
            *cache_specs,
            pl.BlockSpec((None, 4, nblk, ATT_BLK, NA_WIN), lambda b, n: (l, 0, 0, 0, 0), pipeline_mode=pl.Buffered(1)),
            pl.BlockSpec(memory_space=pltpu.SMEM),
            pl.BlockSpec((None, 4, B_HALF), lambda b, n: (l, 0, 0)),
            pl.BlockSpec((None, 1, 4 * HEAD_DIM), lambda b, n: (l, 0, 0)),
            pl.BlockSpec((4 * HEAD_DIM, 4 * HEAD_DIM), lambda b, n: (0, 0)),
        ],
        out_specs=pl.BlockSpec((ATT_BLK, D_MODEL), lambda b, n: (b * nblk + n, 0)),
        scratch_shapes=[pltpu.VMEM((PAST_LEN, CACHE_W), F32)],
        compiler_params=pltpu.CompilerParams(dimension_semantics=("arbitrary", "arbitrary"),
                                             vmem_limit_bytes=VMEM_LIMIT),
        name="attn_latent",
    )(qkv, *caches, dbias, sink_c, lam_b, gsub, bd256)


FFN_CHUNK = D_FF // 2


def _post_kernel(x_ref, br_ref, gates_ref, gt1_ref, sh2_ref, sc2_ref, gt2_ref, g2_ref,
                 wb_ref, wo_ref, wfi_ref, wfo_ref, gf_ref, o_ref, *, final):
    merged = None
    for k in range(4):
        proj = _dot(br_ref[:, k * 256:(k + 1) * 256], wb_ref[k])
        t = _sigmoid(gates_ref[:, k * D_MODEL:(k + 1) * D_MODEL].astype(F32)) * proj
        merged = t if merged is None else merged + t
    x1 = x_ref[...] + gt1_ref[...] * _dot(merged.astype(BF16), wo_ref[...])
    h2 = (_rms(x1, g2_ref[...], NORM_EPS) * (1.0 + sc2_ref[...]) + sh2_ref[...]).astype(BF16)
    acc = None
    for c in range(D_FF // FFN_CHUNK):
        c0, c1 = c * FFN_CHUNK, (c + 1) * FFN_CHUNK
        a = _dot(h2, wfi_ref[:, c0:c1])
        u = _dot(h2, wfi_ref[:, D_FF + c0:D_FF + c1])
        g = ((a * _sigmoid(a)) * u).astype(BF16)
        t = _dot(g, wfo_ref[c0:c1, :])
        acc = t if acc is None else acc + t
    xo = x1 + gt2_ref[...] * acc
    if final:
        xo = _rms(xo, gf_ref[...], NORM_EPS)
    o_ref[...] = xo


def _post_call(l, x, br, gates, mod, g2, wb, wo, wfi, wfo, gf, *, prompt, final):
    rows = x.shape[0]
    tm = 256
    if prompt:
        mod_row = lambda i: CTX_ROW
    else:
        mod_row = lambda i: i // (DEC_SEQ // tm)

    def mod_spec(chunk):
        return pl.BlockSpec((None, None, 1, D_MODEL), lambda i: (l, mod_row(i), 0, chunk))

    def resident(shape):
        nd = len(shape)
        return pl.BlockSpec((None,) + shape, lambda i: (l,) + (0,) * nd, pipeline_mode=pl.Buffered(1))

    return pl.pallas_call(
        functools.partial(_post_kernel, final=final),
        out_shape=jax.ShapeDtypeStruct((rows, D_MODEL), F32),
        grid=(rows // tm,),
        in_specs=[
            pl.BlockSpec((tm, D_MODEL), lambda i: (i, 0)),
            pl.BlockSpec((tm, D_MODEL), lambda i: (i, 0)),
            pl.BlockSpec((tm, GATE_W), lambda i: (i, 0)),
            mod_spec(2), mod_spec(3), mod_spec(4), mod_spec(5),
            pl.BlockSpec((None, 1, D_MODEL), lambda i: (l, 0, 0)),
            resident((4, 256, D_MODEL)),
            resident((D_MODEL, D_MODEL)),
            resident((D_MODEL, 2 * D_FF)),
            resident((D_FF, D_MODEL)),
            pl.BlockSpec((1, D_MODEL), lambda i: (0, 0)),
        ],
        out_specs=pl.BlockSpec((tm, D_MODEL), lambda i: (i, 0)),
        compiler_params=pltpu.CompilerParams(vmem_limit_bytes=VMEM_LIMIT),
        name="post_prompt" if prompt else "post_latent",
    )(x, br, gates, mod, mod, mod, mod, g2, wb, wo, wfi, wfo, gf)


def _gqa_head_order(w, axis, start):
    pieces = [lax.slice_in_dim(w, 0, start, axis=axis)] if start else []
    for hh in (0, 2, 1, 3):
        pieces.append(lax.slice_in_dim(w, start + hh * HEAD_DIM, start + (hh + 1) * HEAD_DIM, axis=axis))
    pieces.append(lax.slice_in_dim(w, start + 4 * HEAD_DIM, w.shape[axis], axis=axis))
    return jnp.concatenate(pieces, axis=axis)


def _rope_tables():
    t = jnp.arange(DEC_SEQ)
    row = (t // GRID_W).astype(F32)[:, None]
    col = (t % GRID_W).astype(F32)[:, None]
    tabs = []
    for d in (HEAD_DIM, B_HALF):
        quarter = d // 4
        inv = jnp.power(jnp.float32(ROPE_THETA), -jnp.arange(quarter, dtype=F32) / quarter)
        ar, ac = row * inv, col * inv
        cos = jnp.concatenate([jnp.cos(ar), jnp.cos(ar), jnp.cos(ac), jnp.cos(ac)], axis=-1)
        sin = jnp.concatenate([-jnp.sin(ar), jnp.sin(ar), -jnp.sin(ac), jnp.sin(ac)], axis=-1)
        reps = LANES // d
        tabs += [jnp.tile(cos, (1, reps)), jnp.tile(sin, (1, reps))]
    return tabs


def _block_diag_mean(width):
    idx = np.arange(width) // HEAD_DIM
    return jnp.asarray((idx[:, None] == idx[None, :]).astype(np.float32) / HEAD_DIM, dtype=BF16)


def kernel(x_prompt, x_sample, cache_a_k, cache_a_v, cache_b_k, cache_b_v, cache_c_k, cache_c_v, cache_d_k, cache_d_v, c, c_ctx, w_ada, b_ada, g_norm1, w_in, g_q_a, g_k_a, lam_b, g_subln_b, sink_c, rpb_d, w_branch, w_out, g_norm2, w_ffn_in, w_ffn_out, g_final):
    w_in_p = _gqa_head_order(_gqa_head_order(w_in, 2, A_QE), 2, C_QE).astype(BF16)
    wb = jnp.stack([_gqa_head_order(w_branch[:, 0], 1, 0), w_branch[:, 1],
                    _gqa_head_order(w_branch[:, 2], 1, 0), w_branch[:, 3]], axis=1).astype(BF16)
    wo = w_out.astype(BF16)
    wfi = w_ffn_in.astype(BF16)
    wfo = w_ffn_out.astype(BF16)
    gq = jnp.tile(g_q_a, (1, 2)).reshape(DEPTH, 1, LANES)
    gk = jnp.tile(g_k_a, (1, 2)).reshape(DEPTH, 1, LANES)
    gsub = jnp.tile(g_subln_b, (1, 4)).reshape(DEPTH, 1, 4 * HEAD_DIM)
    g1 = g_norm1.reshape(DEPTH, 1, D_MODEL)
    g2 = g_norm2.reshape(DEPTH, 1, D_MODEL)
    gf = g_final.reshape(1, D_MODEL)
    bd128 = _block_diag_mean(LANES)
    bd256 = _block_diag_mean(4 * HEAD_DIM)
    rope_tabs = _rope_tables()
    caches = (cache_a_k, cache_a_v, cache_b_k, cache_b_v, cache_c_k, cache_c_v, cache_d_k, cache_d_v)

    cond = jnp.concatenate([c, c_ctx[None, :], jnp.zeros((MOD_ROWS - DEC_BATCH - 1, D_MODEL), F32)], axis=0)
    mod = _ada_call(cond, w_ada, b_ada).reshape(DEPTH, MOD_ROWS, 1, 6 * D_MODEL)
    dbias = _dbias_call(rpb_d)

    xp = x_prompt.reshape(BATCH * SEQ, D_MODEL)
    xs = x_sample.reshape(DEC_BATCH * DEC_SEQ, D_MODEL)
    new_kv = None
    for l in range(DEPTH):
        lam_init = 0.8 - 0.6 * math.exp(-0.3 * l)
        final = l == DEPTH - 1
        outs = _pre_call(l, xp, mod, g1, w_in_p, gq, gk, bd128, None, new_kv, prompt=True)
        qkv_p, gates_p, new_kv = outs[0], outs[1], outs[2:]
        br_p = _attn_prompt_call(l, lam_init, qkv_p, sink_c, lam_b, gsub, bd256)
        xp = _post_call(l, xp, br_p, gates_p, mod, g2, wb, wo, wfi, wfo, gf, prompt=True, final=final)

        qkv_s, gates_s = _pre_call(l, xs, mod, g1, w_in_p, gq, gk, bd128, rope_tabs, None, prompt=False)
        br_s = _attn_latent_call(l, lam_init, qkv_s, caches, dbias, sink_c, lam_b, gsub, bd256)
        xs = _post_call(l, xs, br_s, gates_s, mod, g2, wb, wo, wfi, wfo, gf, prompt=False, final=final)

    y_prompt = xp.reshape(BATCH, SEQ, D_MODEL)
    y_sample = xs.reshape(DEC_BATCH, DEC_SEQ, D_MODEL)
    return (y_prompt, y_sample, *new_kv)
```

```python
import functools
import math

import numpy as np
import jax
import jax.numpy as jnp
from jax import lax
from jax.experimental import pallas as pl
from jax.experimental.pallas import tpu as pltpu

F32 = jnp.float32
BF16 = jnp.bfloat16

D_MODEL = 1024
BATCH = 32
SEQ = 256
DEPTH = 2
DEC_BATCH = 8
DEC_SEQ = 1024
PAST_LEN = 256
GRID_W = 64
HEAD_DIM = 64
B_HALF = HEAD_DIM // 2
C_WINDOW = 128
NA_ROWS = 8
NA_COLS = 16
D_FF = 2816
ROPE_THETA = 10000.0
NORM_EPS = 1e-6
SUBLN_EPS = 1e-5
NEG_INF = -1e30

GATE_W = 4 * D_MODEL
W_A, W_B, W_C, W_D, W_GATES = 0, 512, 1280, 1792, 2560
IN_COLS = W_GATES + GATE_W
MOD_ROWS = 16
CTX_ROW = DEC_BATCH
LANES = 128
ATT_BLK = 256
NA_BLK_ROWS = ATT_BLK // GRID_W
NA_WIN_ROWS = 12
NA_WIN = NA_WIN_ROWS * GRID_W
NA_WIN_START = (0, 0, 4, 4)
C_WIN = ATT_BLK + 2 * C_WINDOW
VMEM_LIMIT = 56 * 1024 * 1024

A_QE, A_QO, A_K, A_V = 0, 128, 256, 384
B_Q, B_K, B_V = 640, 896, 1152
C_QE, C_QO, C_K, C_V = 1664, 1792, 1920, 2048
D_Q, D_K, D_V = 2304, 2560, 2816
QKV_W = 3328
X_AK, X_AV, X_BK, X_BV, X_CK, X_CV, X_DK, X_DV = 0, 128, 384, 640, 1152, 1280, 1536, 1792
CACHE_W = 2304
VAUG = 2 * LANES


def _dot(a, b):
    return jnp.dot(a, b, preferred_element_type=F32)


def _dot_nt(a, b):
    return lax.dot_general(a, b, (((1,), (1,)), ((), ())), preferred_element_type=F32)


def _sigmoid(x):
    return 0.5 * jnp.tanh(0.5 * x) + 0.5


def _rms(x, g, eps):
    ms = jnp.mean(x * x, axis=-1, keepdims=True)
    return x * lax.rsqrt(ms + eps) * g


def _head_rms(v, bd, g, eps):
    v2 = v * v
    hi = v2.astype(BF16)
    lo = (v2 - hi.astype(F32)).astype(BF16)
    ms = _dot(hi, bd) + _dot(lo, bd)
    return v * lax.rsqrt(ms + eps) * g


def _lane_mask(shape, lo, hi):
    lane = lax.broadcasted_iota(jnp.int32, shape, 1)
    return (lane >= lo) & (lane < hi)


def _mask_q(qf, lo, hi):
    return jnp.where(_lane_mask(qf.shape, lo, hi), qf, 0.0).astype(BF16)


def _interleave_heads(lo, hi):
    left = _lane_mask(lo.shape, 0, HEAD_DIM)
    return (jnp.where(left, lo, pltpu.roll(hi, HEAD_DIM, 1)),
            jnp.where(left, pltpu.roll(lo, HEAD_DIM, 1), hi))


def _ada_kernel(cond_ref, w_ref, b_ref, o_ref):
    c = cond_ref[...]
    s = (c * _sigmoid(c)).astype(BF16)
    o_ref[...] = _dot(s, w_ref[...].astype(BF16)) + b_ref[...]


def _ada_call(cond, w_ada, b_ada):
    tn = 1536
    return pl.pallas_call(
        _ada_kernel,
        out_shape=jax.ShapeDtypeStruct((DEPTH, MOD_ROWS, 6 * D_MODEL), F32),
        grid=(DEPTH, 6 * D_MODEL // tn),
        in_specs=[
            pl.BlockSpec((MOD_ROWS, D_MODEL), lambda l, j: (0, 0)),
            pl.BlockSpec((None, D_MODEL, tn), lambda l, j: (l, 0, j)),
            pl.BlockSpec((None, 1, tn), lambda l, j: (l, 0, j)),
        ],
        out_specs=pl.BlockSpec((None, MOD_ROWS, tn), lambda l, j: (l, 0, j)),
        compiler_params=pltpu.CompilerParams(vmem_limit_bytes=VMEM_LIMIT),
        name="ada",
    )(cond, w_ada, b_ada.reshape(DEPTH, 1, 6 * D_MODEL))


def _dbias_kernel(rpb_ref, o_ref):
    n_dr, n_dc = 2 * NA_ROWS - 1, 2 * NA_COLS - 1
    base = (pl.program_id(0) * 4 + pl.program_id(1)) * (n_dr * n_dc)
    shape = (GRID_W, LANES)
    cq = lax.broadcasted_iota(jnp.int32, shape, 0)
    lane = lax.broadcasted_iota(jnp.int32, shape, 1)
    ck = lane & (GRID_W - 1)
    dc = jnp.clip(ck - cq, -(NA_COLS - 1), NA_COLS - 1) + (NA_COLS - 1)
    start_c = jnp.clip(cq - NA_COLS // 2, 0, GRID_W - NA_COLS)
    col_valid = (ck >= start_c) & (ck < start_c + NA_COLS)
    neg = jnp.full(shape, NEG_INF, F32)
    toeplitz = []
    for dr in range(n_dr):
        t = jnp.zeros(shape, F32)
        for m in range(n_dc):
            t = jnp.where(dc == m, rpb_ref[base + dr * n_dc + m], t)
        toeplitz.append(jnp.where(col_valid, t, neg))
    left = lane < GRID_W
    rows = DEC_SEQ // GRID_W
    for n in range(DEC_SEQ // ATT_BLK):
        for rq in range(NA_BLK_ROWS):
            r = NA_BLK_ROWS * n + rq
            start_r = min(max(r - NA_ROWS // 2, 0), rows - NA_ROWS)
            for jp in range(NA_WIN_ROWS // 2):
                pair = []
                for j in (2 * jp, 2 * jp + 1):
                    key_row = NA_WIN_START[n] + j
                    valid = start_r <= key_row < start_r + NA_ROWS
                    pair.append(toeplitz[key_row - r + NA_ROWS - 1] if valid else neg)
                tile = pair[0] if pair[0] is pair[1] else jnp.where(left, pair[0], pair[1])
                o_ref[n, rq * GRID_W:(rq + 1) * GRID_W, jp * LANES:(jp + 1) * LANES] = tile


def _dbias_call(rpb_d):
    nblk = DEC_SEQ // ATT_BLK
    return pl.pallas_call(
        _dbias_kernel,
        out_shape=jax.ShapeDtypeStruct((DEPTH, 4, nblk, ATT_BLK, NA_WIN), F32),
        grid=(DEPTH, 4),
        in_specs=[pl.BlockSpec(memory_space=pltpu.SMEM)],
        out_specs=pl.BlockSpec((None, None, nblk, ATT_BLK, NA_WIN), lambda l, h: (l, h, 0, 0, 0)),
        name="dbias",
    )(rpb_d.reshape(-1))


def _swap_halves(v, half):
    lane = lax.broadcasted_iota(jnp.int32, v.shape, 1)
    up = pltpu.roll(v, LANES - half, 1)
    dn = pltpu.roll(v, half, 1)
    return jnp.where((lane & (2 * half - 1)) < half, up, dn)


def _rope(v, cos, sin, half):
    return v * cos + _swap_halves(v, half) * sin


def _pre_kernel(*refs, prompt, first):
    if prompt:
        (x_ref, sh_ref, sc_ref, g1_ref, w_ref, gq_ref, gk_ref, bd_ref) = refs[:8]
        (qkv_ref, gates_ref, ka_ref, va_ref, kb_ref, vb_ref, kc_ref, vc_ref, kd_ref, vd_ref) = refs[-10:]
    else:
        (x_ref, sh_ref, sc_ref, g1_ref, w_ref, gq_ref, gk_ref, bd_ref,
         ca_ref, sa_ref, cb_ref, sb_ref, qkv_ref, gates_ref) = refs

    h = (_rms(x_ref[...], g1_ref[...], NORM_EPS) * (1.0 + sc_ref[...]) + sh_ref[...]).astype(BF16)
    bd = bd_ref[...]
    ones = jnp.ones((h.shape[0], LANES), BF16)

    def rope_a(v):
        return v if prompt else _rope(v, ca_ref[...], sa_ref[...], 16)

    def rope_b(v):
        return v if prompt else _rope(v, cb_ref[...], sb_ref[...], 8)

    def put(col, v):
        qkv_ref[:, col:col + v.shape[1]] = v.astype(BF16)

    def put_values(col, v):
        for c in range(v.shape[1] // LANES):
            put(col + c * VAUG, v[:, c * LANES:(c + 1) * LANES])
            put(col + c * VAUG + LANES, ones)

    def put_heads(ref, v):
        for hh in range(v.shape[1] // HEAD_DIM):
            piece = v[:, hh * HEAD_DIM:(hh + 1) * HEAD_DIM]
            if first:
                ref[0, hh, :, :] = piece
                for later in range(1, DEPTH):
                    ref[later, hh, :, :] = jnp.zeros_like(piece)
            else:
                ref[hh, :, :] = piece

    scale = HEAD_DIM ** -0.5
    acc = _dot(h, w_ref[:, W_A:W_A + 512])
    gq = gq_ref[...]
    q_e, q_o = _interleave_heads(acc[:, 0:128], acc[:, 128:256])
    put(A_QE, rope_a(_head_rms(q_e, bd, gq, NORM_EPS)) * scale)
    put(A_QO, rope_a(_head_rms(q_o, bd, gq, NORM_EPS)) * scale)
    k_a = _head_rms(acc[:, 256:384], bd, gk_ref[...], NORM_EPS)
    put(A_K, rope_a(k_a))
    put_values(A_V, acc[:, 384:512])
    if prompt:
        put_heads(ka_ref, k_a)
        put_heads(va_ref, acc[:, 384:512])
    acc = _dot(h, w_ref[:, W_B:W_B + 768])
    scale_b = B_HALF ** -0.5
    for c in range(2):
        put(B_Q + c * 128, rope_b(acc[:, c * 128:(c + 1) * 128]) * scale_b)
        put(B_K + c * 128, rope_b(acc[:, 256 + c * 128:256 + (c + 1) * 128]))
    put_values(B_V, acc[:, 512:768])
    if prompt:
        put_heads(kb_ref, acc[:, 256:512])
        put_heads(vb_ref, acc[:, 512:768])
    acc = _dot(h, w_ref[:, W_C:W_C + 512])
    q_e, q_o = _interleave_heads(acc[:, 0:128], acc[:, 128:256])
    put(C_QE, rope_a(q_e) * scale)
    put(C_QO, rope_a(q_o) * scale)
    put(C_K, rope_a(acc[:, 256:384]))
    put_values(C_V, acc[:, 384:512])
    if prompt:
        put_heads(kc_ref, acc[:, 256:384])
        put_heads(vc_ref, acc[:, 384:512])
    acc = _dot(h, w_ref[:, W_D:W_D + 768])
    put(D_Q, acc[:, 0:256] * scale)
    put(D_K, acc[:, 256:512])
    put_values(D_V, acc[:, 512:768])
    if prompt:
        put_heads(kd_ref, acc[:, 256:512])
        put_heads(vd_ref, acc[:, 512:768])
    for j in range(GATE_W // 512):
        gates_ref[:, j * 512:(j + 1) * 512] = _dot(h, w_ref[:, W_GATES + j * 512:W_GATES + (j + 1) * 512]).astype(BF16)


def _pre_call(l, x, mod, g1, w_in, gq, gk, bd128, rope_tabs, kv_prev, *, prompt):
    rows = x.shape[0]
    tm = SEQ if prompt else 512
    tq = tm
    aliases = {}
    if prompt:
        mod_row = lambda i: CTX_ROW
    else:
        mod_row = lambda i: i // (DEC_SEQ // tm)
    in_specs = [
        pl.BlockSpec((tm, D_MODEL), lambda i: (i, 0)),
        pl.BlockSpec((None, None, 1, D_MODEL), lambda i: (l, mod_row(i), 0, 0)),
        pl.BlockSpec((None, None, 1, D_MODEL), lambda i: (l, mod_row(i), 0, 1)),
        pl.BlockSpec((None, 1, D_MODEL), lambda i: (l, 0, 0)),
        pl.BlockSpec((None, D_MODEL, IN_COLS), lambda i: (l, 0, 0), pipeline_mode=pl.Buffered(1)),
        pl.BlockSpec((None, 1, LANES), lambda i: (l, 0, 0)),
        pl.BlockSpec((None, 1, LANES), lambda i: (l, 0, 0)),
        pl.BlockSpec((LANES, LANES), lambda i: (0, 0)),
    ]
    args = [x, mod, mod, g1, w_in, gq, gk, bd128]
    out_shape = [jax.ShapeDtypeStruct((rows, QKV_W), BF16), jax.ShapeDtypeStruct((rows, GATE_W), BF16)]
    out_specs = [pl.BlockSpec((tm, QKV_W), lambda i: (i, 0)), pl.BlockSpec((tm, GATE_W), lambda i: (i, 0))]
    if prompt:
        for j, nh in enumerate((2, 2, 4, 4, 2, 2, 4, 4)):
            out_shape.append(jax.ShapeDtypeStruct((BATCH, DEPTH, nh, SEQ, HEAD_DIM), F32))
            if kv_prev is None:
                out_specs.append(pl.BlockSpec((None, DEPTH, nh, SEQ, HEAD_DIM), lambda i: (i, 0, 0, 0, 0)))
            else:
                out_specs.append(pl.BlockSpec((None, None, nh, SEQ, HEAD_DIM), lambda i: (i, l, 0, 0, 0)))
                aliases[len(args)] = 2 + j
                in_specs.append(pl.BlockSpec(memory_space=pl.ANY))
                args.append(kv_prev[j])
    else:
        nt = DEC_SEQ // tq
        for t in rope_tabs:
            in_specs.append(pl.BlockSpec((tq, LANES), lambda i: (i % nt, 0)))
            args.append(t)
    return pl.pallas_call(
        functools.partial(_pre_kernel, prompt=prompt, first=kv_prev is None),
        out_shape=out_shape,
        grid=(rows // tm,),
        in_specs=in_specs,
        out_specs=out_specs,
        input_output_aliases=aliases,
        compiler_params=pltpu.CompilerParams(vmem_limit_bytes=VMEM_LIMIT),
        name="pre_prompt" if prompt else "pre_latent",
    )(*args)


def _scores(qs, keys, biases):
    out = []
    for k, b in zip(keys, biases):
        s = _dot_nt(qs, k)
        out.append(s if b is None else s + b)
    return out


def _row_max(s_list, sink):
    m = None
    for s in s_list:
        mi = jnp.max(s, axis=-1, keepdims=True)
        m = mi if m is None else jnp.maximum(m, mi)
    return m if sink is None else jnp.maximum(m, sink)


def _softmax_pv(s_list, vaugs, sink=None):
    m = _row_max(s_list, sink)
    r = None
    for s, v in zip(s_list, vaugs):
        p = jnp.exp((s - m).astype(BF16))
        ri = _dot(p, v)
        r = ri if r is None else r + ri
    den = r[:, LANES:]
    if sink is not None:
        den = den + jnp.exp(sink - m)
    return r[:, :LANES] / den


def _gqa(qe, qo, keys, vaugs, biases, sinks):
    mq = qe.shape[0]
    oe = jnp.zeros((mq, LANES), F32)
    oo = jnp.zeros((mq, LANES), F32)
    for g in range(2):
        lo, hi = HEAD_DIM * g, HEAD_DIM * (g + 1)
        qs = jnp.concatenate([_mask_q(qe, lo, hi), _mask_q(qo, lo, hi)], axis=0)
        sink = None
        if sinks is not None:
            row = lax.broadcasted_iota(jnp.int32, (2 * mq, 1), 0)
            sink = jnp.where(row < mq, sinks[2 * g], sinks[2 * g + 1])
        r = _softmax_pv(_scores(qs, keys, biases), vaugs, sink)
        msk = _lane_mask((mq, LANES), lo, hi)
        oe = jnp.where(msk, r[:mq], oe)
        oo = jnp.where(msk, r[mq:], oo)
    return _interleave_heads(oe, oo)


def _diff(q, keys, vaug_fn, lam):
    mq = q.shape[0]
    out = [jnp.zeros((mq, LANES), F32), jnp.zeros((mq, LANES), F32)]
    for hd in range(4):
        lo = HEAD_DIM * hd
        qs = jnp.concatenate([_mask_q(q, lo, lo + B_HALF), _mask_q(q, lo + B_HALF, lo + HEAD_DIM)], axis=0)
        o = _softmax_pv(_scores(qs, keys, [None] * len(keys)), vaug_fn(hd // 2))
        o = o[:mq] - lam * o[mq:]
        plo = HEAD_DIM * (hd % 2)
        out[hd // 2] = jnp.where(_lane_mask(o.shape, plo, plo + HEAD_DIM), o, out[hd // 2])
    return out


def _mha(q, keys, vaug_fn, bias_fn):
    mq = q.shape[0]
    out = [jnp.zeros((mq, LANES), F32), jnp.zeros((mq, LANES), F32)]
    for hd in range(4):
        lo = HEAD_DIM * hd
        qs = _mask_q(q, lo, lo + HEAD_DIM)
        o = _softmax_pv(_scores(qs, keys, bias_fn(hd)), vaug_fn(hd // 2))
        plo = HEAD_DIM * (hd % 2)
        out[hd // 2] = jnp.where(_lane_mask(o.shape, plo, plo + HEAD_DIM), o, out[hd // 2])
    return out


def _lambda(lam_ref, lam_init):
    lp = lam_ref[...]
    a = jnp.sum(lp[0:1, :] * lp[1:2, :], axis=-1, keepdims=True)
    b = jnp.sum(lp[2:3, :] * lp[3:4, :], axis=-1, keepdims=True)
    return jnp.exp(a) - jnp.exp(b) + lam_init


def _store_branches(br_ref, oa, ob, oc, od, bd_ref, gsub_ref, lam_init):
    bd = bd_ref[...]
    ob = [_head_rms(o, bd, gsub_ref[...], SUBLN_EPS) * (1.0 - lam_init) for o in ob]
    for j, o in enumerate((*oa, *ob, *oc, *od)):
        br_ref[:, j * LANES:(j + 1) * LANES] = o.astype(BF16)


def _attn_prompt_kernel(qkv_ref, sink_ref, lam_ref, gsub_ref, bd_ref, br_ref, *, l, lam_init):
    def cols(c, w):
        return qkv_ref[:, c:c + w]

    def qcols(c, w):
        return cols(c, w).astype(F32)

    sinks = [sink_ref[l, i] for i in range(4)]
    oa = _gqa(qcols(A_QE, 128), qcols(A_QO, 128), [cols(A_K, 128)], [cols(A_V, VAUG)], [None], None)
    lam = _lambda(lam_ref, lam_init)
    ob = _diff(qcols(B_Q, 256), [cols(B_K, 256)], lambda pr: [cols(B_V + pr * VAUG, VAUG)], lam)
    oc = _gqa(qcols(C_QE, 128), qcols(C_QO, 128), [cols(C_K, 128)], [cols(C_V, VAUG)], [None], sinks)
    od = _mha(qcols(D_Q, 256), [cols(D_K, 256)], lambda pr: [cols(D_V + pr * VAUG, VAUG)], lambda hd: [None])
    _store_branches(br_ref, oa, ob, oc, od, bd_ref, gsub_ref, lam_init)


def _attn_prompt_call(l, lam_init, qkv, sink_c, lam_b, gsub, bd128):
    rows = qkv.shape[0]
    return pl.pallas_call(
        functools.partial(_attn_prompt_kernel, l=l, lam_init=lam_init),
        out_shape=jax.ShapeDtypeStruct((rows, D_MODEL), BF16),
        grid=(rows // SEQ,),
        in_specs=[
            pl.BlockSpec((SEQ, QKV_W), lambda b: (b, 0)),
            pl.BlockSpec(memory_space=pltpu.SMEM),
            pl.BlockSpec((None, 4, B_HALF), lambda b: (l, 0, 0)),
            pl.BlockSpec((None, 1, LANES), lambda b: (l, 0, 0)),
            pl.BlockSpec((LANES, LANES), lambda b: (0, 0)),
        ],
        out_specs=pl.BlockSpec((SEQ, D_MODEL), lambda b: (b, 0)),
        compiler_params=pltpu.CompilerParams(vmem_limit_bytes=VMEM_LIMIT),
        name="attn_prompt",
    )(qkv, sink_c, lam_b, gsub, bd128)


def _attn_latent_kernel(kv_ref, xak_ref, xav_ref, xbk_ref, xbv_ref, xck_ref, xcv_ref, xdk_ref, xdv_ref,
                        bias_ref, sink_ref, lam_ref, gsub_ref, bd_ref, br_ref, ctx_ref, *, l, lam_init):
    n = pl.program_id(1)
    q0 = pl.multiple_of(n * ATT_BLK, ATT_BLK)

    @pl.when(n == 0)
    def _():
        ones = jnp.ones((PAST_LEN, LANES), F32)
        for col, ref, is_value in ((X_AK, xak_ref, False), (X_AV, xav_ref, True), (X_BK, xbk_ref, False),
                                   (X_BV, xbv_ref, True), (X_CK, xck_ref, False), (X_CV, xcv_ref, True),
                                   (X_DK, xdk_ref, False), (X_DV, xdv_ref, True)):
            for hh in range(ref.shape[0]):
                c = col + hh * HEAD_DIM + ((hh // 2) * LANES if is_value else 0)
                ctx_ref[:, c:c + HEAD_DIM] = ref[hh]
            if is_value:
                for pr in range(ref.shape[0] // 2):
                    ctx_ref[:, col + pr * VAUG + LANES:col + (pr + 1) * VAUG] = ones

    def q(c, w):
        return kv_ref[pl.ds(q0, ATT_BLK), c:c + w].astype(F32)

    def lat(c, w):
        return kv_ref[:, c:c + w]

    def ctx(c, w):
        return ctx_ref[:, c:c + w].astype(BF16)

    sinks = [sink_ref[l, i] for i in range(4)]
    oa = _gqa(q(A_QE, 128), q(A_QO, 128), [ctx(X_AK, 128), lat(A_K, 128)], [ctx(X_AV, VAUG), lat(A_V, VAUG)],
              [None, None], None)
    lam = _lambda(lam_ref, lam_init)
    ob = _diff(q(B_Q, 256), [ctx(X_BK, 256), lat(B_K, 256)],
               lambda pr: [ctx(X_BV + pr * VAUG, VAUG), lat(B_V + pr * VAUG, VAUG)], lam)
    start_c = pl.multiple_of(jnp.clip(q0 - C_WINDOW, 0, DEC_SEQ - C_WIN), C_WINDOW)
    rowq = lax.broadcasted_iota(jnp.int32, (2 * ATT_BLK, C_WIN), 0) & (ATT_BLK - 1)
    colk = lax.broadcasted_iota(jnp.int32, (2 * ATT_BLK, C_WIN), 1)
    band = jnp.where(jnp.abs(rowq - colk + (q0 - start_c)) <= C_WINDOW, 0.0, NEG_INF)
    oc = _gqa(q(C_QE, 128), q(C_QO, 128),
              [ctx(X_CK, 128), kv_ref[pl.ds(start_c, C_WIN), C_K:C_K + 128]],
              [ctx(X_CV, VAUG), kv_ref[pl.ds(start_c, C_WIN), C_V:C_V + VAUG]],
              [None, band], sinks)
    start_d = pl.multiple_of(jnp.where(n >= 2, NA_WIN_START[2] * GRID_W, 0), ATT_BLK)
    kwin = kv_ref[pl.ds(start_d, NA_WIN), D_K:D_K + 256]
    od = _mha(q(D_Q, 256), [ctx(X_DK, 256), kwin],
              lambda pr: [ctx(X_DV + pr * VAUG, VAUG),
                          kv_ref[pl.ds(start_d, NA_WIN), D_V + pr * VAUG:D_V + (pr + 1) * VAUG]],
              lambda hd: [None, bias_ref[hd, n]])
    _store_branches(br_ref, oa, ob, oc, od, bd_ref, gsub_ref, lam_init)


def _attn_latent_call(l, lam_init, qkv, caches, dbias, sink_c, lam_b, gsub, bd128):
    rows = qkv.shape[0]
    nblk = DEC_SEQ // ATT_BLK
    cache_specs = [pl.BlockSpec((None, None, t.shape[2], PAST_LEN, HEAD_DIM), lambda b, n: (b, l, 0, 0, 0))
                   for t in caches]
    return pl.pallas_call(
        functools.partial(_attn_latent_kernel, l=l, lam_init=lam_init),
        out_shape=jax.ShapeDtypeStruct((rows, D_MODEL), BF16),
        grid=(DEC_BATCH, nblk),
        in_specs=[
            pl.BlockSpec((DEC_SEQ, QKV_W), lambda b, n: (b, 0)),
            *cache_specs,
            pl.BlockSpec((None, 4, nblk, ATT_BLK, NA_WIN), lambda b, n: (l, 0, 0, 0, 0), pipeline_mode=pl.Buffered(1)),
            pl.BlockSpec(memory_space=pltpu.SMEM),
            pl.BlockSpec((None, 4, B_HALF), lambda b, n: (l, 0, 0)),
            pl.BlockSpec((None, 1, LANES), lambda b, n: (l, 0, 0)),
            pl.BlockSpec((LANES, LANES), lambda b, n: (0, 0)),
        ],
        out_specs=pl.BlockSpec((ATT_BLK, D_MODEL), lambda b, n: (b * nblk + n, 0)),
        scratch_shapes=[pltpu.VMEM((PAST_LEN, CACHE_W), F32)],
        compiler_params=pltpu.CompilerParams(dimension_semantics=("arbitrary", "arbitrary"),
                                             vmem_limit_bytes=VMEM_LIMIT),
        name="attn_latent",
    )(qkv, *caches, dbias, sink_c, lam_b, gsub, bd128)


MXU_TILE = 256
FFN_CHUNKS = ((0, 6 * MXU_TILE), (6 * MXU_TILE, D_FF))


def _post_kernel(x_ref, br_ref, gates_ref, gt1_ref, sh2_ref, sc2_ref, gt2_ref, g2_ref,
                 wb_ref, wo_ref, wfi_ref, wfo_ref, gf_ref, o_ref, *, final):
    merged = None
    for k in range(4):
        proj = _dot(br_ref[:, k * 256:(k + 1) * 256], wb_ref[k])
        t = _sigmoid(gates_ref[:, k * D_MODEL:(k + 1) * D_MODEL].astype(F32)) * proj
        merged = t if merged is None else merged + t
    x1 = x_ref[...] + gt1_ref[...] * _dot(merged.astype(BF16), wo_ref[...])
    h2 = (_rms(x1, g2_ref[...], NORM_EPS) * (1.0 + sc2_ref[...]) + sh2_ref[...]).astype(BF16)
    acc = None
    for c0, c1 in FFN_CHUNKS:
        a = _dot(h2, wfi_ref[:, c0:c1])
        u = _dot(h2, wfi_ref[:, D_FF + c0:D_FF + c1])
        g = ((a * _sigmoid(a)) * u).astype(BF16)
        t = _dot(g, wfo_ref[c0:c1, :])
        acc = t if acc is None else acc + t
    xo = x1 + gt2_ref[...] * acc
    if final:
        xo = _rms(xo, gf_ref[...], NORM_EPS)
    o_ref[...] = xo


def _post_call(l, x, br, gates, mod, g2, wb, wo, wfi, wfo, gf, *, prompt, final):
    rows = x.shape[0]
    tm = 256
    if prompt:
        mod_row = lambda i: CTX_ROW
    else:
        mod_row = lambda i: i // (DEC_SEQ // tm)

    def mod_spec(chunk):
        return pl.BlockSpec((None, None, 1, D_MODEL), lambda i: (l, mod_row(i), 0, chunk))

    def resident(shape):
        nd = len(shape)
        return pl.BlockSpec((None,) + shape, lambda i: (l,) + (0,) * nd, pipeline_mode=pl.Buffered(1))

    return pl.pallas_call(
        functools.partial(_post_kernel, final=final),
        out_shape=jax.ShapeDtypeStruct((rows, D_MODEL), F32),
        grid=(rows // tm,),
        in_specs=[
            pl.BlockSpec((tm, D_MODEL), lambda i: (i, 0)),
            pl.BlockSpec((tm, D_MODEL), lambda i: (i, 0)),
            pl.BlockSpec((tm, GATE_W), lambda i: (i, 0)),
            mod_spec(2), mod_spec(3), mod_spec(4), mod_spec(5),
            pl.BlockSpec((None, 1, D_MODEL), lambda i: (l, 0, 0)),
            resident((4, 256, D_MODEL)),
            resident((D_MODEL, D_MODEL)),
            resident((D_MODEL, 2 * D_FF)),
            resident((D_FF, D_MODEL)),
            pl.BlockSpec((1, D_MODEL), lambda i: (0, 0)),
        ],
        out_specs=pl.BlockSpec((tm, D_MODEL), lambda i: (i, 0)),
        compiler_params=pltpu.CompilerParams(vmem_limit_bytes=VMEM_LIMIT),
        name="post_prompt" if prompt else "post_latent",
    )(x, br, gates, mod, mod, mod, mod, g2, wb, wo, wfi, wfo, gf)


def _rope_tables():
    t = jnp.arange(DEC_SEQ)
    row = (t // GRID_W).astype(F32)[:, None]
    col = (t % GRID_W).astype(F32)[:, None]
    tabs = []
    for d in (HEAD_DIM, B_HALF):
        quarter = d // 4
        inv = jnp.power(jnp.float32(ROPE_THETA), -jnp.arange(quarter, dtype=F32) / quarter)
        ar, ac = row * inv, col * inv
        cos = jnp.concatenate([jnp.cos(ar), jnp.cos(ar), jnp.cos(ac), jnp.cos(ac)], axis=-1)
        sin = jnp.concatenate([-jnp.sin(ar), jnp.sin(ar), -jnp.sin(ac), jnp.sin(ac)], axis=-1)
        reps = LANES // d
        tabs += [jnp.tile(cos, (1, reps)), jnp.tile(sin, (1, reps))]
    return tabs


def _block_diag_mean(width):
    idx = np.arange(width) // HEAD_DIM
    return jnp.asarray((idx[:, None] == idx[None, :]).astype(np.float32) / HEAD_DIM, dtype=BF16)


def kernel(x_prompt, x_sample, cache_a_k, cache_a_v, cache_b_k, cache_b_v, cache_c_k, cache_c_v, cache_d_k, cache_d_v, c, c_ctx, w_ada, b_ada, g_norm1, w_in, g_q_a, g_k_a, lam_b, g_subln_b, sink_c, rpb_d, w_branch, w_out, g_norm2, w_ffn_in, w_ffn_out, g_final):
    w_in_b = w_in.astype(BF16)
    wb = w_branch.astype(BF16)
    wo = w_out.astype(BF16)
    wfi = w_ffn_in.astype(BF16)
    wfo = w_ffn_out.astype(BF16)
    gq = jnp.tile(g_q_a, (1, 2)).reshape(DEPTH, 1, LANES)
    gk = jnp.tile(g_k_a, (1, 2)).reshape(DEPTH, 1, LANES)
    gsub = jnp.tile(g_subln_b, (1, 2)).reshape(DEPTH, 1, LANES)
    g1 = g_norm1.reshape(DEPTH, 1, D_MODEL)
    g2 = g_norm2.reshape(DEPTH, 1, D_MODEL)
    gf = g_final.reshape(1, D_MODEL)
    bd128 = _block_diag_mean(LANES)
    rope_tabs = _rope_tables()
    caches = (cache_a_k, cache_a_v, cache_b_k, cache_b_v, cache_c_k, cache_c_v, cache_d_k, cache_d_v)

    cond = jnp.concatenate([c, c_ctx[None, :], jnp.zeros((MOD_ROWS - DEC_BATCH - 1, D_MODEL), F32)], axis=0)
    mod = _ada_call(cond, w_ada, b_ada).reshape(DEPTH, MOD_ROWS, 1, 6 * D_MODEL)
    dbias = _dbias_call(rpb_d)

    xp = x_prompt.reshape(BATCH * SEQ, D_MODEL)
    xs = x_sample.reshape(DEC_BATCH * DEC_SEQ, D_MODEL)
    new_kv = None
    for l in range(DEPTH):
        lam_init = 0.8 - 0.6 * math.exp(-0.3 * l)
        final = l == DEPTH - 1
        outs = _pre_call(l, xp, mod, g1, w_in_b, gq, gk, bd128, None, new_kv, prompt=True)
        qkv_p, gates_p, new_kv = outs[0], outs[1], outs[2:]
        br_p = _attn_prompt_call(l, lam_init, qkv_p, sink_c, lam_b, gsub, bd128)
        xp = _post_call(l, xp, br_p, gates_p, mod, g2, wb, wo, wfi, wfo, gf, prompt=True, final=final)

        qkv_s, gates_s = _pre_call(l, xs, mod, g1, w_in_b, gq, gk, bd128, rope_tabs, None, prompt=False)
        br_s = _attn_latent_call(l, lam_init, qkv_s, caches, dbias, sink_c, lam_b, gsub, bd128)
        xs = _post_call(l, xs, br_s, gates_s, mod, g2, wb, wo, wfi, wfo, gf, prompt=False, final=final)

    y_prompt = xp.reshape(BATCH, SEQ, D_MODEL)
    y_sample = xs.reshape(DEC_BATCH, DEC_SEQ, D_MODEL)
    return (y_prompt, y_sample, *new_kv)
```

```python
import functools
import math

import numpy as np
import jax
import jax.numpy as jnp
from jax import lax
from jax.experimental import pallas as pl
from jax.experimental.pallas import tpu as pltpu

F32 = jnp.float32
BF16 = jnp.bfloat16

D_MODEL = 1024
BATCH = 32
SEQ = 256
DEPTH = 2
DEC_BATCH = 8
DEC_SEQ = 1024
PAST_LEN = 256
GRID_W = 64
HEAD_DIM = 64
B_HALF = HEAD_DIM // 2
C_WINDOW = 128
NA_ROWS = 8
NA_COLS = 16
D_FF = 2816
ROPE_THETA = 10000.0
NORM_EPS = 1e-6
SUBLN_EPS = 1e-5
NEG_INF = -1e30

GATE_W = 4 * D_MODEL
W_A, W_B, W_C, W_D, W_GATES = 0, 512, 1280, 1792, 2560
IN_COLS = W_GATES + GATE_W
MOD_ROWS = 16
CTX_ROW = DEC_BATCH
LANES = 128
ATT_BLK = 256
NA_BLK_ROWS = ATT_BLK // GRID_W
NA_WIN_ROWS = 12
NA_WIN = NA_WIN_ROWS * GRID_W
NA_WIN_START = (0, 0, 4, 4)
C_WIN = ATT_BLK + 2 * C_WINDOW
VMEM_LIMIT = 56 * 1024 * 1024

A_QE, A_QO, A_K, A_V = 0, 128, 256, 384
B_Q, B_K, B_V = 640, 896, 1152
C_QE, C_QO, C_K, C_V = 1664, 1792, 1920, 2048
D_Q, D_K, D_V = 2304, 2560, 2816
QKV_W = 3328
VAUG = 2 * LANES


def _dot(a, b):
    return jnp.dot(a, b, preferred_element_type=F32)


def _dot_nt(a, b):
    return lax.dot_general(a, b, (((1,), (1,)), ((), ())), preferred_element_type=F32)


def _sigmoid(x):
    return 0.5 * jnp.tanh(0.5 * x) + 0.5


def _rms(x, g, eps):
    ms = jnp.mean(x * x, axis=-1, keepdims=True)
    return x * lax.rsqrt(ms + eps) * g


def _head_rms(v, bd, g, eps):
    v2 = v * v
    hi = v2.astype(BF16)
    lo = (v2 - hi.astype(F32)).astype(BF16)
    ms = _dot(hi, bd) + _dot(lo, bd)
    return v * lax.rsqrt(ms + eps) * g


def _lane_mask(shape, lo, hi):
    lane = lax.broadcasted_iota(jnp.int32, shape, 1)
    return (lane >= lo) & (lane < hi)


def _mask_q(qf, lo, hi):
    return jnp.where(_lane_mask(qf.shape, lo, hi), qf, 0.0).astype(BF16)


def _interleave_heads(lo, hi):
    left = _lane_mask(lo.shape, 0, HEAD_DIM)
    return (jnp.where(left, lo, pltpu.roll(hi, HEAD_DIM, 1)),
            jnp.where(left, pltpu.roll(lo, HEAD_DIM, 1), hi))


def _ada_kernel(cond_ref, w_ref, b_ref, o_ref):
    c = cond_ref[...]
    s = (c * _sigmoid(c)).astype(BF16)
    o_ref[...] = _dot(s, w_ref[...].astype(BF16)) + b_ref[...]


def _ada_call(cond, w_ada, b_ada):
    tn = 1536
    return pl.pallas_call(
        _ada_kernel,
        out_shape=jax.ShapeDtypeStruct((DEPTH, MOD_ROWS, 6 * D_MODEL), F32),
        grid=(DEPTH, 6 * D_MODEL // tn),
        in_specs=[
            pl.BlockSpec((MOD_ROWS, D_MODEL), lambda l, j: (0, 0)),
            pl.BlockSpec((None, D_MODEL, tn), lambda l, j: (l, 0, j)),
            pl.BlockSpec((None, 1, tn), lambda l, j: (l, 0, j)),
        ],
        out_specs=pl.BlockSpec((None, MOD_ROWS, tn), lambda l, j: (l, 0, j)),
        compiler_params=pltpu.CompilerParams(vmem_limit_bytes=VMEM_LIMIT),
        name="ada",
    )(cond, w_ada, b_ada.reshape(DEPTH, 1, 6 * D_MODEL))


def _dbias_kernel(rpb_ref, o_ref):
    n_dr, n_dc = 2 * NA_ROWS - 1, 2 * NA_COLS - 1
    base = (pl.program_id(0) * 4 + pl.program_id(1)) * (n_dr * n_dc)
    shape = (GRID_W, LANES)
    cq = lax.broadcasted_iota(jnp.int32, shape, 0)
    lane = lax.broadcasted_iota(jnp.int32, shape, 1)
    ck = lane & (GRID_W - 1)
    dc = jnp.clip(ck - cq, -(NA_COLS - 1), NA_COLS - 1) + (NA_COLS - 1)
    start_c = jnp.clip(cq - NA_COLS // 2, 0, GRID_W - NA_COLS)
    col_valid = (ck >= start_c) & (ck < start_c + NA_COLS)
    neg = jnp.full(shape, NEG_INF, F32)
    toeplitz = []
    for dr in range(n_dr):
        t = jnp.zeros(shape, F32)
        for m in range(n_dc):
            t = jnp.where(dc == m, rpb_ref[base + dr * n_dc + m], t)
        toeplitz.append(jnp.where(col_valid, t, neg))
    left = lane < GRID_W
    rows = DEC_SEQ // GRID_W
    for n in range(DEC_SEQ // ATT_BLK):
        for rq in range(NA_BLK_ROWS):
            r = NA_BLK_ROWS * n + rq
            start_r = min(max(r - NA_ROWS // 2, 0), rows - NA_ROWS)
            for jp in range(NA_WIN_ROWS // 2):
                pair = []
                for j in (2 * jp, 2 * jp + 1):
                    key_row = NA_WIN_START[n] + j
                    valid = start_r <= key_row < start_r + NA_ROWS
                    pair.append(toeplitz[key_row - r + NA_ROWS - 1] if valid else neg)
                tile = pair[0] if pair[0] is pair[1] else jnp.where(left, pair[0], pair[1])
                o_ref[n, rq * GRID_W:(rq + 1) * GRID_W, jp * LANES:(jp + 1) * LANES] = tile


def _dbias_call(rpb_d):
    nblk = DEC_SEQ // ATT_BLK
    return pl.pallas_call(
        _dbias_kernel,
        out_shape=jax.ShapeDtypeStruct((DEPTH, 4, nblk, ATT_BLK, NA_WIN), F32),
        grid=(DEPTH, 4),
        in_specs=[pl.BlockSpec(memory_space=pltpu.SMEM)],
        out_specs=pl.BlockSpec((None, None, nblk, ATT_BLK, NA_WIN), lambda l, h: (l, h, 0, 0, 0)),
        name="dbias",
    )(rpb_d.reshape(-1))


def _swap_halves(v, half):
    lane = lax.broadcasted_iota(jnp.int32, v.shape, 1)
    up = pltpu.roll(v, LANES - half, 1)
    dn = pltpu.roll(v, half, 1)
    return jnp.where((lane & (2 * half - 1)) < half, up, dn)


def _rope(v, cos, sin, half):
    return v * cos + _swap_halves(v, half) * sin


def _pre_kernel(*refs, prompt, first):
    if prompt:
        (x_ref, sh_ref, sc_ref, g1_ref, w_ref, gq_ref, gk_ref, bd_ref) = refs[:8]
        (qkv_ref, gates_ref, ka_ref, va_ref, kb_ref, vb_ref, kc_ref, vc_ref, kd_ref, vd_ref) = refs[-10:]
    else:
        (x_ref, sh_ref, sc_ref, g1_ref, w_ref, gq_ref, gk_ref, bd_ref,
         ca_ref, sa_ref, cb_ref, sb_ref, qkv_ref, gates_ref) = refs

    h = (_rms(x_ref[...], g1_ref[...], NORM_EPS) * (1.0 + sc_ref[...]) + sh_ref[...]).astype(BF16)
    bd = bd_ref[...]
    ones = jnp.ones((h.shape[0], LANES), BF16)

    def rope_a(v):
        return v if prompt else _rope(v, ca_ref[...], sa_ref[...], 16)

    def rope_b(v):
        return v if prompt else _rope(v, cb_ref[...], sb_ref[...], 8)

    def put(col, v):
        qkv_ref[:, col:col + v.shape[1]] = v.astype(BF16)

    def put_values(col, v):
        for c in range(v.shape[1] // LANES):
            put(col + c * VAUG, v[:, c * LANES:(c + 1) * LANES])
            put(col + c * VAUG + LANES, ones)

    def put_heads(ref, v):
        for bi in range(v.shape[0] // SEQ):
            for pr in range(v.shape[1] // LANES):
                t = v[bi * SEQ:(bi + 1) * SEQ, pr * LANES:(pr + 1) * LANES].T
                for hh in range(2):
                    piece = t[hh * HEAD_DIM:(hh + 1) * HEAD_DIM]
                    if first:
                        ref[bi, 0, 2 * pr + hh] = piece
                        for later in range(1, DEPTH):
                            ref[bi, later, 2 * pr + hh] = jnp.zeros_like(piece)
                    else:
                        ref[bi, 2 * pr + hh] = piece

    scale = HEAD_DIM ** -0.5
    acc = _dot(h, w_ref[:, W_A:W_A + 512])
    gq = gq_ref[...]
    q_e, q_o = _interleave_heads(acc[:, 0:128], acc[:, 128:256])
    put(A_QE, rope_a(_head_rms(q_e, bd, gq, NORM_EPS)) * scale)
    put(A_QO, rope_a(_head_rms(q_o, bd, gq, NORM_EPS)) * scale)
    k_a = _head_rms(acc[:, 256:384], bd, gk_ref[...], NORM_EPS)
    put(A_K, rope_a(k_a))
    put_values(A_V, acc[:, 384:512])
    if prompt:
        put_heads(ka_ref, k_a)
        put_heads(va_ref, acc[:, 384:512])
    acc = _dot(h, w_ref[:, W_B:W_B + 768])
    scale_b = B_HALF ** -0.5
    for c in range(2):
        put(B_Q + c * 128, rope_b(acc[:, c * 128:(c + 1) * 128]) * scale_b)
        put(B_K + c * 128, rope_b(acc[:, 256 + c * 128:256 + (c + 1) * 128]))
    put_values(B_V, acc[:, 512:768])
    if prompt:
        put_heads(kb_ref, acc[:, 256:512])
        put_heads(vb_ref, acc[:, 512:768])
    acc = _dot(h, w_ref[:, W_C:W_C + 512])
    q_e, q_o = _interleave_heads(acc[:, 0:128], acc[:, 128:256])
    put(C_QE, rope_a(q_e) * scale)
    put(C_QO, rope_a(q_o) * scale)
    put(C_K, rope_a(acc[:, 256:384]))
    put_values(C_V, acc[:, 384:512])
    if prompt:
        put_heads(kc_ref, acc[:, 256:384])
        put_heads(vc_ref, acc[:, 384:512])
    acc = _dot(h, w_ref[:, W_D:W_D + 768])
    put(D_Q, acc[:, 0:256] * scale)
    put(D_K, acc[:, 256:512])
    put_values(D_V, acc[:, 512:768])
    if prompt:
        put_heads(kd_ref, acc[:, 256:512])
        put_heads(vd_ref, acc[:, 512:768])
    for j in range(GATE_W // 512):
        gates_ref[:, j * 512:(j + 1) * 512] = _dot(h, w_ref[:, W_GATES + j * 512:W_GATES + (j + 1) * 512]).astype(BF16)


def _pre_call(l, x, mod, g1, w_in, gq, gk, bd128, rope_tabs, kv_prev, *, prompt):
    rows = x.shape[0]
    tm = 512
    tq = tm
    bpb = tm // SEQ
    aliases = {}
    if prompt:
        mod_row = lambda i: CTX_ROW
    else:
        mod_row = lambda i: i // (DEC_SEQ // tm)
    in_specs = [
        pl.BlockSpec((tm, D_MODEL), lambda i: (i, 0)),
        pl.BlockSpec((None, None, 1, D_MODEL), lambda i: (l, mod_row(i), 0, 0)),
        pl.BlockSpec((None, None, 1, D_MODEL), lambda i: (l, mod_row(i), 0, 1)),
        pl.BlockSpec((None, 1, D_MODEL), lambda i: (l, 0, 0)),
        pl.BlockSpec((None, D_MODEL, IN_COLS), lambda i: (l, 0, 0), pipeline_mode=pl.Buffered(1)),
        pl.BlockSpec((None, 1, LANES), lambda i: (l, 0, 0)),
        pl.BlockSpec((None, 1, LANES), lambda i: (l, 0, 0)),
        pl.BlockSpec((LANES, LANES), lambda i: (0, 0)),
    ]
    args = [x, mod, mod, g1, w_in, gq, gk, bd128]
    out_shape = [jax.ShapeDtypeStruct((rows, QKV_W), BF16), jax.ShapeDtypeStruct((rows, GATE_W), BF16)]
    out_specs = [pl.BlockSpec((tm, QKV_W), lambda i: (i, 0)), pl.BlockSpec((tm, GATE_W), lambda i: (i, 0))]
    if prompt:
        for j, nh in enumerate((2, 2, 4, 4, 2, 2, 4, 4)):
            out_shape.append(jax.ShapeDtypeStruct((BATCH, DEPTH, nh, HEAD_DIM, SEQ), F32))
            if kv_prev is None:
                out_specs.append(pl.BlockSpec((bpb, DEPTH, nh, HEAD_DIM, SEQ), lambda i: (i, 0, 0, 0, 0)))
            else:
                out_specs.append(pl.BlockSpec((bpb, None, nh, HEAD_DIM, SEQ), lambda i: (i, l, 0, 0, 0)))
                aliases[len(args)] = 2 + j
                in_specs.append(pl.BlockSpec(memory_space=pl.ANY))
                args.append(kv_prev[j])
    else:
        nt = DEC_SEQ // tq
        for t in rope_tabs:
            in_specs.append(pl.BlockSpec((tq, LANES), lambda i: (i % nt, 0)))
            args.append(t)
    return pl.pallas_call(
        functools.partial(_pre_kernel, prompt=prompt, first=kv_prev is None),
        out_shape=out_shape,
        grid=(rows // tm,),
        in_specs=in_specs,
        out_specs=out_specs,
        input_output_aliases=aliases,
        compiler_params=pltpu.CompilerParams(vmem_limit_bytes=VMEM_LIMIT),
        name="pre_prompt" if prompt else "pre_latent",
    )(*args)


class _Transposed:
    def __init__(self, a):
        self.a = a


def _scores(qs, keys, biases):
    out = []
    for k, b in zip(keys, biases):
        s = _dot(qs, k.a) if isinstance(k, _Transposed) else _dot_nt(qs, k)
        out.append(s if b is None else s + b)
    return out


def _row_max(s_list, sink):
    m = None
    for s in s_list:
        mi = jnp.max(s, axis=-1, keepdims=True)
        m = mi if m is None else jnp.maximum(m, mi)
    return m if sink is None else jnp.maximum(m, sink)


def _softmax_pv(s_list, vaugs, sink=None):
    m = _row_max(s_list, sink)
    r = None
    for s, v in zip(s_list, vaugs):
        p = jnp.exp((s - m).astype(BF16))
        ri = _dot_nt(p, v.a) if isinstance(v, _Transposed) else _dot(p, v)
        r = ri if r is None else r + ri
    den = r[:, LANES:]
    if sink is not None:
        den = den + jnp.exp(sink - m)
    return r[:, :LANES] / den


def _gqa(qe, qo, keys, vaugs, biases, sinks):
    mq = qe.shape[0]
    oe = jnp.zeros((mq, LANES), F32)
    oo = jnp.zeros((mq, LANES), F32)
    for g in range(2):
        lo, hi = HEAD_DIM * g, HEAD_DIM * (g + 1)
        qs = jnp.concatenate([_mask_q(qe, lo, hi), _mask_q(qo, lo, hi)], axis=0)
        sink = None
        if sinks is not None:
            row = lax.broadcasted_iota(jnp.int32, (2 * mq, 1), 0)
            sink = jnp.where(row < mq, sinks[2 * g], sinks[2 * g + 1])
        r = _softmax_pv(_scores(qs, keys, biases), vaugs, sink)
        msk = _lane_mask((mq, LANES), lo, hi)
        oe = jnp.where(msk, r[:mq], oe)
        oo = jnp.where(msk, r[mq:], oo)
    return _interleave_heads(oe, oo)


def _diff(q, keys, vaug_fn, lam):
    mq = q.shape[0]
    out = [jnp.zeros((mq, LANES), F32), jnp.zeros((mq, LANES), F32)]
    for hd in range(4):
        lo = HEAD_DIM * hd
        qs = jnp.concatenate([_mask_q(q, lo, lo + B_HALF), _mask_q(q, lo + B_HALF, lo + HEAD_DIM)], axis=0)
        o = _softmax_pv(_scores(qs, keys, [None] * len(keys)), vaug_fn(hd // 2))
        o = o[:mq] - lam * o[mq:]
        plo = HEAD_DIM * (hd % 2)
        out[hd // 2] = jnp.where(_lane_mask(o.shape, plo, plo + HEAD_DIM), o, out[hd // 2])
    return out


def _mha(q, keys, vaug_fn, bias_fn):
    mq = q.shape[0]
    out = [jnp.zeros((mq, LANES), F32), jnp.zeros((mq, LANES), F32)]
    for hd in range(4):
        lo = HEAD_DIM * hd
        qs = _mask_q(q, lo, lo + HEAD_DIM)
        o = _softmax_pv(_scores(qs, keys, bias_fn(hd)), vaug_fn(hd // 2))
        plo = HEAD_DIM * (hd % 2)
        out[hd // 2] = jnp.where(_lane_mask(o.shape, plo, plo + HEAD_DIM), o, out[hd // 2])
    return out


def _lambda(lam_ref, lam_init):
    lp = lam_ref[...]
    a = jnp.sum(lp[0:1, :] * lp[1:2, :], axis=-1, keepdims=True)
    b = jnp.sum(lp[2:3, :] * lp[3:4, :], axis=-1, keepdims=True)
    return jnp.exp(a) - jnp.exp(b) + lam_init


def _store_branches(br_ref, oa, ob, oc, od, bd_ref, gsub_ref, lam_init):
    bd = bd_ref[...]
    ob = [_head_rms(o, bd, gsub_ref[...], SUBLN_EPS) * (1.0 - lam_init) for o in ob]
    for j, o in enumerate((*oa, *ob, *oc, *od)):
        br_ref[:, j * LANES:(j + 1) * LANES] = o.astype(BF16)


def _attn_prompt_kernel(qkv_ref, sink_ref, lam_ref, gsub_ref, bd_ref, br_ref, *, l, lam_init):
    def cols(c, w):
        return qkv_ref[:, c:c + w]

    def qcols(c, w):
        return cols(c, w).astype(F32)

    sinks = [sink_ref[l, i] for i in range(4)]
    oa = _gqa(qcols(A_QE, 128), qcols(A_QO, 128), [cols(A_K, 128)], [cols(A_V, VAUG)], [None], None)
    lam = _lambda(lam_ref, lam_init)
    ob = _diff(qcols(B_Q, 256), [cols(B_K, 256)], lambda pr: [cols(B_V + pr * VAUG, VAUG)], lam)
    oc = _gqa(qcols(C_QE, 128), qcols(C_QO, 128), [cols(C_K, 128)], [cols(C_V, VAUG)], [None], sinks)
    od = _mha(qcols(D_Q, 256), [cols(D_K, 256)], lambda pr: [cols(D_V + pr * VAUG, VAUG)], lambda hd: [None])
    _store_branches(br_ref, oa, ob, oc, od, bd_ref, gsub_ref, lam_init)


def _attn_prompt_call(l, lam_init, qkv, sink_c, lam_b, gsub, bd128):
    rows = qkv.shape[0]
    return pl.pallas_call(
        functools.partial(_attn_prompt_kernel, l=l, lam_init=lam_init),
        out_shape=jax.ShapeDtypeStruct((rows, D_MODEL), BF16),
        grid=(rows // SEQ,),
        in_specs=[
            pl.BlockSpec((SEQ, QKV_W), lambda b: (b, 0)),
            pl.BlockSpec(memory_space=pltpu.SMEM),
            pl.BlockSpec((None, 4, B_HALF), lambda b: (l, 0, 0)),
            pl.BlockSpec((None, 1, LANES), lambda b: (l, 0, 0)),
            pl.BlockSpec((LANES, LANES), lambda b: (0, 0)),
        ],
        out_specs=pl.BlockSpec((SEQ, D_MODEL), lambda b: (b, 0)),
        compiler_params=pltpu.CompilerParams(vmem_limit_bytes=VMEM_LIMIT),
        name="attn_prompt",
    )(qkv, sink_c, lam_b, gsub, bd128)


def _attn_latent_kernel(kv_ref, xak_ref, xav_ref, xbk_ref, xbv_ref, xck_ref, xcv_ref, xdk_ref, xdv_ref,
                        bias_ref, sink_ref, lam_ref, gsub_ref, bd_ref, br_ref, *, l, lam_init):
    n = pl.program_id(1)
    q0 = pl.multiple_of(n * ATT_BLK, ATT_BLK)

    def q(c, w):
        return kv_ref[pl.ds(q0, ATT_BLK), c:c + w].astype(F32)

    def lat(c, w):
        return kv_ref[:, c:c + w]

    def ctx_k(ref):
        return _Transposed(ref[...].reshape(ref.shape[0] * HEAD_DIM, PAST_LEN).astype(BF16))

    def ctx_v(ref, pr):
        vt = ref[2 * pr:2 * pr + 2].reshape(LANES, PAST_LEN).astype(BF16)
        return _Transposed(jnp.concatenate([vt, jnp.ones((LANES, PAST_LEN), BF16)], axis=0))

    sinks = [sink_ref[l, i] for i in range(4)]
    oa = _gqa(q(A_QE, 128), q(A_QO, 128), [ctx_k(xak_ref), lat(A_K, 128)], [ctx_v(xav_ref, 0), lat(A_V, VAUG)],
              [None, None], None)
    lam = _lambda(lam_ref, lam_init)
    ob = _diff(q(B_Q, 256), [ctx_k(xbk_ref), lat(B_K, 256)],
               lambda pr: [ctx_v(xbv_ref, pr), lat(B_V + pr * VAUG, VAUG)], lam)
    start_c = pl.multiple_of(jnp.clip(q0 - C_WINDOW, 0, DEC_SEQ - C_WIN), C_WINDOW)
    rowq = lax.broadcasted_iota(jnp.int32, (2 * ATT_BLK, C_WIN), 0) & (ATT_BLK - 1)
    colk = lax.broadcasted_iota(jnp.int32, (2 * ATT_BLK, C_WIN), 1)
    band = jnp.where(jnp.abs(rowq - colk + (q0 - start_c)) <= C_WINDOW, 0.0, NEG_INF)
    oc = _gqa(q(C_QE, 128), q(C_QO, 128),
              [ctx_k(xck_ref), kv_ref[pl.ds(start_c, C_WIN), C_K:C_K + 128]],
              [ctx_v(xcv_ref, 0), kv_ref[pl.ds(start_c, C_WIN), C_V:C_V + VAUG]],
              [None, band], sinks)
    start_d = pl.multiple_of(jnp.where(n >= 2, NA_WIN_START[2] * GRID_W, 0), ATT_BLK)
    kwin = kv_ref[pl.ds(start_d, NA_WIN), D_K:D_K + 256]
    od = _mha(q(D_Q, 256), [ctx_k(xdk_ref), kwin],
              lambda pr: [ctx_v(xdv_ref, pr),
                          kv_ref[pl.ds(start_d, NA_WIN), D_V + pr * VAUG:D_V + (pr + 1) * VAUG]],
              lambda hd: [None, bias_ref[hd, n]])
    _store_branches(br_ref, oa, ob, oc, od, bd_ref, gsub_ref, lam_init)


def _attn_latent_call(l, lam_init, qkv, caches, dbias, sink_c, lam_b, gsub, bd128):
    rows = qkv.shape[0]
    nblk = DEC_SEQ // ATT_BLK
    cache_specs = [pl.BlockSpec((None, None, t.shape[2], HEAD_DIM, PAST_LEN), lambda b, n: (b, l, 0, 0, 0))
                   for t in caches]
    return pl.pallas_call(
        functools.partial(_attn_latent_kernel, l=l, lam_init=lam_init),
        out_shape=jax.ShapeDtypeStruct((rows, D_MODEL), BF16),
        grid=(DEC_BATCH, nblk),
        in_specs=[
            pl.BlockSpec((DEC_SEQ, QKV_W), lambda b, n: (b, 0)),
            *cache_specs,
            pl.BlockSpec((None, 4, nblk, ATT_BLK, NA_WIN), lambda b, n: (l, 0, 0, 0, 0), pipeline_mode=pl.Buffered(1)),
            pl.BlockSpec(memory_space=pltpu.SMEM),
            pl.BlockSpec((None, 4, B_HALF), lambda b, n: (l, 0, 0)),
            pl.BlockSpec((None, 1, LANES), lambda b, n: (l, 0, 0)),
            pl.BlockSpec((LANES, LANES), lambda b, n: (0, 0)),
        ],
        out_specs=pl.BlockSpec((ATT_BLK, D_MODEL), lambda b, n: (b * nblk + n, 0)),
        compiler_params=pltpu.CompilerParams(vmem_limit_bytes=VMEM_LIMIT),
        name="attn_latent",
    )(qkv, *caches, dbias, sink_c, lam_b, gsub, bd128)


MXU_TILE = 256
FFN_CHUNKS = ((0, 6 * MXU_TILE), (6 * MXU_TILE, D_FF))


def _post_kernel(x_ref, br_ref, gates_ref, gt1_ref, sh2_ref, sc2_ref, gt2_ref, g2_ref,
                 wb_ref, wo_ref, wfi_ref, wfo_ref, gf_ref, o_ref, *, final):
    merged = None
    for k in range(4):
        proj = _dot(br_ref[:, k * 256:(k + 1) * 256], wb_ref[k])
        t = _sigmoid(gates_ref[:, k * D_MODEL:(k + 1) * D_MODEL].astype(F32)) * proj
        merged = t if merged is None else merged + t
    x1 = x_ref[...] + gt1_ref[...] * _dot(merged.astype(BF16), wo_ref[...])
    h2 = (_rms(x1, g2_ref[...], NORM_EPS) * (1.0 + sc2_ref[...]) + sh2_ref[...]).astype(BF16)
    acc = None
    for c0, c1 in FFN_CHUNKS:
        a = _dot(h2, wfi_ref[:, c0:c1])
        u = _dot(h2, wfi_ref[:, D_FF + c0:D_FF + c1])
        g = ((a * _sigmoid(a)) * u).astype(BF16)
        t = _dot(g, wfo_ref[c0:c1, :])
        acc = t if acc is None else acc + t
    xo = x1 + gt2_ref[...] * acc
    if final:
        xo = _rms(xo, gf_ref[...], NORM_EPS)
    o_ref[...] = xo


def _post_call(l, x, br, gates, mod, g2, wb, wo, wfi, wfo, gf, *, prompt, final):
    rows = x.shape[0]
    tm = 512
    if prompt:
        mod_row = lambda i: CTX_ROW
    else:
        mod_row = lambda i: i // (DEC_SEQ // tm)

    def mod_spec(chunk):
        return pl.BlockSpec((None, None, 1, D_MODEL), lambda i: (l, mod_row(i), 0, chunk))

    def resident(shape):
        nd = len(shape)
        return pl.BlockSpec((None,) + shape, lambda i: (l,) + (0,) * nd, pipeline_mode=pl.Buffered(1))

    return pl.pallas_call(
        functools.partial(_post_kernel, final=final),
        out_shape=jax.ShapeDtypeStruct((rows, D_MODEL), F32),
        grid=(rows // tm,),
        in_specs=[
            pl.BlockSpec((tm, D_MODEL), lambda i: (i, 0)),
            pl.BlockSpec((tm, D_MODEL), lambda i: (i, 0)),
            pl.BlockSpec((tm, GATE_W), lambda i: (i, 0)),
            mod_spec(2), mod_spec(3), mod_spec(4), mod_spec(5),
            pl.BlockSpec((None, 1, D_MODEL), lambda i: (l, 0, 0)),
            resident((4, 256, D_MODEL)),
            resident((D_MODEL, D_MODEL)),
            resident((D_MODEL, 2 * D_FF)),
            resident((D_FF, D_MODEL)),
            pl.BlockSpec((1, D_MODEL), lambda i: (0, 0)),
        ],
        out_specs=pl.BlockSpec((tm, D_MODEL), lambda i: (i, 0)),
        compiler_params=pltpu.CompilerParams(vmem_limit_bytes=VMEM_LIMIT),
        name="post_prompt" if prompt else "post_latent",
    )(x, br, gates, mod, mod, mod, mod, g2, wb, wo, wfi, wfo, gf)


def _rope_tables():
    t = jnp.arange(DEC_SEQ)
    row = (t // GRID_W).astype(F32)[:, None]
    col = (t % GRID_W).astype(F32)[:, None]
    tabs = []
    for d in (HEAD_DIM, B_HALF):
        quarter = d // 4
        inv = jnp.power(jnp.float32(ROPE_THETA), -jnp.arange(quarter, dtype=F32) / quarter)
        ar, ac = row * inv, col * inv
        cos = jnp.concatenate([jnp.cos(ar), jnp.cos(ar), jnp.cos(ac), jnp.cos(ac)], axis=-1)
        sin = jnp.concatenate([-jnp.sin(ar), jnp.sin(ar), -jnp.sin(ac), jnp.sin(ac)], axis=-1)
        reps = LANES // d
        tabs += [jnp.tile(cos, (1, reps)), jnp.tile(sin, (1, reps))]
    return tabs


def _block_diag_mean(width):
    idx = np.arange(width) // HEAD_DIM
    return jnp.asarray((idx[:, None] == idx[None, :]).astype(np.float32) / HEAD_DIM, dtype=BF16)


def kernel(x_prompt, x_sample, cache_a_k, cache_a_v, cache_b_k, cache_b_v, cache_c_k, cache_c_v, cache_d_k, cache_d_v, c, c_ctx, w_ada, b_ada, g_norm1, w_in, g_q_a, g_k_a, lam_b, g_subln_b, sink_c, rpb_d, w_branch, w_out, g_norm2, w_ffn_in, w_ffn_out, g_final):
    w_in_b = w_in.astype(BF16)
    wb = w_branch.astype(BF16)
    wo = w_out.astype(BF16)
    wfi = w_ffn_in.astype(BF16)
    wfo = w_ffn_out.astype(BF16)
    gq = jnp.tile(g_q_a, (1, 2)).reshape(DEPTH, 1, LANES)
    gk = jnp.tile(g_k_a, (1, 2)).reshape(DEPTH, 1, LANES)
    gsub = jnp.tile(g_subln_b, (1, 2)).reshape(DEPTH, 1, LANES)
    g1 = g_norm1.reshape(DEPTH, 1, D_MODEL)
    g2 = g_norm2.reshape(DEPTH, 1, D_MODEL)
    gf = g_final.reshape(1, D_MODEL)
    bd128 = _block_diag_mean(LANES)
    rope_tabs = _rope_tables()
    caches = tuple(jnp.swapaxes(t, 3, 4) for t in (cache_a_k, cache_a_v, cache_b_k, cache_b_v,
                                                   cache_c_k, cache_c_v, cache_d_k, cache_d_v))

    cond = jnp.concatenate([c, c_ctx[None, :], jnp.zeros((MOD_ROWS - DEC_BATCH - 1, D_MODEL), F32)], axis=0)
    mod = _ada_call(cond, w_ada, b_ada).reshape(DEPTH, MOD_ROWS, 1, 6 * D_MODEL)
    dbias = _dbias_call(rpb_d)

    xp = x_prompt.reshape(BATCH * SEQ, D_MODEL)
    xs = x_sample.reshape(DEC_BATCH * DEC_SEQ, D_MODEL)
    new_kv = None
    for l in range(DEPTH):
        lam_init = 0.8 - 0.6 * math.exp(-0.3 * l)
        final = l == DEPTH - 1
        outs = _pre_call(l, xp, mod, g1, w_in_b, gq, gk, bd128, None, new_kv, prompt=True)
        qkv_p, gates_p, new_kv = outs[0], outs[1], outs[2:]
        br_p = _attn_prompt_call(l, lam_init, qkv_p, sink_c, lam_b, gsub, bd128)
        xp = _post_call(l, xp, br_p, gates_p, mod, g2, wb, wo, wfi, wfo, gf, prompt=True, final=final)

        qkv_s, gates_s = _pre_call(l, xs, mod, g1, w_in_b, gq, gk, bd128, rope_tabs, None, prompt=False)
        br_s = _attn_latent_call(l, lam_init, qkv_s, caches, dbias, sink_c, lam_b, gsub, bd128)
        xs = _post_call(l, xs, br_s, gates_s, mod, g2, wb, wo, wfi, wfo, gf, prompt=False, final=final)

    y_prompt = xp.reshape(BATCH, SEQ, D_MODEL)
    y_sample = xs.reshape(DEC_BATCH, DEC_SEQ, D_MODEL)
    return (y_prompt, y_sample, *(jnp.swapaxes(t, 3, 4) for t in new_kv))
```

```python
import functools
import math

import numpy as np
import jax
import jax.numpy as jnp
from jax import lax
from jax.experimental import pallas as pl
from jax.experimental.pallas import tpu as pltpu

F32 = jnp.float32
BF16 = jnp.bfloat16

D_MODEL = 1024
BATCH = 32
SEQ = 256
DEPTH = 2
DEC_BATCH = 8
DEC_SEQ = 1024
PAST_LEN = 256
GRID_W = 64
HEAD_DIM = 64
B_HALF = HEAD_DIM // 2
C_WINDOW = 128
NA_ROWS = 8
NA_COLS = 16
D_FF = 2816
ROPE_THETA = 10000.0
NORM_EPS = 1e-6
SUBLN_EPS = 1e-5
NEG_INF = -1e30

GATE_W = 4 * D_MODEL
W_A, W_B, W_C, W_D, W_GATES = 0, 512, 1280, 1792, 2560
IN_COLS = W_GATES + GATE_W
MOD_ROWS = 16
CTX_ROW = DEC_BATCH
LANES = 128
ATT_BLK = 256
NA_BLK_ROWS = ATT_BLK // GRID_W
NA_WIN_ROWS = 12
NA_WIN = NA_WIN_ROWS * GRID_W
NA_WIN_START = (0, 0, 4, 4)
C_WIN = ATT_BLK + 2 * C_WINDOW
VMEM_LIMIT = 56 * 1024 * 1024

A_QE, A_QO, A_K, A_V = 0, 128, 256, 384
B_Q, B_K, B_V = 640, 896, 1152
C_QE, C_QO, C_K, C_V = 1664, 1792, 1920, 2048
D_Q, D_K, D_V = 2304, 2560, 2816
QKV_W = 3328
VAUG = 2 * LANES


def _dot(a, b):
    return jnp.dot(a, b, preferred_element_type=F32)


def _dot_nt(a, b):
    return lax.dot_general(a, b, (((1,), (1,)), ((), ())), preferred_element_type=F32)


def _sigmoid(x):
    return 0.5 * jnp.tanh(0.5 * x) + 0.5


def _rms(x, g, eps):
    ms = jnp.mean(x * x, axis=-1, keepdims=True)
    return x * lax.rsqrt(ms + eps) * g


def _head_rms(v, bd, g, eps):
    v2 = v * v
    hi = v2.astype(BF16)
    lo = (v2 - hi.astype(F32)).astype(BF16)
    ms = _dot(hi, bd) + _dot(lo, bd)
    return v * lax.rsqrt(ms + eps) * g


def _lane_mask(shape, lo, hi):
    lane = lax.broadcasted_iota(jnp.int32, shape, 1)
    return (lane >= lo) & (lane < hi)


def _mask_q(qf, lo, hi):
    return jnp.where(_lane_mask(qf.shape, lo, hi), qf, 0.0).astype(BF16)


def _interleave_heads(lo, hi):
    left = _lane_mask(lo.shape, 0, HEAD_DIM)
    return (jnp.where(left, lo, pltpu.roll(hi, HEAD_DIM, 1)),
            jnp.where(left, pltpu.roll(lo, HEAD_DIM, 1), hi))


def _ada_kernel(cond_ref, w_ref, b_ref, o_ref):
    c = cond_ref[...]
    s = (c * _sigmoid(c)).astype(BF16)
    o_ref[...] = _dot(s, w_ref[...].astype(BF16)) + b_ref[...]


def _ada_call(cond, w_ada, b_ada):
    tn = 1536
    return pl.pallas_call(
        _ada_kernel,
        out_shape=jax.ShapeDtypeStruct((DEPTH, MOD_ROWS, 6 * D_MODEL), F32),
        grid=(DEPTH, 6 * D_MODEL // tn),
        in_specs=[
            pl.BlockSpec((MOD_ROWS, D_MODEL), lambda l, j: (0, 0)),
            pl.BlockSpec((None, D_MODEL, tn), lambda l, j: (l, 0, j)),
            pl.BlockSpec((None, 1, tn), lambda l, j: (l, 0, j)),
        ],
        out_specs=pl.BlockSpec((None, MOD_ROWS, tn), lambda l, j: (l, 0, j)),
        compiler_params=pltpu.CompilerParams(vmem_limit_bytes=VMEM_LIMIT),
        name="ada",
    )(cond, w_ada, b_ada.reshape(DEPTH, 1, 6 * D_MODEL))


def _dbias_kernel(rpb_ref, o_ref):
    n_dr, n_dc = 2 * NA_ROWS - 1, 2 * NA_COLS - 1
    base = (pl.program_id(0) * 4 + pl.program_id(1)) * (n_dr * n_dc)
    shape = (GRID_W, LANES)
    cq = lax.broadcasted_iota(jnp.int32, shape, 0)
    lane = lax.broadcasted_iota(jnp.int32, shape, 1)
    ck = lane & (GRID_W - 1)
    dc = jnp.clip(ck - cq, -(NA_COLS - 1), NA_COLS - 1) + (NA_COLS - 1)
    start_c = jnp.clip(cq - NA_COLS // 2, 0, GRID_W - NA_COLS)
    col_valid = (ck >= start_c) & (ck < start_c + NA_COLS)
    neg = jnp.full(shape, NEG_INF, F32)
    toeplitz = []
    for dr in range(n_dr):
        t = jnp.zeros(shape, F32)
        for m in range(n_dc):
            t = jnp.where(dc == m, rpb_ref[base + dr * n_dc + m], t)
        toeplitz.append(jnp.where(col_valid, t, neg))
    left = lane < GRID_W
    rows = DEC_SEQ // GRID_W
    for n in range(DEC_SEQ // ATT_BLK):
        for rq in range(NA_BLK_ROWS):
            r = NA_BLK_ROWS * n + rq
            start_r = min(max(r - NA_ROWS // 2, 0), rows - NA_ROWS)
            for jp in range(NA_WIN_ROWS // 2):
                pair = []
                for j in (2 * jp, 2 * jp + 1):
                    key_row = NA_WIN_START[n] + j
                    valid = start_r <= key_row < start_r + NA_ROWS
                    pair.append(toeplitz[key_row - r + NA_ROWS - 1] if valid else neg)
                tile = pair[0] if pair[0] is pair[1] else jnp.where(left, pair[0], pair[1])
                o_ref[n, rq * GRID_W:(rq + 1) * GRID_W, jp * LANES:(jp + 1) * LANES] = tile


def _dbias_call(rpb_d):
    nblk = DEC_SEQ // ATT_BLK
    return pl.pallas_call(
        _dbias_kernel,
        out_shape=jax.ShapeDtypeStruct((DEPTH, 4, nblk, ATT_BLK, NA_WIN), F32),
        grid=(DEPTH, 4),
        in_specs=[pl.BlockSpec(memory_space=pltpu.SMEM)],
        out_specs=pl.BlockSpec((None, None, nblk, ATT_BLK, NA_WIN), lambda l, h: (l, h, 0, 0, 0)),
        name="dbias",
    )(rpb_d.reshape(-1))


def _swap_halves(v, half):
    lane = lax.broadcasted_iota(jnp.int32, v.shape, 1)
    up = pltpu.roll(v, LANES - half, 1)
    dn = pltpu.roll(v, half, 1)
    return jnp.where((lane & (2 * half - 1)) < half, up, dn)


def _rope(v, cos, sin, half):
    return v * cos + _swap_halves(v, half) * sin


def _pre_kernel(*refs, prompt, first):
    if prompt:
        (x_ref, sh_ref, sc_ref, g1_ref, w_ref, gq_ref, gk_ref, bd_ref) = refs[:8]
        (qkv_ref, gates_ref, ka_ref, va_ref, kb_ref, vb_ref, kc_ref, vc_ref, kd_ref, vd_ref) = refs[-10:]
    else:
        (x_ref, sh_ref, sc_ref, g1_ref, w_ref, gq_ref, gk_ref, bd_ref,
         ca_ref, sa_ref, cb_ref, sb_ref, qkv_ref, gates_ref) = refs

    h = (_rms(x_ref[...], g1_ref[...], NORM_EPS) * (1.0 + sc_ref[...]) + sh_ref[...]).astype(BF16)
    bd = bd_ref[...]
    ones = jnp.ones((h.shape[0], LANES), BF16)

    def rope_a(v):
        return v if prompt else _rope(v, ca_ref[...], sa_ref[...], 16)

    def rope_b(v):
        return v if prompt else _rope(v, cb_ref[...], sb_ref[...], 8)

    def put(col, v):
        qkv_ref[:, col:col + v.shape[1]] = v.astype(BF16)

    def put_values(col, v):
        for c in range(v.shape[1] // LANES):
            put(col + c * VAUG, v[:, c * LANES:(c + 1) * LANES])
            put(col + c * VAUG + LANES, ones)

    def put_heads(ref, v):
        for bi in range(v.shape[0] // SEQ):
            for pr in range(v.shape[1] // LANES):
                t = v[bi * SEQ:(bi + 1) * SEQ, pr * LANES:(pr + 1) * LANES].T
                for hh in range(2):
                    piece = t[hh * HEAD_DIM:(hh + 1) * HEAD_DIM]
                    if first:
                        ref[bi, 0, 2 * pr + hh] = piece
                        for later in range(1, DEPTH):
                            ref[bi, later, 2 * pr + hh] = jnp.zeros_like(piece)
                    else:
                        ref[bi, 2 * pr + hh] = piece

    scale = HEAD_DIM ** -0.5
    acc = _dot(h, w_ref[:, W_A:W_A + 512])
    gq = gq_ref[...]
    q_e, q_o = _interleave_heads(acc[:, 0:128], acc[:, 128:256])
    put(A_QE, rope_a(_head_rms(q_e, bd, gq, NORM_EPS)) * scale)
    put(A_QO, rope_a(_head_rms(q_o, bd, gq, NORM_EPS)) * scale)
    k_a = _head_rms(acc[:, 256:384], bd, gk_ref[...], NORM_EPS)
    put(A_K, rope_a(k_a))
    put_values(A_V, acc[:, 384:512])
    if prompt:
        put_heads(ka_ref, k_a)
        put_heads(va_ref, acc[:, 384:512])
    acc = _dot(h, w_ref[:, W_B:W_B + 768])
    scale_b = B_HALF ** -0.5
    for c in range(2):
        put(B_Q + c * 128, rope_b(acc[:, c * 128:(c + 1) * 128]) * scale_b)
        put(B_K + c * 128, rope_b(acc[:, 256 + c * 128:256 + (c + 1) * 128]))
    put_values(B_V, acc[:, 512:768])
    if prompt:
        put_heads(kb_ref, acc[:, 256:512])
        put_heads(vb_ref, acc[:, 512:768])
    acc = _dot(h, w_ref[:, W_C:W_C + 512])
    q_e, q_o = _interleave_heads(acc[:, 0:128], acc[:, 128:256])
    put(C_QE, rope_a(q_e) * scale)
    put(C_QO, rope_a(q_o) * scale)
    put(C_K, rope_a(acc[:, 256:384]))
    put_values(C_V, acc[:, 384:512])
    if prompt:
        put_heads(kc_ref, acc[:, 256:384])
        put_heads(vc_ref, acc[:, 384:512])
    acc = _dot(h, w_ref[:, W_D:W_D + 768])
    put(D_Q, acc[:, 0:256] * scale)
    put(D_K, acc[:, 256:512])
    put_values(D_V, acc[:, 512:768])
    if prompt:
        put_heads(kd_ref, acc[:, 256:512])
        put_heads(vd_ref, acc[:, 512:768])
    for j in range(GATE_W // 512):
        gates_ref[:, j * 512:(j + 1) * 512] = _dot(h, w_ref[:, W_GATES + j * 512:W_GATES + (j + 1) * 512]).astype(BF16)


def _pre_call(l, x, mod, g1, w_in, gq, gk, bd128, rope_tabs, kv_prev, *, prompt):
    rows = x.shape[0]
    tm = 512
    tq = tm
    bpb = tm // SEQ
    aliases = {}
    if prompt:
        mod_row = lambda i: CTX_ROW
    else:
        mod_row = lambda i: i // (DEC_SEQ // tm)
    in_specs = [
        pl.BlockSpec((tm, D_MODEL), lambda i: (i, 0)),
        pl.BlockSpec((None, None, 1, D_MODEL), lambda i: (l, mod_row(i), 0, 0)),
        pl.BlockSpec((None, None, 1, D_MODEL), lambda i: (l, mod_row(i), 0, 1)),
        pl.BlockSpec((None, 1, D_MODEL), lambda i: (l, 0, 0)),
        pl.BlockSpec((None, D_MODEL, IN_COLS), lambda i: (l, 0, 0), pipeline_mode=pl.Buffered(1)),
        pl.BlockSpec((None, 1, LANES), lambda i: (l, 0, 0)),
        pl.BlockSpec((None, 1, LANES), lambda i: (l, 0, 0)),
        pl.BlockSpec((LANES, LANES), lambda i: (0, 0)),
    ]
    args = [x, mod, mod, g1, w_in, gq, gk, bd128]
    out_shape = [jax.ShapeDtypeStruct((rows, QKV_W), BF16), jax.ShapeDtypeStruct((rows, GATE_W), BF16)]
    out_specs = [pl.BlockSpec((tm, QKV_W), lambda i: (i, 0)), pl.BlockSpec((tm, GATE_W), lambda i: (i, 0))]
    if prompt:
        for j, nh in enumerate((2, 2, 4, 4, 2, 2, 4, 4)):
            out_shape.append(jax.ShapeDtypeStruct((BATCH, DEPTH, nh, HEAD_DIM, SEQ), F32))
            if kv_prev is None:
                out_specs.append(pl.BlockSpec((bpb, DEPTH, nh, HEAD_DIM, SEQ), lambda i: (i, 0, 0, 0, 0)))
            else:
                out_specs.append(pl.BlockSpec((bpb, None, nh, HEAD_DIM, SEQ), lambda i: (i, l, 0, 0, 0)))
                aliases[len(args)] = 2 + j
                in_specs.append(pl.BlockSpec(memory_space=pl.ANY))
                args.append(kv_prev[j])
    else:
        nt = DEC_SEQ // tq
        for t in rope_tabs:
            in_specs.append(pl.BlockSpec((tq, LANES), lambda i: (i % nt, 0)))
            args.append(t)
    return pl.pallas_call(
        functools.partial(_pre_kernel, prompt=prompt, first=kv_prev is None),
        out_shape=out_shape,
        grid=(rows // tm,),
        in_specs=in_specs,
        out_specs=out_specs,
        input_output_aliases=aliases,
        compiler_params=pltpu.CompilerParams(vmem_limit_bytes=VMEM_LIMIT),
        name="pre_prompt" if prompt else "pre_latent",
    )(*args)


class _Transposed:
    def __init__(self, a):
        self.a = a


def _scores(qs, keys, biases):
    out = []
    for k, b in zip(keys, biases):
        s = _dot(qs, k.a) if isinstance(k, _Transposed) else _dot_nt(qs, k)
        out.append(s if b is None else s + b)
    return out


def _row_max(s_list, sink):
    m = None
    for s in s_list:
        mi = jnp.max(s, axis=-1, keepdims=True)
        m = mi if m is None else jnp.maximum(m, mi)
    return m if sink is None else jnp.maximum(m, sink)


def _softmax_pv(s_list, vaugs, sink=None):
    m = _row_max(s_list, sink)
    r = None
    for s, v in zip(s_list, vaugs):
        p = jnp.exp((s - m).astype(BF16))
        ri = _dot_nt(p, v.a) if isinstance(v, _Transposed) else _dot(p, v)
        r = ri if r is None else r + ri
    den = r[:, LANES:]
    if sink is not None:
        den = den + jnp.exp(sink - m)
    return r[:, :LANES] / den


def _gqa(qe, qo, keys, vaugs, biases, sinks):
    mq = qe.shape[0]
    oe = jnp.zeros((mq, LANES), F32)
    oo = jnp.zeros((mq, LANES), F32)
    for g in range(2):
        lo, hi = HEAD_DIM * g, HEAD_DIM * (g + 1)
        qs = jnp.concatenate([_mask_q(qe, lo, hi), _mask_q(qo, lo, hi)], axis=0)
        sink = None
        if sinks is not None:
            row = lax.broadcasted_iota(jnp.int32, (2 * mq, 1), 0)
            sink = jnp.where(row < mq, sinks[2 * g], sinks[2 * g + 1])
        r = _softmax_pv(_scores(qs, keys, biases), vaugs, sink)
        msk = _lane_mask((mq, LANES), lo, hi)
        oe = jnp.where(msk, r[:mq], oe)
        oo = jnp.where(msk, r[mq:], oo)
    return _interleave_heads(oe, oo)


def _diff(q, keys, vaug_fn, lam):
    mq = q.shape[0]
    out = [jnp.zeros((mq, LANES), F32), jnp.zeros((mq, LANES), F32)]
    for hd in range(4):
        lo = HEAD_DIM * hd
        qs = jnp.concatenate([_mask_q(q, lo, lo + B_HALF), _mask_q(q, lo + B_HALF, lo + HEAD_DIM)], axis=0)
        o = _softmax_pv(_scores(qs, keys, [None] * len(keys)), vaug_fn(hd // 2))
        o = o[:mq] - lam * o[mq:]
        plo = HEAD_DIM * (hd % 2)
        out[hd // 2] = jnp.where(_lane_mask(o.shape, plo, plo + HEAD_DIM), o, out[hd // 2])
    return out


def _mha(q, keys, vaug_fn, bias_fn):
    mq = q.shape[0]
    out = [jnp.zeros((mq, LANES), F32), jnp.zeros((mq, LANES), F32)]
    for hd in range(4):
        lo = HEAD_DIM * hd
        qs = _mask_q(q, lo, lo + HEAD_DIM)
        o = _softmax_pv(_scores(qs, keys, bias_fn(hd)), vaug_fn(hd // 2))
        plo = HEAD_DIM * (hd % 2)
        out[hd // 2] = jnp.where(_lane_mask(o.shape, plo, plo + HEAD_DIM), o, out[hd // 2])
    return out


def _lambda(lam_ref, lam_init):
    lp = lam_ref[...]
    a = jnp.sum(lp[0:1, :] * lp[1:2, :], axis=-1, keepdims=True)
    b = jnp.sum(lp[2:3, :] * lp[3:4, :], axis=-1, keepdims=True)
    return jnp.exp(a) - jnp.exp(b) + lam_init


def _store_branches(br_ref, oa, ob, oc, od, bd_ref, gsub_ref, lam_init):
    bd = bd_ref[...]
    ob = [_head_rms(o, bd, gsub_ref[...], SUBLN_EPS) * (1.0 - lam_init) for o in ob]
    for j, o in enumerate((*oa, *ob, *oc, *od)):
        br_ref[:, j * LANES:(j + 1) * LANES] = o.astype(BF16)


def _attn_prompt_kernel(qkv_ref, sink_ref, lam_ref, gsub_ref, bd_ref, br_ref, *, l, lam_init):
    sinks = [sink_ref[l, i] for i in range(4)]
    lam = _lambda(lam_ref, lam_init)
    for bi in range(qkv_ref.shape[0] // SEQ):
        r0 = bi * SEQ

        def cols(c, w):
            return qkv_ref[r0:r0 + SEQ, c:c + w]

        def qcols(c, w):
            return cols(c, w).astype(F32)

        oa = _gqa(qcols(A_QE, 128), qcols(A_QO, 128), [cols(A_K, 128)], [cols(A_V, VAUG)], [None], None)
        ob = _diff(qcols(B_Q, 256), [cols(B_K, 256)], lambda pr: [cols(B_V + pr * VAUG, VAUG)], lam)
        oc = _gqa(qcols(C_QE, 128), qcols(C_QO, 128), [cols(C_K, 128)], [cols(C_V, VAUG)], [None], sinks)
        od = _mha(qcols(D_Q, 256), [cols(D_K, 256)], lambda pr: [cols(D_V + pr * VAUG, VAUG)], lambda hd: [None])
        _store_branches(br_ref.at[r0:r0 + SEQ], oa, ob, oc, od, bd_ref, gsub_ref, lam_init)


def _attn_prompt_call(l, lam_init, qkv, sink_c, lam_b, gsub, bd128):
    rows = qkv.shape[0]
    tm = 4 * SEQ
    return pl.pallas_call(
        functools.partial(_attn_prompt_kernel, l=l, lam_init=lam_init),
        out_shape=jax.ShapeDtypeStruct((rows, D_MODEL), BF16),
        grid=(rows // tm,),
        in_specs=[
            pl.BlockSpec((tm, QKV_W), lambda b: (b, 0)),
            pl.BlockSpec(memory_space=pltpu.SMEM),
            pl.BlockSpec((None, 4, B_HALF), lambda b: (l, 0, 0)),
            pl.BlockSpec((None, 1, LANES), lambda b: (l, 0, 0)),
            pl.BlockSpec((LANES, LANES), lambda b: (0, 0)),
        ],
        out_specs=pl.BlockSpec((tm, D_MODEL), lambda b: (b, 0)),
        compiler_params=pltpu.CompilerParams(vmem_limit_bytes=VMEM_LIMIT),
        name="attn_prompt",
    )(qkv, sink_c, lam_b, gsub, bd128)


def _attn_latent_kernel(kv_ref, xak_ref, xav_ref, xbk_ref, xbv_ref, xck_ref, xcv_ref, xdk_ref, xdv_ref,
                        bias_ref, sink_ref, lam_ref, gsub_ref, bd_ref, br_ref, *, l, lam_init):
    sinks = [sink_ref[l, i] for i in range(4)]
    lam = _lambda(lam_ref, lam_init)
    nsub = br_ref.shape[0] // ATT_BLK
    for sub in range(nsub):
        n = pl.program_id(1) * nsub + sub
        q0 = pl.multiple_of(n * ATT_BLK, ATT_BLK)

        def q(c, w):
            return kv_ref[pl.ds(q0, ATT_BLK), c:c + w].astype(F32)

        def lat(c, w):
            return kv_ref[:, c:c + w]

        def ctx_k(ref):
            return _Transposed(ref[...].reshape(ref.shape[0] * HEAD_DIM, PAST_LEN).astype(BF16))

        def ctx_v(ref, pr):
            vt = ref[2 * pr:2 * pr + 2].reshape(LANES, PAST_LEN).astype(BF16)
            return _Transposed(jnp.concatenate([vt, jnp.ones((LANES, PAST_LEN), BF16)], axis=0))

        oa = _gqa(q(A_QE, 128), q(A_QO, 128), [ctx_k(xak_ref), lat(A_K, 128)], [ctx_v(xav_ref, 0), lat(A_V, VAUG)],
                  [None, None], None)
        ob = _diff(q(B_Q, 256), [ctx_k(xbk_ref), lat(B_K, 256)],
                   lambda pr: [ctx_v(xbv_ref, pr), lat(B_V + pr * VAUG, VAUG)], lam)
        start_c = pl.multiple_of(jnp.clip(q0 - C_WINDOW, 0, DEC_SEQ - C_WIN), C_WINDOW)
        rowq = lax.broadcasted_iota(jnp.int32, (2 * ATT_BLK, C_WIN), 0) & (ATT_BLK - 1)
        colk = lax.broadcasted_iota(jnp.int32, (2 * ATT_BLK, C_WIN), 1)
        band = jnp.where(jnp.abs(rowq - colk + (q0 - start_c)) <= C_WINDOW, 0.0, NEG_INF)
        oc = _gqa(q(C_QE, 128), q(C_QO, 128),
                  [ctx_k(xck_ref), kv_ref[pl.ds(start_c, C_WIN), C_K:C_K + 128]],
                  [ctx_v(xcv_ref, 0), kv_ref[pl.ds(start_c, C_WIN), C_V:C_V + VAUG]],
                  [None, band], sinks)
        start_d = pl.multiple_of(jnp.where(n >= 2, NA_WIN_START[2] * GRID_W, 0), ATT_BLK)
        kwin = kv_ref[pl.ds(start_d, NA_WIN), D_K:D_K + 256]
        od = _mha(q(D_Q, 256), [ctx_k(xdk_ref), kwin],
                  lambda pr: [ctx_v(xdv_ref, pr),
                              kv_ref[pl.ds(start_d, NA_WIN), D_V + pr * VAUG:D_V + (pr + 1) * VAUG]],
                  lambda hd: [None, bias_ref[hd, n]])
        _store_branches(br_ref.at[sub * ATT_BLK:(sub + 1) * ATT_BLK], oa, ob, oc, od, bd_ref, gsub_ref, lam_init)


def _attn_latent_call(l, lam_init, qkv, caches, dbias, sink_c, lam_b, gsub, bd128):
    rows = qkv.shape[0]
    nblk = DEC_SEQ // ATT_BLK
    nsub = 1
    cache_specs = [pl.BlockSpec((None, None, t.shape[2], HEAD_DIM, PAST_LEN), lambda b, n: (b, l, 0, 0, 0))
                   for t in caches]
    return pl.pallas_call(
        functools.partial(_attn_latent_kernel, l=l, lam_init=lam_init),
        out_shape=jax.ShapeDtypeStruct((rows, D_MODEL), BF16),
        grid=(DEC_BATCH, nblk // nsub),
        in_specs=[
            pl.BlockSpec((DEC_SEQ, QKV_W), lambda b, n: (b, 0)),
            *cache_specs,
            pl.BlockSpec((None, 4, nblk, ATT_BLK, NA_WIN), lambda b, n: (l, 0, 0, 0, 0), pipeline_mode=pl.Buffered(1)),
            pl.BlockSpec(memory_space=pltpu.SMEM),
            pl.BlockSpec((None, 4, B_HALF), lambda b, n: (l, 0, 0)),
            pl.BlockSpec((None, 1, LANES), lambda b, n: (l, 0, 0)),
            pl.BlockSpec((LANES, LANES), lambda b, n: (0, 0)),
        ],
        out_specs=pl.BlockSpec((nsub * ATT_BLK, D_MODEL), lambda b, n: (b * (nblk // nsub) + n, 0)),
        compiler_params=pltpu.CompilerParams(vmem_limit_bytes=VMEM_LIMIT),
        name="attn_latent",
    )(qkv, *caches, dbias, sink_c, lam_b, gsub, bd128)


MXU_TILE = 256
FFN_CHUNKS = ((0, 6 * MXU_TILE), (6 * MXU_TILE, D_FF))


def _post_kernel(x_ref, br_ref, gates_ref, gt1_ref, sh2_ref, sc2_ref, gt2_ref, g2_ref,
                 wb_ref, wo_ref, wfi_ref, wfo_ref, gf_ref, o_ref, *, final):
    merged = None
    for k in range(4):
        proj = _dot(br_ref[:, k * 256:(k + 1) * 256], wb_ref[k])
        t = (jnp.tanh(gates_ref[:, k * D_MODEL:(k + 1) * D_MODEL]) + 1.0).astype(F32) * proj
        merged = t if merged is None else merged + t
    x1 = x_ref[...] + gt1_ref[...] * _dot(merged.astype(BF16), wo_ref[...])
    h2 = (_rms(x1, g2_ref[...], NORM_EPS) * (1.0 + sc2_ref[...]) + sh2_ref[...]).astype(BF16)
    acc = None
    for c0, c1 in FFN_CHUNKS:
        a = _dot(h2, wfi_ref[:, c0:c1])
        u = _dot(h2, wfi_ref[:, D_FF + c0:D_FF + c1])
        g = ((a * _sigmoid(a)) * u).astype(BF16)
        t = _dot(g, wfo_ref[c0:c1, :])
        acc = t if acc is None else acc + t
    xo = x1 + gt2_ref[...] * acc
    if final:
        xo = _rms(xo, gf_ref[...], NORM_EPS)
    o_ref[...] = xo


def _post_call(l, x, br, gates, mod, g2, wb, wo, wfi, wfo, gf, *, prompt, final):
    rows = x.shape[0]
    tm = 512
    if prompt:
        mod_row = lambda i: CTX_ROW
    else:
        mod_row = lambda i: i // (DEC_SEQ // tm)

    def mod_spec(chunk):
        return pl.BlockSpec((None, None, 1, D_MODEL), lambda i: (l, mod_row(i), 0, chunk))

    def resident(shape):
        nd = len(shape)
        return pl.BlockSpec((None,) + shape, lambda i: (l,) + (0,) * nd, pipeline_mode=pl.Buffered(1))

    return pl.pallas_call(
        functools.partial(_post_kernel, final=final),
        out_shape=jax.ShapeDtypeStruct((rows, D_MODEL), F32),
        grid=(rows // tm,),
        in_specs=[
            pl.BlockSpec((tm, D_MODEL), lambda i: (i, 0)),
            pl.BlockSpec((tm, D_MODEL), lambda i: (i, 0)),
            pl.BlockSpec((tm, GATE_W), lambda i: (i, 0)),
            mod_spec(2), mod_spec(3), mod_spec(4), mod_spec(5),
            pl.BlockSpec((None, 1, D_MODEL), lambda i: (l, 0, 0)),
            resident((4, 256, D_MODEL)),
            resident((D_MODEL, D_MODEL)),
            resident((D_MODEL, 2 * D_FF)),
            resident((D_FF, D_MODEL)),
            pl.BlockSpec((1, D_MODEL), lambda i: (0, 0)),
        ],
        out_specs=pl.BlockSpec((tm, D_MODEL), lambda i: (i, 0)),
        compiler_params=pltpu.CompilerParams(vmem_limit_bytes=VMEM_LIMIT),
        name="post_prompt" if prompt else "post_latent",
    )(x, br, gates, mod, mod, mod, mod, g2, wb, wo, wfi, wfo, gf)


def _rope_tables():
    t = jnp.arange(DEC_SEQ)
    row = (t // GRID_W).astype(F32)[:, None]
    col = (t % GRID_W).astype(F32)[:, None]
    tabs = []
    for d in (HEAD_DIM, B_HALF):
        quarter = d // 4
        inv = jnp.power(jnp.float32(ROPE_THETA), -jnp.arange(quarter, dtype=F32) / quarter)
        ar, ac = row * inv, col * inv
        cos = jnp.concatenate([jnp.cos(ar), jnp.cos(ar), jnp.cos(ac), jnp.cos(ac)], axis=-1)
        sin = jnp.concatenate([-jnp.sin(ar), jnp.sin(ar), -jnp.sin(ac), jnp.sin(ac)], axis=-1)
        reps = LANES // d
        tabs += [jnp.tile(cos, (1, reps)), jnp.tile(sin, (1, reps))]
    return tabs


def _block_diag_mean(width):
    idx = np.arange(width) // HEAD_DIM
    return jnp.asarray((idx[:, None] == idx[None, :]).astype(np.float32) / HEAD_DIM, dtype=BF16)


def kernel(x_prompt, x_sample, cache_a_k, cache_a_v, cache_b_k, cache_b_v, cache_c_k, cache_c_v, cache_d_k, cache_d_v, c, c_ctx, w_ada, b_ada, g_norm1, w_in, g_q_a, g_k_a, lam_b, g_subln_b, sink_c, rpb_d, w_branch, w_out, g_norm2, w_ffn_in, w_ffn_out, g_final):
    gate_half = jnp.where(jnp.arange(IN_COLS) >= W_GATES, 0.5, 1.0).astype(F32)
    w_in_b = (w_in * gate_half).astype(BF16)
    wb = w_branch.astype(BF16)
    wo = (w_out * 0.5).astype(BF16)
    wfi = w_ffn_in.astype(BF16)
    wfo = w_ffn_out.astype(BF16)
    gq = jnp.tile(g_q_a, (1, 2)).reshape(DEPTH, 1, LANES)
    gk = jnp.tile(g_k_a, (1, 2)).reshape(DEPTH, 1, LANES)
    gsub = jnp.tile(g_subln_b, (1, 2)).reshape(DEPTH, 1, LANES)
    g1 = g_norm1.reshape(DEPTH, 1, D_MODEL)
    g2 = g_norm2.reshape(DEPTH, 1, D_MODEL)
    gf = g_final.reshape(1, D_MODEL)
    bd128 = _block_diag_mean(LANES)
    rope_tabs = _rope_tables()
    caches = tuple(jnp.swapaxes(t, 3, 4) for t in (cache_a_k, cache_a_v, cache_b_k, cache_b_v,
                                                   cache_c_k, cache_c_v, cache_d_k, cache_d_v))

    cond = jnp.concatenate([c, c_ctx[None, :], jnp.zeros((MOD_ROWS - DEC_BATCH - 1, D_MODEL), F32)], axis=0)
    mod = _ada_call(cond, w_ada, b_ada).reshape(DEPTH, MOD_ROWS, 1, 6 * D_MODEL)
    dbias = _dbias_call(rpb_d)

    xp = x_prompt.reshape(BATCH * SEQ, D_MODEL)
    xs = x_sample.reshape(DEC_BATCH * DEC_SEQ, D_MODEL)
    new_kv = None
    for l in range(DEPTH):
        lam_init = 0.8 - 0.6 * math.exp(-0.3 * l)
        final = l == DEPTH - 1
        outs = _pre_call(l, xp, mod, g1, w_in_b, gq, gk, bd128, None, new_kv, prompt=True)
        qkv_p, gates_p, new_kv = outs[0], outs[1], outs[2:]
        br_p = _attn_prompt_call(l, lam_init, qkv_p, sink_c, lam_b, gsub, bd128)
        xp = _post_call(l, xp, br_p, gates_p, mod, g2, wb, wo, wfi, wfo, gf, prompt=True, final=final)

        qkv_s, gates_s = _pre_call(l, xs, mod, g1, w_in_b, gq, gk, bd128, rope_tabs, None, prompt=False)
        br_s = _attn_latent_call(l, lam_init, qkv_s, caches, dbias, sink_c, lam_b, gsub, bd128)
        xs = _post_call(l, xs, br_s, gates_s, mod, g2, wb, wo, wfi, wfo, gf, prompt=False, final=final)

    y_prompt = xp.reshape(BATCH, SEQ, D_MODEL)
    y_sample = xs.reshape(DEC_BATCH, DEC_SEQ, D_MODEL)
    return (y_prompt, y_sample, *(jnp.swapaxes(t, 3, 4) for t in new_kv))
```

```python
import functools
import math

import jax
import jax.numpy as jnp
from jax import lax
from jax.experimental import pallas as pl
from jax.experimental.pallas import tpu as pltpu

F32 = jnp.float32
BF16 = jnp.bfloat16

D_MODEL = 1024
BATCH = 32
SEQ = 256
DEPTH = 2
DEC_BATCH = 8
DEC_SEQ = 1024
PAST_LEN = 256
GRID_W = 64
HEAD_DIM = 64
B_HALF = HEAD_DIM // 2
C_WINDOW = 128
NA_ROWS = 8
NA_COLS = 16
D_FF = 2816
ROPE_THETA = 10000.0
NORM_EPS = 1e-6
SUBLN_EPS = 1e-5
NEG_INF = -1e30

GATE_W = 4 * D_MODEL
W_A, W_B, W_C, W_D, W_GATES = 0, 512, 1280, 1792, 2560
IN_COLS = W_GATES + GATE_W
MOD_ROWS = 16
CTX_ROW = DEC_BATCH
LANES = 128
ATT_BLK = 256
NA_BLK_ROWS = ATT_BLK // GRID_W
NA_WIN_ROWS = 12
NA_WIN = NA_WIN_ROWS * GRID_W
NA_WIN_START = (0, 0, 4, 4)
C_WIN = ATT_BLK + 2 * C_WINDOW
VMEM_LIMIT = 56 * 1024 * 1024

A_QE, A_QO, A_K, A_V = 0, 128, 256, 384
B_Q, B_K, B_V = 640, 896, 1152
C_QE, C_QO, C_K, C_V = 1664, 1792, 1920, 2048
D_Q, D_K, D_V = 2304, 2560, 2816
QKV_W = 3328
VAUG = 2 * LANES


def _dot(a, b):
    return jnp.dot(a, b, preferred_element_type=F32)


def _dot_nt(a, b):
    return lax.dot_general(a, b, (((1,), (1,)), ((), ())), preferred_element_type=F32)


def _sigmoid(x):
    return 0.5 * jnp.tanh(0.5 * x) + 0.5


def _rms(x, g, eps):
    ms = jnp.mean(x * x, axis=-1, keepdims=True)
    return x * lax.rsqrt(ms + eps) * g


def _head_rms_mxu(v, bd, g, eps):
    v2 = v * v
    hi = v2.astype(BF16)
    lo = (v2 - hi.astype(F32)).astype(BF16)
    ms = _dot(hi, bd) + _dot(lo, bd)
    return v * lax.rsqrt(ms + eps) * g


def _head_rms(v, g, eps):
    v2 = v * v
    left = _lane_mask(v.shape, 0, HEAD_DIM)
    s_left = jnp.sum(jnp.where(left, v2, 0.0), axis=-1, keepdims=True)
    s_right = jnp.sum(jnp.where(left, 0.0, v2), axis=-1, keepdims=True)
    ms = jnp.where(left, s_left, s_right) * (1.0 / HEAD_DIM)
    return v * lax.rsqrt(ms + eps) * g


def _lane_mask(shape, lo, hi):
    lane = lax.broadcasted_iota(jnp.int32, shape, 1)
    return (lane >= lo) & (lane < hi)


def _mask_q(qf, lo, hi):
    return jnp.where(_lane_mask(qf.shape, lo, hi), qf, 0.0).astype(BF16)


def _interleave_heads(lo, hi):
    left = _lane_mask(lo.shape, 0, HEAD_DIM)
    return (jnp.where(left, lo, pltpu.roll(hi, HEAD_DIM, 1)),
            jnp.where(left, pltpu.roll(lo, HEAD_DIM, 1), hi))


def _ada_kernel(cond_ref, w_ref, b_ref, o_ref):
    c = cond_ref[...]
    s = (c * _sigmoid(c)).astype(BF16)
    o_ref[...] = _dot(s, w_ref[...].astype(BF16)) + b_ref[...]


def _ada_call(cond, w_ada, b_ada):
    tn = 1536
    return pl.pallas_call(
        _ada_kernel,
        out_shape=jax.ShapeDtypeStruct((DEPTH, MOD_ROWS, 6 * D_MODEL), F32),
        grid=(DEPTH, 6 * D_MODEL // tn),
        in_specs=[
            pl.BlockSpec((MOD_ROWS, D_MODEL), lambda l, j: (0, 0)),
            pl.BlockSpec((None, D_MODEL, tn), lambda l, j: (l, 0, j)),
            pl.BlockSpec((None, 1, tn), lambda l, j: (l, 0, j)),
        ],
        out_specs=pl.BlockSpec((None, MOD_ROWS, tn), lambda l, j: (l, 0, j)),
        compiler_params=pltpu.CompilerParams(vmem_limit_bytes=VMEM_LIMIT),
        name="ada",
    )(cond, w_ada, b_ada.reshape(DEPTH, 1, 6 * D_MODEL))


def _dbias_kernel(rpb_ref, o_ref):
    n_dr, n_dc = 2 * NA_ROWS - 1, 2 * NA_COLS - 1
    base = (pl.program_id(0) * 4 + pl.program_id(1)) * (n_dr * n_dc)
    shape = (GRID_W, LANES)
    cq = lax.broadcasted_iota(jnp.int32, shape, 0)
    lane = lax.broadcasted_iota(jnp.int32, shape, 1)
    ck = lane & (GRID_W - 1)
    dc = jnp.clip(ck - cq, -(NA_COLS - 1), NA_COLS - 1) + (NA_COLS - 1)
    start_c = jnp.clip(cq - NA_COLS // 2, 0, GRID_W - NA_COLS)
    col_valid = (ck >= start_c) & (ck < start_c + NA_COLS)
    neg = jnp.full(shape, NEG_INF, F32)
    toeplitz = []
    for dr in range(n_dr):
        t = jnp.zeros(shape, F32)
        for m in range(n_dc):
            t = jnp.where(dc == m, rpb_ref[base + dr * n_dc + m], t)
        toeplitz.append(jnp.where(col_valid, t, neg))
    left = lane < GRID_W
    rows = DEC_SEQ // GRID_W
    for n in range(DEC_SEQ // ATT_BLK):
        for rq in range(NA_BLK_ROWS):
            r = NA_BLK_ROWS * n + rq
            start_r = min(max(r - NA_ROWS // 2, 0), rows - NA_ROWS)
            for jp in range(NA_WIN_ROWS // 2):
                pair = []
                for j in (2 * jp, 2 * jp + 1):
                    key_row = NA_WIN_START[n] + j
                    valid = start_r <= key_row < start_r + NA_ROWS
                    pair.append(toeplitz[key_row - r + NA_ROWS - 1] if valid else neg)
                tile = pair[0] if pair[0] is pair[1] else jnp.where(left, pair[0], pair[1])
                o_ref[n, rq * GRID_W:(rq + 1) * GRID_W, jp * LANES:(jp + 1) * LANES] = tile


def _dbias_call(rpb_d):
    nblk = DEC_SEQ // ATT_BLK
    return pl.pallas_call(
        _dbias_kernel,
        out_shape=jax.ShapeDtypeStruct((DEPTH, 4, nblk, ATT_BLK, NA_WIN), F32),
        grid=(DEPTH, 4),
        in_specs=[pl.BlockSpec(memory_space=pltpu.SMEM)],
        out_specs=pl.BlockSpec((None, None, nblk, ATT_BLK, NA_WIN), lambda l, h: (l, h, 0, 0, 0)),
        name="dbias",
    )(rpb_d.reshape(-1))


def _swap_halves(v, half):
    lane = lax.broadcasted_iota(jnp.int32, v.shape, 1)
    up = pltpu.roll(v, LANES - half, 1)
    dn = pltpu.roll(v, half, 1)
    return jnp.where((lane & (2 * half - 1)) < half, up, dn)


def _rope(v, cos, sin, half):
    return v * cos + _swap_halves(v, half) * sin


def _pre_kernel(*refs, prompt, first):
    if prompt:
        (x_ref, sh_ref, sc_ref, g1_ref, w_ref, gq_ref, gk_ref) = refs[:7]
        (qkv_ref, gates_ref, ka_ref, va_ref, kb_ref, vb_ref, kc_ref, vc_ref, kd_ref, vd_ref) = refs[-10:]
    else:
        (x_ref, sh_ref, sc_ref, g1_ref, w_ref, gq_ref, gk_ref,
         ca_ref, sa_ref, cb_ref, sb_ref, qkv_ref, gates_ref) = refs

    h = (_rms(x_ref[...], g1_ref[...], NORM_EPS) * (1.0 + sc_ref[...]) + sh_ref[...]).astype(BF16)
    ones = jnp.ones((h.shape[0], LANES), BF16)

    def rope_a(v):
        return v if prompt else _rope(v, ca_ref[...], sa_ref[...], 16)

    def rope_b(v):
        return v if prompt else _rope(v, cb_ref[...], sb_ref[...], 8)

    def put(col, v):
        qkv_ref[:, col:col + v.shape[1]] = v.astype(BF16)

    def put_values(col, v):
        for c in range(v.shape[1] // LANES):
            put(col + c * VAUG, v[:, c * LANES:(c + 1) * LANES])
            put(col + c * VAUG + LANES, ones)

    def put_heads(ref, v):
        for bi in range(v.shape[0] // SEQ):
            for pr in range(v.shape[1] // LANES):
                t = v[bi * SEQ:(bi + 1) * SEQ, pr * LANES:(pr + 1) * LANES].T
                for hh in range(2):
                    piece = t[hh * HEAD_DIM:(hh + 1) * HEAD_DIM]
                    if first:
                        ref[bi, 0, 2 * pr + hh] = piece
                        for later in range(1, DEPTH):
                            ref[bi, later, 2 * pr + hh] = jnp.zeros_like(piece)
                    else:
                        ref[bi, 2 * pr + hh] = piece

    scale = HEAD_DIM ** -0.5
    acc_a = _dot(h, w_ref[:, W_A:W_A + 512])
    acc = _dot(h, w_ref[:, W_B:W_B + 768])
    scale_b = B_HALF ** -0.5
    for c in range(2):
        put(B_Q + c * 128, rope_b(acc[:, c * 128:(c + 1) * 128]) * scale_b)
        put(B_K + c * 128, rope_b(acc[:, 256 + c * 128:256 + (c + 1) * 128]))
    put_values(B_V, acc[:, 512:768])
    if prompt:
        put_heads(kb_ref, acc[:, 256:512])
        put_heads(vb_ref, acc[:, 512:768])
    acc = _dot(h, w_ref[:, W_C:W_C + 512])
    q_e, q_o = _interleave_heads(acc[:, 0:128], acc[:, 128:256])
    put(C_QE, rope_a(q_e) * scale)
    put(C_QO, rope_a(q_o) * scale)
    put(C_K, rope_a(acc[:, 256:384]))
    put_values(C_V, acc[:, 384:512])
    if prompt:
        put_heads(kc_ref, acc[:, 256:384])
        put_heads(vc_ref, acc[:, 384:512])
    acc = _dot(h, w_ref[:, W_D:W_D + 768])
    put(D_Q, acc[:, 0:256] * scale)
    put(D_K, acc[:, 256:512])
    put_values(D_V, acc[:, 512:768])
    if prompt:
        put_heads(kd_ref, acc[:, 256:512])
        put_heads(vd_ref, acc[:, 512:768])
    gq = gq_ref[...]
    q_e, q_o = _interleave_heads(acc_a[:, 0:128], acc_a[:, 128:256])
    put(A_QE, rope_a(_head_rms(q_e, gq, NORM_EPS)) * scale)
    put(A_QO, rope_a(_head_rms(q_o, gq, NORM_EPS)) * scale)
    k_a = _head_rms(acc_a[:, 256:384], gk_ref[...], NORM_EPS)
    put(A_K, rope_a(k_a))
    put_values(A_V, acc_a[:, 384:512])
    if prompt:
        put_heads(ka_ref, k_a)
        put_heads(va_ref, acc_a[:, 384:512])
    for j in range(GATE_W // 512):
        g_half = _dot(h, w_ref[:, W_GATES + j * 512:W_GATES + (j + 1) * 512])
        gates_ref[:, j * 512:(j + 1) * 512] = (jnp.tanh(g_half) + 1.0).astype(BF16)


def _pre_call(l, x, mod, g1, w_in, gq, gk, rope_tabs, kv_prev, *, prompt):
    rows = x.shape[0]
    tm = 512
    tq = tm
    bpb = tm // SEQ
    aliases = {}
    if prompt:
        mod_row = lambda i: CTX_ROW
    else:
        mod_row = lambda i: i // (DEC_SEQ // tm)
    in_specs = [
        pl.BlockSpec((tm, D_MODEL), lambda i: (i, 0)),
        pl.BlockSpec((None, None, 1, D_MODEL), lambda i: (l, mod_row(i), 0, 0)),
        pl.BlockSpec((None, None, 1, D_MODEL), lambda i: (l, mod_row(i), 0, 1)),
        pl.BlockSpec((None, 1, D_MODEL), lambda i: (l, 0, 0)),
        pl.BlockSpec((None, D_MODEL, IN_COLS), lambda i: (l, 0, 0), pipeline_mode=pl.Buffered(1)),
        pl.BlockSpec((None, 1, LANES), lambda i: (l, 0, 0)),
        pl.BlockSpec((None, 1, LANES), lambda i: (l, 0, 0)),
    ]
    args = [x, mod, mod, g1, w_in, gq, gk]
    out_shape = [jax.ShapeDtypeStruct((rows, QKV_W), BF16), jax.ShapeDtypeStruct((rows, GATE_W), BF16)]
    out_specs = [pl.BlockSpec((tm, QKV_W), lambda i: (i, 0)), pl.BlockSpec((tm, GATE_W), lambda i: (i, 0))]
    if prompt:
        for j, nh in enumerate((2, 2, 4, 4, 2, 2, 4, 4)):
            out_shape.append(jax.ShapeDtypeStruct((BATCH, DEPTH, nh, HEAD_DIM, SEQ), F32))
            if kv_prev is None:
                out_specs.append(pl.BlockSpec((bpb, DEPTH, nh, HEAD_DIM, SEQ), lambda i: (i, 0, 0, 0, 0)))
            else:
                out_specs.append(pl.BlockSpec((bpb, None, nh, HEAD_DIM, SEQ), lambda i: (i, l, 0, 0, 0)))
                aliases[len(args)] = 2 + j
                in_specs.append(pl.BlockSpec(memory_space=pl.ANY))
                args.append(kv_prev[j])
    else:
        nt = DEC_SEQ // tq
        for t in rope_tabs:
            in_specs.append(pl.BlockSpec((tq, LANES), lambda i: (i % nt, 0)))
            args.append(t)
    return pl.pallas_call(
        functools.partial(_pre_kernel, prompt=prompt, first=kv_prev is None),
        out_shape=out_shape,
        grid=(rows // tm,),
        in_specs=in_specs,
        out_specs=out_specs,
        input_output_aliases=aliases,
        compiler_params=pltpu.CompilerParams(vmem_limit_bytes=VMEM_LIMIT),
        name="pre_prompt" if prompt else "pre_latent",
    )(*args)


class _Transposed:
    def __init__(self, a):
        self.a = a


def _scores(qs, keys, biases):
    out = []
    for k, b in zip(keys, biases):
        s = _dot(qs, k.a) if isinstance(k, _Transposed) else _dot_nt(qs, k)
        out.append(s if b is None else s + b)
    return out


def _row_max(s_list, sink):
    m = None
    for s in s_list:
        mi = jnp.max(s, axis=-1, keepdims=True)
        m = mi if m is None else jnp.maximum(m, mi)
    return m if sink is None else jnp.maximum(m, sink)


def _softmax_pv(s_list, vaugs, sink=None):
    m = _row_max(s_list, sink)
    r = None
    for s, v in zip(s_list, vaugs):
        p = jnp.exp((s - m).astype(BF16))
        ri = _dot_nt(p, v.a) if isinstance(v, _Transposed) else _dot(p, v)
        r = ri if r is None else r + ri
    den = r[:, LANES:]
    if sink is not None:
        den = den + jnp.exp(sink - m)
    return r[:, :LANES] / den


def _gqa(qe, qo, keys, vaugs, biases, sinks):
    mq = qe.shape[0]
    oe = jnp.zeros((mq, LANES), F32)
    oo = jnp.zeros((mq, LANES), F32)
    for g in range(2):
        lo, hi = HEAD_DIM * g, HEAD_DIM * (g + 1)
        qs = jnp.concatenate([_mask_q(qe, lo, hi), _mask_q(qo, lo, hi)], axis=0)
        sink = None
        if sinks is not None:
            row = lax.broadcasted_iota(jnp.int32, (2 * mq, 1), 0)
            sink = jnp.where(row < mq, sinks[2 * g], sinks[2 * g + 1])
        r = _softmax_pv(_scores(qs, keys, biases), vaugs, sink)
        msk = _lane_mask((mq, LANES), lo, hi)
        oe = jnp.where(msk, r[:mq], oe)
        oo = jnp.where(msk, r[mq:], oo)
    return _interleave_heads(oe, oo)


def _diff(q, keys, vaug_fn, lam):
    mq = q.shape[0]
    out = [jnp.zeros((mq, LANES), F32), jnp.zeros((mq, LANES), F32)]
    for hd in range(4):
        lo = HEAD_DIM * hd
        qs = jnp.concatenate([_mask_q(q, lo, lo + B_HALF), _mask_q(q, lo + B_HALF, lo + HEAD_DIM)], axis=0)
        o = _softmax_pv(_scores(qs, keys, [None] * len(keys)), vaug_fn(hd // 2))
        o = o[:mq] - lam * o[mq:]
        plo = HEAD_DIM * (hd % 2)
        out[hd // 2] = jnp.where(_lane_mask(o.shape, plo, plo + HEAD_DIM), o, out[hd // 2])
    return out


def _mha(q, keys, vaug_fn, bias_fn):
    mq = q.shape[0]
    out = [jnp.zeros((mq, LANES), F32), jnp.zeros((mq, LANES), F32)]
    for hd in range(4):
        lo = HEAD_DIM * hd
        qs = _mask_q(q, lo, lo + HEAD_DIM)
        o = _softmax_pv(_scores(qs, keys, bias_fn(hd)), vaug_fn(hd // 2))
        plo = HEAD_DIM * (hd % 2)
        out[hd // 2] = jnp.where(_lane_mask(o.shape, plo, plo + HEAD_DIM), o, out[hd // 2])
    return out


def _lambda(lam_ref, lam_init):
    lp = lam_ref[...]
    a = jnp.sum(lp[0:1, :] * lp[1:2, :], axis=-1, keepdims=True)
    b = jnp.sum(lp[2:3, :] * lp[3:4, :], axis=-1, keepdims=True)
    return jnp.exp(a) - jnp.exp(b) + lam_init


def _store_branches(br_ref, oa, ob, oc, od, bd_ref, gsub_ref, lam_init):
    bd = bd_ref[...]
    ob = [_head_rms_mxu(o, bd, gsub_ref[...], SUBLN_EPS) * (1.0 - lam_init) for o in ob]
    for j, o in enumerate((*oa, *ob, *oc, *od)):
        br_ref[:, j * LANES:(j + 1) * LANES] = o.astype(BF16)


def _attn_prompt_kernel(qkv_ref, sink_ref, lam_ref, gsub_ref, bd_ref, br_ref, *, l, lam_init):
    sinks = [sink_ref[l, i] for i in range(4)]
    lam = _lambda(lam_ref, lam_init)
    for bi in range(qkv_ref.shape[0] // SEQ):
        r0 = bi * SEQ

        def cols(c, w):
            return qkv_ref[r0:r0 + SEQ, c:c + w]

        def qcols(c, w):
            return cols(c, w).astype(F32)

        oa = _gqa(qcols(A_QE, 128), qcols(A_QO, 128), [cols(A_K, 128)], [cols(A_V, VAUG)], [None], None)
        ob = _diff(qcols(B_Q, 256), [cols(B_K, 256)], lambda pr: [cols(B_V + pr * VAUG, VAUG)], lam)
        oc = _gqa(qcols(C_QE, 128), qcols(C_QO, 128), [cols(C_K, 128)], [cols(C_V, VAUG)], [None], sinks)
        od = _mha(qcols(D_Q, 256), [cols(D_K, 256)], lambda pr: [cols(D_V + pr * VAUG, VAUG)], lambda hd: [None])
        _store_branches(br_ref.at[r0:r0 + SEQ], oa, ob, oc, od, bd_ref, gsub_ref, lam_init)


def _attn_prompt_call(l, lam_init, qkv, sink_c, lam_b, gsub, bd128):
    rows = qkv.shape[0]
    tm = 4 * SEQ
    return pl.pallas_call(
        functools.partial(_attn_prompt_kernel, l=l, lam_init=lam_init),
        out_shape=jax.ShapeDtypeStruct((rows, D_MODEL), BF16),
        grid=(rows // tm,),
        in_specs=[
            pl.BlockSpec((tm, QKV_W), lambda b: (b, 0)),
            pl.BlockSpec(memory_space=pltpu.SMEM),
            pl.BlockSpec((None, 4, B_HALF), lambda b: (l, 0, 0)),
            pl.BlockSpec((None, 1, LANES), lambda b: (l, 0, 0)),
            pl.BlockSpec((LANES, LANES), lambda b: (0, 0)),
        ],
        out_specs=pl.BlockSpec((tm, D_MODEL), lambda b: (b, 0)),
        compiler_params=pltpu.CompilerParams(vmem_limit_bytes=VMEM_LIMIT),
        name="attn_prompt",
    )(qkv, sink_c, lam_b, gsub, bd128)


def _attn_latent_kernel(kv_ref, xak_ref, xav_ref, xbk_ref, xbv_ref, xck_ref, xcv_ref, xdk_ref, xdv_ref,
                        bias_ref, sink_ref, lam_ref, gsub_ref, bd_ref, br_ref, *, l, lam_init):
    sinks = [sink_ref[l, i] for i in range(4)]
    lam = _lambda(lam_ref, lam_init)
    nsub = br_ref.shape[0] // ATT_BLK
    for sub in range(nsub):
        n = pl.program_id(1) * nsub + sub
        q0 = pl.multiple_of(n * ATT_BLK, ATT_BLK)

        def q(c, w):
            return kv_ref[pl.ds(q0, ATT_BLK), c:c + w].astype(F32)

        def lat(c, w):
            return kv_ref[:, c:c + w]

        def ctx_k(ref):
            return _Transposed(ref[...].reshape(ref.shape[0] * HEAD_DIM, PAST_LEN).astype(BF16))

        def ctx_v(ref, pr):
            vt = ref[2 * pr:2 * pr + 2].reshape(LANES, PAST_LEN).astype(BF16)
            return _Transposed(jnp.concatenate([vt, jnp.ones((LANES, PAST_LEN), BF16)], axis=0))

        oa = _gqa(q(A_QE, 128), q(A_QO, 128), [ctx_k(xak_ref), lat(A_K, 128)], [ctx_v(xav_ref, 0), lat(A_V, VAUG)],
                  [None, None], None)
        ob = _diff(q(B_Q, 256), [ctx_k(xbk_ref), lat(B_K, 256)],
                   lambda pr: [ctx_v(xbv_ref, pr), lat(B_V + pr * VAUG, VAUG)], lam)
        start_c = pl.multiple_of(jnp.clip(q0 - C_WINDOW, 0, DEC_SEQ - C_WIN), C_WINDOW)
        rowq = lax.broadcasted_iota(jnp.int32, (2 * ATT_BLK, C_WIN), 0) & (ATT_BLK - 1)
        colk = lax.broadcasted_iota(jnp.int32, (2 * ATT_BLK, C_WIN), 1)
        band = jnp.where(jnp.abs(rowq - colk + (q0 - start_c)) <= C_WINDOW, 0.0, NEG_INF)
        oc = _gqa(q(C_QE, 128), q(C_QO, 128),
                  [ctx_k(xck_ref), kv_ref[pl.ds(start_c, C_WIN), C_K:C_K + 128]],
                  [ctx_v(xcv_ref, 0), kv_ref[pl.ds(start_c, C_WIN), C_V:C_V + VAUG]],
                  [None, band], sinks)
        start_d = pl.multiple_of(jnp.where(n >= 2, NA_WIN_START[2] * GRID_W, 0), ATT_BLK)
        kwin = kv_ref[pl.ds(start_d, NA_WIN), D_K:D_K + 256]
        od = _mha(q(D_Q, 256), [ctx_k(xdk_ref), kwin],
                  lambda pr: [ctx_v(xdv_ref, pr),
                              kv_ref[pl.ds(start_d, NA_WIN), D_V + pr * VAUG:D_V + (pr + 1) * VAUG]],
                  lambda hd: [None, bias_ref[hd, n]])
        _store_branches(br_ref.at[sub * ATT_BLK:(sub + 1) * ATT_BLK], oa, ob, oc, od, bd_ref, gsub_ref, lam_init)


def _attn_latent_call(l, lam_init, qkv, caches, dbias, sink_c, lam_b, gsub, bd128):
    rows = qkv.shape[0]
    nblk = DEC_SEQ // ATT_BLK
    nsub = 1
    cache_specs = [pl.BlockSpec((None, None, t.shape[2], HEAD_DIM, PAST_LEN), lambda b, n: (b, l, 0, 0, 0))
                   for t in caches]
    return pl.pallas_call(
        functools.partial(_attn_latent_kernel, l=l, lam_init=lam_init),
        out_shape=jax.ShapeDtypeStruct((rows, D_MODEL), BF16),
        grid=(DEC_BATCH, nblk // nsub),
        in_specs=[
            pl.BlockSpec((DEC_SEQ, QKV_W), lambda b, n: (b, 0)),
            *cache_specs,
            pl.BlockSpec((None, 4, nblk, ATT_BLK, NA_WIN), lambda b, n: (l, 0, 0, 0, 0), pipeline_mode=pl.Buffered(1)),
            pl.BlockSpec(memory_space=pltpu.SMEM),
            pl.BlockSpec((None, 4, B_HALF), lambda b, n: (l, 0, 0)),
            pl.BlockSpec((None, 1, LANES), lambda b, n: (l, 0, 0)),
            pl.BlockSpec((LANES, LANES), lambda b, n: (0, 0)),
        ],
        out_specs=pl.BlockSpec((nsub * ATT_BLK, D_MODEL), lambda b, n: (b * (nblk // nsub) + n, 0)),
        compiler_params=pltpu.CompilerParams(vmem_limit_bytes=VMEM_LIMIT),
        name="attn_latent",
    )(qkv, *caches, dbias, sink_c, lam_b, gsub, bd128)


MXU_TILE = 256
FFN_CHUNKS = ((0, 6 * MXU_TILE), (6 * MXU_TILE, D_FF))


def _post_kernel(x_ref, br_ref, gates_ref, gt1_ref, sh2_ref, sc2_ref, gt2_ref, g2_ref,
                 wb_ref, wo_ref, wfi_ref, wfo_ref, gf_ref, o_ref, *, final):
    merged = None
    for k in range(4):
        proj = _dot(br_ref[:, k * 256:(k + 1) * 256], wb_ref[k])
        t = gates_ref[:, k * D_MODEL:(k + 1) * D_MODEL].astype(F32) * proj
        merged = t if merged is None else merged + t
    x1 = x_ref[...] + gt1_ref[...] * _dot(merged.astype(BF16), wo_ref[...])
    h2 = (_rms(x1, g2_ref[...], NORM_EPS) * (1.0 + sc2_ref[...]) + sh2_ref[...]).astype(BF16)
    acc = None
    for c0, c1 in FFN_CHUNKS:
        a = _dot(h2, wfi_ref[:, c0:c1])
        u = _dot(h2, wfi_ref[:, D_FF + c0:D_FF + c1])
        g = ((a * _sigmoid(a)) * u).astype(BF16)
        t = _dot(g, wfo_ref[c0:c1, :])
        acc = t if acc is None else acc + t
    xo = x1 + gt2_ref[...] * acc
    if final:
        xo = _rms(xo, gf_ref[...], NORM_EPS)
    o_ref[...] = xo


def _post_call(l, x, br, gates, mod, g2, wb, wo, wfi, wfo, gf, *, prompt, final):
    rows = x.shape[0]
    tm = 512
    if prompt:
        mod_row = lambda i: CTX_ROW
    else:
        mod_row = lambda i: i // (DEC_SEQ // tm)

    def mod_spec(chunk):
        return pl.BlockSpec((None, None, 1, D_MODEL), lambda i: (l, mod_row(i), 0, chunk))

    def resident(shape):
        nd = len(shape)
        return pl.BlockSpec((None,) + shape, lambda i: (l,) + (0,) * nd, pipeline_mode=pl.Buffered(1))

    return pl.pallas_call(
        functools.partial(_post_kernel, final=final),
        out_shape=jax.ShapeDtypeStruct((rows, D_MODEL), F32),
        grid=(rows // tm,),
        in_specs=[
            pl.BlockSpec((tm, D_MODEL), lambda i: (i, 0)),
            pl.BlockSpec((tm, D_MODEL), lambda i: (i, 0)),
            pl.BlockSpec((tm, GATE_W), lambda i: (i, 0)),
            mod_spec(2), mod_spec(3), mod_spec(4), mod_spec(5),
            pl.BlockSpec((None, 1, D_MODEL), lambda i: (l, 0, 0)),
            resident((4, 256, D_MODEL)),
            resident((D_MODEL, D_MODEL)),
            resident((D_MODEL, 2 * D_FF)),
            resident((D_FF, D_MODEL)),
            pl.BlockSpec((1, D_MODEL), lambda i: (0, 0)),
        ],
        out_specs=pl.BlockSpec((tm, D_MODEL), lambda i: (i, 0)),
        compiler_params=pltpu.CompilerParams(vmem_limit_bytes=VMEM_LIMIT),
        name="post_prompt" if prompt else "post_latent",
    )(x, br, gates, mod, mod, mod, mod, g2, wb, wo, wfi, wfo, gf)


def _rope_tables():
    t = jnp.arange(DEC_SEQ)
    row = (t // GRID_W).astype(F32)[:, None]
    col = (t % GRID_W).astype(F32)[:, None]
    tabs = []
    for d in (HEAD_DIM, B_HALF):
        quarter = d // 4
        inv = jnp.power(jnp.float32(ROPE_THETA), -jnp.arange(quarter, dtype=F32) / quarter)
        ar, ac = row * inv, col * inv
        cos = jnp.concatenate([jnp.cos(ar), jnp.cos(ar), jnp.cos(ac), jnp.cos(ac)], axis=-1)
        sin = jnp.concatenate([-jnp.sin(ar), jnp.sin(ar), -jnp.sin(ac), jnp.sin(ac)], axis=-1)
        reps = LANES // d
        tabs += [jnp.tile(cos, (1, reps)), jnp.tile(sin, (1, reps))]
    return tabs


def kernel(x_prompt, x_sample, cache_a_k, cache_a_v, cache_b_k, cache_b_v, cache_c_k, cache_c_v, cache_d_k, cache_d_v, c, c_ctx, w_ada, b_ada, g_norm1, w_in, g_q_a, g_k_a, lam_b, g_subln_b, sink_c, rpb_d, w_branch, w_out, g_norm2, w_ffn_in, w_ffn_out, g_final):
    gate_half = jnp.where(jnp.arange(IN_COLS) >= W_GATES, 0.5, 1.0).astype(F32)
    w_in_b = (w_in * gate_half).astype(BF16)
    wb = w_branch.astype(BF16)
    wo = (w_out * 0.5).astype(BF16)
    wfi = w_ffn_in.astype(BF16)
    wfo = w_ffn_out.astype(BF16)
    gq = jnp.tile(g_q_a, (1, 2)).reshape(DEPTH, 1, LANES)
    gk = jnp.tile(g_k_a, (1, 2)).reshape(DEPTH, 1, LANES)
    gsub = jnp.tile(g_subln_b, (1, 2)).reshape(DEPTH, 1, LANES)
    g1 = g_norm1.reshape(DEPTH, 1, D_MODEL)
    g2 = g_norm2.reshape(DEPTH, 1, D_MODEL)
    gf = g_final.reshape(1, D_MODEL)
    head_of_lane = jnp.arange(LANES) // HEAD_DIM
    bd128 = ((head_of_lane[:, None] == head_of_lane[None, :]).astype(F32) / HEAD_DIM).astype(BF16)
    rope_tabs = _rope_tables()
    caches = tuple(jnp.swapaxes(t, 3, 4) for t in (cache_a_k, cache_a_v, cache_b_k, cache_b_v,
                                                   cache_c_k, cache_c_v, cache_d_k, cache_d_v))

    cond = jnp.concatenate([c, c_ctx[None, :], jnp.zeros((MOD_ROWS - DEC_BATCH - 1, D_MODEL), F32)], axis=0)
    mod = _ada_call(cond, w_ada, b_ada).reshape(DEPTH, MOD_ROWS, 1, 6 * D_MODEL)
    dbias = _dbias_call(rpb_d)

    xp = x_prompt.reshape(BATCH * SEQ, D_MODEL)
    xs = x_sample.reshape(DEC_BATCH * DEC_SEQ, D_MODEL)
    new_kv = None
    for l in range(DEPTH):
        lam_init = 0.8 - 0.6 * math.exp(-0.3 * l)
        final = l == DEPTH - 1
        outs = _pre_call(l, xp, mod, g1, w_in_b, gq, gk, None, new_kv, prompt=True)
        qkv_p, gates_p, new_kv = outs[0], outs[1], outs[2:]
        br_p = _attn_prompt_call(l, lam_init, qkv_p, sink_c, lam_b, gsub, bd128)
        xp = _post_call(l, xp, br_p, gates_p, mod, g2, wb, wo, wfi, wfo, gf, prompt=True, final=final)

        qkv_s, gates_s = _pre_call(l, xs, mod, g1, w_in_b, gq, gk, rope_tabs, None, prompt=False)
        br_s = _attn_latent_call(l, lam_init, qkv_s, caches, dbias, sink_c, lam_b, gsub, bd128)
        xs = _post_call(l, xs, br_s, gates_s, mod, g2, wb, wo, wfi, wfo, gf, prompt=False, final=final)

    y_prompt = xp.reshape(BATCH, SEQ, D_MODEL)
    y_sample = xs.reshape(DEC_BATCH, DEC_SEQ, D_MODEL)
    return (y_prompt, y_sample, *(jnp.swapaxes(t, 3, 4) for t in new_kv))
```

```python
import functools
import math

import jax
import jax.numpy as jnp
from jax import lax
from jax.experimental import pallas as pl
from jax.experimental.pallas import tpu as pltpu

F32 = jnp.float32
BF16 = jnp.bfloat16

D_MODEL = 1024
BATCH = 32
SEQ = 256
DEPTH = 2
DEC_BATCH = 8
DEC_SEQ = 1024
PAST_LEN = 256
GRID_W = 64
HEAD_DIM = 64
B_HALF = HEAD_DIM // 2
C_WINDOW = 128
NA_ROWS = 8
NA_COLS = 16
D_FF = 2816
ROPE_THETA = 10000.0
NORM_EPS = 1e-6
SUBLN_EPS = 1e-5
NEG_INF = -1e30

GATE_W = 4 * D_MODEL
W_A, W_B, W_C, W_D, W_GATES = 0, 512, 1280, 1792, 2560
IN_COLS = W_GATES + GATE_W
MOD_ROWS = 16
CTX_ROW = DEC_BATCH
LANES = 128
ATT_BLK = 256
NA_BLK_ROWS = ATT_BLK // GRID_W
NA_WIN_ROWS = 12
NA_WIN = NA_WIN_ROWS * GRID_W
NA_WIN_START = (0, 0, 4, 4)
C_WIN = ATT_BLK + 2 * C_WINDOW
VMEM_LIMIT = 56 * 1024 * 1024

A_QE, A_QO, A_K, A_V = 0, 128, 256, 384
B_Q, B_K, B_V = 640, 896, 1152
C_QE, C_QO, C_K, C_V = 1664, 1792, 1920, 2048
D_Q, D_K, D_V = 2304, 2560, 2816
QKV_W = 3328
VAUG = 2 * LANES


def _dot(a, b):
    return jnp.dot(a, b, preferred_element_type=F32)


def _dot_nt(a, b):
    return lax.dot_general(a, b, (((1,), (1,)), ((), ())), preferred_element_type=F32)


def _sigmoid(x):
    return 0.5 * jnp.tanh(0.5 * x) + 0.5


def _rms(x, g, eps):
    ms = jnp.mean(x * x, axis=-1, keepdims=True)
    return x * lax.rsqrt(ms + eps) * g


def _head_rms_mxu(v, bd, g, eps):
    v2 = v * v
    hi = v2.astype(BF16)
    lo = (v2 - hi.astype(F32)).astype(BF16)
    ms = _dot(hi, bd) + _dot(lo, bd)
    return v * lax.rsqrt(ms + eps) * g


def _head_rms(v, g, eps):
    v2 = v * v
    left = _lane_mask(v.shape, 0, HEAD_DIM)
    s_left = jnp.sum(jnp.where(left, v2, 0.0), axis=-1, keepdims=True)
    s_right = jnp.sum(jnp.where(left, 0.0, v2), axis=-1, keepdims=True)
    ms = jnp.where(left, s_left, s_right) * (1.0 / HEAD_DIM)
    return v * lax.rsqrt(ms + eps) * g


def _lane_mask(shape, lo, hi):
    lane = lax.broadcasted_iota(jnp.int32, shape, 1)
    return (lane >= lo) & (lane < hi)


def _mask_q(qf, lo, hi):
    return jnp.where(_lane_mask(qf.shape, lo, hi), qf, 0.0).astype(BF16)


def _interleave_heads(lo, hi):
    left = _lane_mask(lo.shape, 0, HEAD_DIM)
    return (jnp.where(left, lo, pltpu.roll(hi, HEAD_DIM, 1)),
            jnp.where(left, pltpu.roll(lo, HEAD_DIM, 1), hi))


def _ada_kernel(cond_ref, w_ref, b_ref, o_ref):
    c = cond_ref[...]
    s = (c * _sigmoid(c)).astype(BF16)
    o_ref[...] = _dot(s, w_ref[...].astype(BF16)) + b_ref[...]


def _ada_call(cond, w_ada, b_ada):
    tn = 1536
    return pl.pallas_call(
        _ada_kernel,
        out_shape=jax.ShapeDtypeStruct((DEPTH, MOD_ROWS, 6 * D_MODEL), F32),
        grid=(DEPTH, 6 * D_MODEL // tn),
        in_specs=[
            pl.BlockSpec((MOD_ROWS, D_MODEL), lambda l, j: (0, 0)),
            pl.BlockSpec((None, D_MODEL, tn), lambda l, j: (l, 0, j)),
            pl.BlockSpec((None, 1, tn), lambda l, j: (l, 0, j)),
        ],
        out_specs=pl.BlockSpec((None, MOD_ROWS, tn), lambda l, j: (l, 0, j)),
        compiler_params=pltpu.CompilerParams(vmem_limit_bytes=VMEM_LIMIT),
        name="ada",
    )(cond, w_ada, b_ada.reshape(DEPTH, 1, 6 * D_MODEL))


def _dbias_kernel(rpb_ref, o_ref):
    n_dr, n_dc = 2 * NA_ROWS - 1, 2 * NA_COLS - 1
    base = (pl.program_id(0) * 4 + pl.program_id(1)) * (n_dr * n_dc)
    shape = (GRID_W, LANES)
    cq = lax.broadcasted_iota(jnp.int32, shape, 0)
    lane = lax.broadcasted_iota(jnp.int32, shape, 1)
    ck = lane & (GRID_W - 1)
    dc = jnp.clip(ck - cq, -(NA_COLS - 1), NA_COLS - 1) + (NA_COLS - 1)
    start_c = jnp.clip(cq - NA_COLS // 2, 0, GRID_W - NA_COLS)
    col_valid = (ck >= start_c) & (ck < start_c + NA_COLS)
    neg = jnp.full(shape, NEG_INF, F32)
    toeplitz = []
    for dr in range(n_dr):
        t = jnp.zeros(shape, F32)
        for m in range(n_dc):
            t = jnp.where(dc == m, rpb_ref[base + dr * n_dc + m], t)
        toeplitz.append(jnp.where(col_valid, t, neg))
    left = lane < GRID_W
    rows = DEC_SEQ // GRID_W
    for n in range(DEC_SEQ // ATT_BLK):
        for rq in range(NA_BLK_ROWS):
            r = NA_BLK_ROWS * n + rq
            start_r = min(max(r - NA_ROWS // 2, 0), rows - NA_ROWS)
            for jp in range(NA_WIN_ROWS // 2):
                pair = []
                for j in (2 * jp, 2 * jp + 1):
                    key_row = NA_WIN_START[n] + j
                    valid = start_r <= key_row < start_r + NA_ROWS
                    pair.append(toeplitz[key_row - r + NA_ROWS - 1] if valid else neg)
                tile = pair[0] if pair[0] is pair[1] else jnp.where(left, pair[0], pair[1])
                o_ref[n, rq * GRID_W:(rq + 1) * GRID_W, jp * LANES:(jp + 1) * LANES] = tile


def _dbias_call(rpb_d):
    nblk = DEC_SEQ // ATT_BLK
    return pl.pallas_call(
        _dbias_kernel,
        out_shape=jax.ShapeDtypeStruct((DEPTH, 4, nblk, ATT_BLK, NA_WIN), F32),
        grid=(DEPTH, 4),
        in_specs=[pl.BlockSpec(memory_space=pltpu.SMEM)],
        out_specs=pl.BlockSpec((None, None, nblk, ATT_BLK, NA_WIN), lambda l, h: (l, h, 0, 0, 0)),
        name="dbias",
    )(rpb_d.reshape(-1))


def _swap_halves(v, half):
    lane = lax.broadcasted_iota(jnp.int32, v.shape, 1)
    up = pltpu.roll(v, LANES - half, 1)
    dn = pltpu.roll(v, half, 1)
    return jnp.where((lane & (2 * half - 1)) < half, up, dn)


def _rope(v, cos, sin, half):
    return v * cos + _swap_halves(v, half) * sin


def _pre_kernel(*refs, prompt, first):
    if prompt:
        (x_ref, sh_ref, sc_ref, g1_ref, w_ref, gq_ref, gk_ref) = refs[:7]
        (qkv_ref, gates_ref, ka_ref, va_ref, kb_ref, vb_ref, kc_ref, vc_ref, kd_ref, vd_ref) = refs[-10:]
    else:
        (x_ref, sh_ref, sc_ref, g1_ref, w_ref, gq_ref, gk_ref,
         ca_ref, sa_ref, cb_ref, sb_ref, qkv_ref, gates_ref) = refs

    h = (_rms(x_ref[...], g1_ref[...], NORM_EPS) * (1.0 + sc_ref[...]) + sh_ref[...]).astype(BF16)
    ones = jnp.ones((h.shape[0], LANES), BF16)

    def rope_a(v):
        return v if prompt else _rope(v, ca_ref[...], sa_ref[...], 16)

    def rope_b(v):
        return v if prompt else _rope(v, cb_ref[...], sb_ref[...], 8)

    def put(col, v):
        qkv_ref[:, col:col + v.shape[1]] = v.astype(BF16)

    def put_values(col, v):
        for c in range(v.shape[1] // LANES):
            put(col + c * VAUG, v[:, c * LANES:(c + 1) * LANES])
            put(col + c * VAUG + LANES, ones)

    def put_heads(ref, v):
        for bi in range(v.shape[0] // SEQ):
            for pr in range(v.shape[1] // LANES):
                t = v[bi * SEQ:(bi + 1) * SEQ, pr * LANES:(pr + 1) * LANES].T
                for hh in range(2):
                    piece = t[hh * HEAD_DIM:(hh + 1) * HEAD_DIM]
                    if first:
                        ref[bi, 0, 2 * pr + hh] = piece
                        for later in range(1, DEPTH):
                            ref[bi, later, 2 * pr + hh] = jnp.zeros_like(piece)
                    else:
                        ref[bi, 2 * pr + hh] = piece

    scale = HEAD_DIM ** -0.5
    acc_a = _dot(h, w_ref[:, W_A:W_A + 512])
    acc = _dot(h, w_ref[:, W_B:W_B + 768])
    scale_b = B_HALF ** -0.5
    for c in range(2):
        put(B_Q + c * 128, rope_b(acc[:, c * 128:(c + 1) * 128]) * scale_b)
        put(B_K + c * 128, rope_b(acc[:, 256 + c * 128:256 + (c + 1) * 128]))
    put_values(B_V, acc[:, 512:768])
    if prompt:
        put_heads(kb_ref, acc[:, 256:512])
        put_heads(vb_ref, acc[:, 512:768])
    acc = _dot(h, w_ref[:, W_C:W_C + 512])
    q_e, q_o = _interleave_heads(acc[:, 0:128], acc[:, 128:256])
    put(C_QE, rope_a(q_e) * scale)
    put(C_QO, rope_a(q_o) * scale)
    put(C_K, rope_a(acc[:, 256:384]))
    put_values(C_V, acc[:, 384:512])
    if prompt:
        put_heads(kc_ref, acc[:, 256:384])
        put_heads(vc_ref, acc[:, 384:512])
    acc = _dot(h, w_ref[:, W_D:W_D + 768])
    put(D_Q, acc[:, 0:256] * scale)
    put(D_K, acc[:, 256:512])
    put_values(D_V, acc[:, 512:768])
    if prompt:
        put_heads(kd_ref, acc[:, 256:512])
        put_heads(vd_ref, acc[:, 512:768])
    gq = gq_ref[...]
    q_e, q_o = _interleave_heads(acc_a[:, 0:128], acc_a[:, 128:256])
    put(A_QE, rope_a(_head_rms(q_e, gq, NORM_EPS)) * scale)
    put(A_QO, rope_a(_head_rms(q_o, gq, NORM_EPS)) * scale)
    k_a = _head_rms(acc_a[:, 256:384], gk_ref[...], NORM_EPS)
    put(A_K, rope_a(k_a))
    put_values(A_V, acc_a[:, 384:512])
    if prompt:
        put_heads(ka_ref, k_a)
        put_heads(va_ref, acc_a[:, 384:512])
    for j in range(GATE_W // 512):
        g_half = _dot(h, w_ref[:, W_GATES + j * 512:W_GATES + (j + 1) * 512])
        gates_ref[:, j * 512:(j + 1) * 512] = (jnp.tanh(g_half) + 1.0).astype(BF16)


def _pre_call(l, x, mod, g1, w_in, gq, gk, rope_tabs, kv_prev, *, prompt):
    rows = x.shape[0]
    tm = 512
    tq = tm
    bpb = tm // SEQ
    aliases = {}
    if prompt:
        mod_row = lambda i: CTX_ROW
    else:
        mod_row = lambda i: i // (DEC_SEQ // tm)
    in_specs = [
        pl.BlockSpec((tm, D_MODEL), lambda i: (i, 0)),
        pl.BlockSpec((None, None, 1, D_MODEL), lambda i: (l, mod_row(i), 0, 0)),
        pl.BlockSpec((None, None, 1, D_MODEL), lambda i: (l, mod_row(i), 0, 1)),
        pl.BlockSpec((None, 1, D_MODEL), lambda i: (l, 0, 0)),
        pl.BlockSpec((None, D_MODEL, IN_COLS), lambda i: (l, 0, 0), pipeline_mode=pl.Buffered(1)),
        pl.BlockSpec((None, 1, LANES), lambda i: (l, 0, 0)),
        pl.BlockSpec((None, 1, LANES), lambda i: (l, 0, 0)),
    ]
    args = [x, mod, mod, g1, w_in, gq, gk]
    out_shape = [jax.ShapeDtypeStruct((rows, QKV_W), BF16), jax.ShapeDtypeStruct((rows, GATE_W), BF16)]
    out_specs = [pl.BlockSpec((tm, QKV_W), lambda i: (i, 0)), pl.BlockSpec((tm, GATE_W), lambda i: (i, 0))]
    if prompt:
        for j, nh in enumerate((2, 2, 4, 4, 2, 2, 4, 4)):
            out_shape.append(jax.ShapeDtypeStruct((BATCH, DEPTH, nh, HEAD_DIM, SEQ), F32))
            if kv_prev is None:
                out_specs.append(pl.BlockSpec((bpb, DEPTH, nh, HEAD_DIM, SEQ), lambda i: (i, 0, 0, 0, 0)))
            else:
                out_specs.append(pl.BlockSpec((bpb, None, nh, HEAD_DIM, SEQ), lambda i: (i, l, 0, 0, 0)))
                aliases[len(args)] = 2 + j
                in_specs.append(pl.BlockSpec(memory_space=pl.ANY))
                args.append(kv_prev[j])
    else:
        nt = DEC_SEQ // tq
        for t in rope_tabs:
            in_specs.append(pl.BlockSpec((tq, LANES), lambda i: (i % nt, 0)))
            args.append(t)
    return pl.pallas_call(
        functools.partial(_pre_kernel, prompt=prompt, first=kv_prev is None),
        out_shape=out_shape,
        grid=(rows // tm,),
        in_specs=in_specs,
        out_specs=out_specs,
        input_output_aliases=aliases,
        compiler_params=pltpu.CompilerParams(vmem_limit_bytes=VMEM_LIMIT),
        name="pre_prompt" if prompt else "pre_latent",
    )(*args)


class _Transposed:
    def __init__(self, a):
        self.a = a


def _scores(qs, keys, biases):
    out = []
    for k, b in zip(keys, biases):
        s = _dot(qs, k.a) if isinstance(k, _Transposed) else _dot_nt(qs, k)
        out.append(s if b is None else s + b)
    return out


def _row_max(s_list, sink):
    m = None
    for s in s_list:
        mi = jnp.max(s, axis=-1, keepdims=True)
        m = mi if m is None else jnp.maximum(m, mi)
    return m if sink is None else jnp.maximum(m, sink)


def _softmax_pv(s_list, vaugs, sink=None):
    m = _row_max(s_list, sink)
    r = None
    for s, v in zip(s_list, vaugs):
        p = jnp.exp((s - m).astype(BF16))
        ri = _dot_nt(p, v.a) if isinstance(v, _Transposed) else _dot(p, v)
        r = ri if r is None else r + ri
    den = r[:, LANES:]
    if sink is not None:
        den = den + jnp.exp(sink - m)
    return r[:, :LANES] / den


def _gqa(qe, qo, keys, vaugs, biases, sinks):
    mq = qe.shape[0]
    oe = jnp.zeros((mq, LANES), F32)
    oo = jnp.zeros((mq, LANES), F32)
    for g in range(2):
        lo, hi = HEAD_DIM * g, HEAD_DIM * (g + 1)
        qs = jnp.concatenate([_mask_q(qe, lo, hi), _mask_q(qo, lo, hi)], axis=0)
        sink = None
        if sinks is not None:
            row = lax.broadcasted_iota(jnp.int32, (2 * mq, 1), 0)
            sink = jnp.where(row < mq, sinks[2 * g], sinks[2 * g + 1])
        r = _softmax_pv(_scores(qs, keys, biases), vaugs, sink)
        msk = _lane_mask((mq, LANES), lo, hi)
        oe = jnp.where(msk, r[:mq], oe)
        oo = jnp.where(msk, r[mq:], oo)
    return _interleave_heads(oe, oo)


def _diff(q, keys, vaug_fn, lam):
    mq = q.shape[0]
    out = [jnp.zeros((mq, LANES), F32), jnp.zeros((mq, LANES), F32)]
    for hd in range(4):
        lo = HEAD_DIM * hd
        qs = jnp.concatenate([_mask_q(q, lo, lo + B_HALF), _mask_q(q, lo + B_HALF, lo + HEAD_DIM)], axis=0)
        o = _softmax_pv(_scores(qs, keys, [None] * len(keys)), vaug_fn(hd // 2))
        o = o[:mq] - lam * o[mq:]
        plo = HEAD_DIM * (hd % 2)
        out[hd // 2] = jnp.where(_lane_mask(o.shape, plo, plo + HEAD_DIM), o, out[hd // 2])
    return out


def _mha(q, keys, vaug_fn, bias_fn):
    mq = q.shape[0]
    out = [jnp.zeros((mq, LANES), F32), jnp.zeros((mq, LANES), F32)]
    for hd in range(4):
        lo = HEAD_DIM * hd
        qs = _mask_q(q, lo, lo + HEAD_DIM)
        o = _softmax_pv(_scores(qs, keys, bias_fn(hd)), vaug_fn(hd // 2))
        plo = HEAD_DIM * (hd % 2)
        out[hd // 2] = jnp.where(_lane_mask(o.shape, plo, plo + HEAD_DIM), o, out[hd // 2])
    return out


def _lambda(lam_ref, lam_init):
    lp = lam_ref[...]
    a = jnp.sum(lp[0:1, :] * lp[1:2, :], axis=-1, keepdims=True)
    b = jnp.sum(lp[2:3, :] * lp[3:4, :], axis=-1, keepdims=True)
    return jnp.exp(a) - jnp.exp(b) + lam_init


def _store_branches(br_ref, oa, ob, oc, od, bd_ref, gsub_ref, lam_init):
    bd = bd_ref[...]
    ob = [_head_rms_mxu(o, bd, gsub_ref[...], SUBLN_EPS) * (1.0 - lam_init) for o in ob]
    for j, o in enumerate((*oa, *ob, *oc, *od)):
        br_ref[:, j * LANES:(j + 1) * LANES] = o.astype(BF16)


def _attn_prompt_kernel(qkv_ref, sink_ref, lam_ref, gsub_ref, bd_ref, br_ref, *, l, lam_init):
    sinks = [sink_ref[l, i] for i in range(4)]
    lam = _lambda(lam_ref, lam_init)
    for bi in range(qkv_ref.shape[0] // SEQ):
        r0 = bi * SEQ

        def cols(c, w):
            return qkv_ref[r0:r0 + SEQ, c:c + w]

        def qcols(c, w):
            return cols(c, w).astype(F32)

        oa = _gqa(qcols(A_QE, 128), qcols(A_QO, 128), [cols(A_K, 128)], [cols(A_V, VAUG)], [None], None)
        ob = _diff(qcols(B_Q, 256), [cols(B_K, 256)], lambda pr: [cols(B_V + pr * VAUG, VAUG)], lam)
        oc = _gqa(qcols(C_QE, 128), qcols(C_QO, 128), [cols(C_K, 128)], [cols(C_V, VAUG)], [None], sinks)
        od = _mha(qcols(D_Q, 256), [cols(D_K, 256)], lambda pr: [cols(D_V + pr * VAUG, VAUG)], lambda hd: [None])
        _store_branches(br_ref.at[r0:r0 + SEQ], oa, ob, oc, od, bd_ref, gsub_ref, lam_init)


def _attn_prompt_call(l, lam_init, qkv, sink_c, lam_b, gsub, bd128):
    rows = qkv.shape[0]
    tm = 4 * SEQ
    return pl.pallas_call(
        functools.partial(_attn_prompt_kernel, l=l, lam_init=lam_init),
        out_shape=jax.ShapeDtypeStruct((rows, D_MODEL), BF16),
        grid=(rows // tm,),
        in_specs=[
            pl.BlockSpec((tm, QKV_W), lambda b: (b, 0)),
            pl.BlockSpec(memory_space=pltpu.SMEM),
            pl.BlockSpec((None, 4, B_HALF), lambda b: (l, 0, 0)),
            pl.BlockSpec((None, 1, LANES), lambda b: (l, 0, 0)),
            pl.BlockSpec((LANES, LANES), lambda b: (0, 0)),
        ],
        out_specs=pl.BlockSpec((tm, D_MODEL), lambda b: (b, 0)),
        compiler_params=pltpu.CompilerParams(vmem_limit_bytes=VMEM_LIMIT),
        name="attn_prompt",
    )(qkv, sink_c, lam_b, gsub, bd128)


def _attn_latent_kernel(kv_ref, xak_ref, xav_ref, xbk_ref, xbv_ref, xck_ref, xcv_ref, xdk_ref, xdv_ref,
                        bias_ref, sink_ref, lam_ref, gsub_ref, bd_ref, br_ref, *, l, lam_init):
    sinks = [sink_ref[l, i] for i in range(4)]
    lam = _lambda(lam_ref, lam_init)
    nsub = br_ref.shape[0] // ATT_BLK
    for sub in range(nsub):
        n = pl.program_id(1) * nsub + sub
        q0 = pl.multiple_of(n * ATT_BLK, ATT_BLK)

        def q(c, w):
            return kv_ref[pl.ds(q0, ATT_BLK), c:c + w].astype(F32)

        def lat(c, w):
            return kv_ref[:, c:c + w]

        def ctx_k(ref):
            return _Transposed(ref[...].reshape(ref.shape[0] * HEAD_DIM, PAST_LEN).astype(BF16))

        def ctx_v(ref, pr):
            vt = ref[2 * pr:2 * pr + 2].reshape(LANES, PAST_LEN).astype(BF16)
            return _Transposed(jnp.concatenate([vt, jnp.ones((LANES, PAST_LEN), BF16)], axis=0))

        oa = _gqa(q(A_QE, 128), q(A_QO, 128), [ctx_k(xak_ref), lat(A_K, 128)], [ctx_v(xav_ref, 0), lat(A_V, VAUG)],
                  [None, None], None)
        ob = _diff(q(B_Q, 256), [ctx_k(xbk_ref), lat(B_K, 256)],
                   lambda pr: [ctx_v(xbv_ref, pr), lat(B_V + pr * VAUG, VAUG)], lam)
        start_c = pl.multiple_of(jnp.clip(q0 - C_WINDOW, 0, DEC_SEQ - C_WIN), C_WINDOW)
        rowq = lax.broadcasted_iota(jnp.int32, (2 * ATT_BLK, C_WIN), 0) & (ATT_BLK - 1)
        colk = lax.broadcasted_iota(jnp.int32, (2 * ATT_BLK, C_WIN), 1)
        band = jnp.where(jnp.abs(rowq - colk + (q0 - start_c)) <= C_WINDOW, 0.0, NEG_INF)
        oc = _gqa(q(C_QE, 128), q(C_QO, 128),
                  [ctx_k(xck_ref), kv_ref[pl.ds(start_c, C_WIN), C_K:C_K + 128]],
                  [ctx_v(xcv_ref, 0), kv_ref[pl.ds(start_c, C_WIN), C_V:C_V + VAUG]],
                  [None, band], sinks)
        start_d = pl.multiple_of(jnp.where(n >= 2, NA_WIN_START[2] * GRID_W, 0), ATT_BLK)
        kwin = kv_ref[pl.ds(start_d, NA_WIN), D_K:D_K + 256]
        od = _mha(q(D_Q, 256), [ctx_k(xdk_ref), kwin],
                  lambda pr: [ctx_v(xdv_ref, pr),
                              kv_ref[pl.ds(start_d, NA_WIN), D_V + pr * VAUG:D_V + (pr + 1) * VAUG]],
                  lambda hd: [None, bias_ref[hd, sub]])
        _store_branches(br_ref.at[sub * ATT_BLK:(sub + 1) * ATT_BLK], oa, ob, oc, od, bd_ref, gsub_ref, lam_init)


def _attn_latent_call(l, lam_init, qkv, caches, dbias, sink_c, lam_b, gsub, bd128):
    rows = qkv.shape[0]
    nblk = DEC_SEQ // ATT_BLK
    nsub = 1
    cache_specs = [pl.BlockSpec((None, None, t.shape[2], HEAD_DIM, PAST_LEN), lambda b, n: (b, l, 0, 0, 0))
                   for t in caches]
    return pl.pallas_call(
        functools.partial(_attn_latent_kernel, l=l, lam_init=lam_init),
        out_shape=jax.ShapeDtypeStruct((rows, D_MODEL), BF16),
        grid=(DEC_BATCH, nblk // nsub),
        in_specs=[
            pl.BlockSpec((DEC_SEQ, QKV_W), lambda b, n: (b, 0)),
            *cache_specs,
            pl.BlockSpec((None, 4, nsub, ATT_BLK, NA_WIN), lambda b, n: (l, 0, n, 0, 0)),
            pl.BlockSpec(memory_space=pltpu.SMEM),
            pl.BlockSpec((None, 4, B_HALF), lambda b, n: (l, 0, 0)),
            pl.BlockSpec((None, 1, LANES), lambda b, n: (l, 0, 0)),
            pl.BlockSpec((LANES, LANES), lambda b, n: (0, 0)),
        ],
        out_specs=pl.BlockSpec((nsub * ATT_BLK, D_MODEL), lambda b, n: (b * (nblk // nsub) + n, 0)),
        compiler_params=pltpu.CompilerParams(vmem_limit_bytes=VMEM_LIMIT),
        name="attn_latent",
    )(qkv, *caches, dbias, sink_c, lam_b, gsub, bd128)


MXU_TILE = 256
FFN_CHUNKS = ((0, 6 * MXU_TILE), (6 * MXU_TILE, D_FF))


def _post_kernel(x_ref, br_ref, gates_ref, gt1_ref, sh2_ref, sc2_ref, gt2_ref, g2_ref,
                 wb_ref, wo_ref, wfi_ref, wfo_ref, gf_ref, o_ref, *, final):
    merged = None
    for k in range(4):
        proj = _dot(br_ref[:, k * 256:(k + 1) * 256], wb_ref[k].astype(BF16))
        t = gates_ref[:, k * D_MODEL:(k + 1) * D_MODEL].astype(F32) * proj
        merged = t if merged is None else merged + t
    x1 = x_ref[...] + (0.5 * gt1_ref[...]) * _dot(merged.astype(BF16), wo_ref[...].astype(BF16))
    h2 = (_rms(x1, g2_ref[...], NORM_EPS) * (1.0 + sc2_ref[...]) + sh2_ref[...]).astype(BF16)
    acc = None
    for c0, c1 in FFN_CHUNKS:
        a = _dot(h2, wfi_ref[:, c0:c1])
        u = _dot(h2, wfi_ref[:, D_FF + c0:D_FF + c1])
        g = ((a * _sigmoid(a)) * u).astype(BF16)
        t = _dot(g, wfo_ref[c0:c1, :])
        acc = t if acc is None else acc + t
    xo = x1 + gt2_ref[...] * acc
    if final:
        xo = _rms(xo, gf_ref[...], NORM_EPS)
    o_ref[...] = xo


def _post_call(l, x, br, gates, mod, g2, wb, wo, wfi, wfo, gf, *, prompt, final):
    rows = x.shape[0]
    tm = 512
    if prompt:
        mod_row = lambda i: CTX_ROW
    else:
        mod_row = lambda i: i // (DEC_SEQ // tm)

    def mod_spec(chunk):
        return pl.BlockSpec((None, None, 1, D_MODEL), lambda i: (l, mod_row(i), 0, chunk))

    def resident(shape):
        nd = len(shape)
        return pl.BlockSpec((None,) + shape, lambda i: (l,) + (0,) * nd, pipeline_mode=pl.Buffered(1))

    return pl.pallas_call(
        functools.partial(_post_kernel, final=final),
        out_shape=jax.ShapeDtypeStruct((rows, D_MODEL), F32),
        grid=(rows // tm,),
        in_specs=[
            pl.BlockSpec((tm, D_MODEL), lambda i: (i, 0)),
            pl.BlockSpec((tm, D_MODEL), lambda i: (i, 0)),
            pl.BlockSpec((tm, GATE_W), lambda i: (i, 0)),
            mod_spec(2), mod_spec(3), mod_spec(4), mod_spec(5),
            pl.BlockSpec((None, 1, D_MODEL), lambda i: (l, 0, 0)),
            resident((4, 256, D_MODEL)),
            resident((D_MODEL, D_MODEL)),
            resident((D_MODEL, 2 * D_FF)),
            resident((D_FF, D_MODEL)),
            pl.BlockSpec((1, D_MODEL), lambda i: (0, 0)),
        ],
        out_specs=pl.BlockSpec((tm, D_MODEL), lambda i: (i, 0)),
        compiler_params=pltpu.CompilerParams(vmem_limit_bytes=VMEM_LIMIT),
        name="post_prompt" if prompt else "post_latent",
    )(x, br, gates, mod, mod, mod, mod, g2, wb, wo, wfi, wfo, gf)


def _rope_tables():
    t = jnp.arange(DEC_SEQ)
    row = (t // GRID_W).astype(F32)[:, None]
    col = (t % GRID_W).astype(F32)[:, None]
    tabs = []
    for d in (HEAD_DIM, B_HALF):
        quarter = d // 4
        inv = jnp.power(jnp.float32(ROPE_THETA), -jnp.arange(quarter, dtype=F32) / quarter)
        ar, ac = row * inv, col * inv
        cos = jnp.concatenate([jnp.cos(ar), jnp.cos(ar), jnp.cos(ac), jnp.cos(ac)], axis=-1)
        sin = jnp.concatenate([-jnp.sin(ar), jnp.sin(ar), -jnp.sin(ac), jnp.sin(ac)], axis=-1)
        reps = LANES // d
        tabs += [jnp.tile(cos, (1, reps)), jnp.tile(sin, (1, reps))]
    return tabs


def kernel(x_prompt, x_sample, cache_a_k, cache_a_v, cache_b_k, cache_b_v, cache_c_k, cache_c_v, cache_d_k, cache_d_v, c, c_ctx, w_ada, b_ada, g_norm1, w_in, g_q_a, g_k_a, lam_b, g_subln_b, sink_c, rpb_d, w_branch, w_out, g_norm2, w_ffn_in, w_ffn_out, g_final):
    gate_half = jnp.where(jnp.arange(IN_COLS) >= W_GATES, 0.5, 1.0).astype(F32)
    w_in_b = (w_in * gate_half).astype(BF16)
    wb = w_branch
    wo = w_out
    wfi = w_ffn_in.astype(BF16)
    wfo = w_ffn_out.astype(BF16)
    gq = jnp.tile(g_q_a, (1, 2)).reshape(DEPTH, 1, LANES)
    gk = jnp.tile(g_k_a, (1, 2)).reshape(DEPTH, 1, LANES)
    gsub = jnp.tile(g_subln_b, (1, 2)).reshape(DEPTH, 1, LANES)
    g1 = g_norm1.reshape(DEPTH, 1, D_MODEL)
    g2 = g_norm2.reshape(DEPTH, 1, D_MODEL)
    gf = g_final.reshape(1, D_MODEL)
    head_of_lane = jnp.arange(LANES) // HEAD_DIM
    bd128 = ((head_of_lane[:, None] == head_of_lane[None, :]).astype(F32) / HEAD_DIM).astype(BF16)
    rope_tabs = _rope_tables()
    caches = tuple(jnp.swapaxes(t, 3, 4) for t in (cache_a_k, cache_a_v, cache_b_k, cache_b_v,
                                                   cache_c_k, cache_c_v, cache_d_k, cache_d_v))

    cond = jnp.concatenate([c, c_ctx[None, :], jnp.zeros((MOD_ROWS - DEC_BATCH - 1, D_MODEL), F32)], axis=0)
    mod = _ada_call(cond, w_ada, b_ada).reshape(DEPTH, MOD_ROWS, 1, 6 * D_MODEL)
    dbias = _dbias_call(rpb_d)

    xp = x_prompt.reshape(BATCH * SEQ, D_MODEL)
    xs = x_sample.reshape(DEC_BATCH * DEC_SEQ, D_MODEL)
    new_kv = None
    for l in range(DEPTH):
        lam_init = 0.8 - 0.6 * math.exp(-0.3 * l)
        final = l == DEPTH - 1
        outs = _pre_call(l, xp, mod, g1, w_in_b, gq, gk, None, new_kv, prompt=True)
        qkv_p, gates_p, new_kv = outs[0], outs[1], outs[2:]
        br_p = _attn_prompt_call(l, lam_init, qkv_p, sink_c, lam_b, gsub, bd128)
        xp = _post_call(l, xp, br_p, gates_p, mod, g2, wb, wo, wfi, wfo, gf, prompt=True, final=final)

        qkv_s, gates_s = _pre_call(l, xs, mod, g1, w_in_b, gq, gk, rope_tabs, None, prompt=False)
        br_s = _attn_latent_call(l, lam_init, qkv_s, caches, dbias, sink_c, lam_b, gsub, bd128)
        xs = _post_call(l, xs, br_s, gates_s, mod, g2, wb, wo, wfi, wfo, gf, prompt=False, final=final)

    y_prompt = xp.reshape(BATCH, SEQ, D_MODEL)
    y_sample = xs.reshape(DEC_BATCH, DEC_SEQ, D_MODEL)
    return (y_prompt, y_sample, *(jnp.swapaxes(t, 3, 4) for t in new_kv))
```

```python
import functools
import math
from typing import NamedTuple

import jax
import jax.numpy as jnp
from jax import lax
from jax.experimental import pallas as pl
from jax.experimental.pallas import tpu as pltpu

F32 = jnp.float32
BF16 = jnp.bfloat16

D_MODEL = 1024
BATCH = 32
SEQ = 256
DEPTH = 2
DEC_BATCH = 8
DEC_SEQ = 1024
PAST_LEN = 256
GRID_W = 64
HEAD_DIM = 64
B_HALF = HEAD_DIM // 2
C_WINDOW = 128
NA_ROWS = 8
NA_COLS = 16
D_FF = 2816
ROPE_THETA = 10000.0
NORM_EPS = 1e-6
SUBLN_EPS = 1e-5
NEG_INF = -1e30

GATE_W = 4 * D_MODEL
W_A, W_B, W_C, W_D, W_GATES = 0, 512, 1280, 1792, 2560
IN_COLS = W_GATES + GATE_W
MOD_ROWS = 16
CTX_ROW = DEC_BATCH
LANES = 128
ATT_BLK = 256
NA_BLK_ROWS = ATT_BLK // GRID_W
NA_WIN_ROWS = 12
NA_WIN = NA_WIN_ROWS * GRID_W
NA_WIN_START = (0, 0, 4, 4)
C_WIN = ATT_BLK + 2 * C_WINDOW
VMEM_LIMIT = 56 * 1024 * 1024

A_QE, A_QO, A_K, A_V = 0, 128, 256, 384
B_Q, B_K, B_V = 640, 896, 1152
C_QE, C_QO, C_K, C_V = 1664, 1792, 1920, 2048
D_Q, D_K, D_V = 2304, 2560, 2816
QKV_W = 3328
VAUG = 2 * LANES


def _dot(a, b):
    return jnp.dot(a, b, preferred_element_type=F32)


def _dot_nt(a, b):
    return lax.dot_general(a, b, (((1,), (1,)), ((), ())), preferred_element_type=F32)


def _sigmoid(x):
    return 0.5 * jnp.tanh(0.5 * x) + 0.5


def _rms(x, g, eps):
    ms = jnp.mean(x * x, axis=-1, keepdims=True)
    return x * lax.rsqrt(ms + eps) * g


def _head_rms_mxu(v, bd, g, eps):
    v2 = v * v
    hi = v2.astype(BF16)
    lo = (v2 - hi.astype(F32)).astype(BF16)
    ms = _dot(hi, bd) + _dot(lo, bd)
    return v * lax.rsqrt(ms + eps) * g


def _head_rms(v, g, eps):
    v2 = v * v
    left = _lane_mask(v.shape, 0, HEAD_DIM)
    s_left = jnp.sum(jnp.where(left, v2, 0.0), axis=-1, keepdims=True)
    s_right = jnp.sum(jnp.where(left, 0.0, v2), axis=-1, keepdims=True)
    ms = jnp.where(left, s_left, s_right) * (1.0 / HEAD_DIM)
    return v * lax.rsqrt(ms + eps) * g


def _lane_mask(shape, lo, hi):
    lane = lax.broadcasted_iota(jnp.int32, shape, 1)
    return (lane >= lo) & (lane < hi)


def _mask_q(qf, lo, hi):
    return jnp.where(_lane_mask(qf.shape, lo, hi), qf, 0.0).astype(BF16)


def _interleave_heads(lo, hi):
    left = _lane_mask(lo.shape, 0, HEAD_DIM)
    return (jnp.where(left, lo, pltpu.roll(hi, HEAD_DIM, 1)),
            jnp.where(left, pltpu.roll(lo, HEAD_DIM, 1), hi))


def _ada_kernel(cond_ref, w_ref, b_ref, o_ref):
    c = cond_ref[...]
    s = (c * _sigmoid(c)).astype(BF16)
    o_ref[...] = _dot(s, w_ref[...].astype(BF16)) + b_ref[...]


def _ada_call(cond, w_ada, b_ada):
    tn = 1536
    return pl.pallas_call(
        _ada_kernel,
        out_shape=jax.ShapeDtypeStruct((DEPTH, MOD_ROWS, 6 * D_MODEL), F32),
        grid=(DEPTH, 6 * D_MODEL // tn),
        in_specs=[
            pl.BlockSpec((MOD_ROWS, D_MODEL), lambda l, j: (0, 0)),
            pl.BlockSpec((None, D_MODEL, tn), lambda l, j: (l, 0, j)),
            pl.BlockSpec((None, 1, tn), lambda l, j: (l, 0, j)),
        ],
        out_specs=pl.BlockSpec((None, MOD_ROWS, tn), lambda l, j: (l, 0, j)),
        compiler_params=pltpu.CompilerParams(vmem_limit_bytes=VMEM_LIMIT),
        name="ada",
    )(cond, w_ada, b_ada.reshape(DEPTH, 1, 6 * D_MODEL))


def _dbias_kernel(rpb_ref, o_ref):
    n_dr, n_dc = 2 * NA_ROWS - 1, 2 * NA_COLS - 1
    base = (pl.program_id(0) * 4 + pl.program_id(1)) * (n_dr * n_dc)
    shape = (GRID_W, LANES)
    cq = lax.broadcasted_iota(jnp.int32, shape, 0)
    lane = lax.broadcasted_iota(jnp.int32, shape, 1)
    ck = lane & (GRID_W - 1)
    dc = jnp.clip(ck - cq, -(NA_COLS - 1), NA_COLS - 1) + (NA_COLS - 1)
    start_c = jnp.clip(cq - NA_COLS // 2, 0, GRID_W - NA_COLS)
    col_valid = (ck >= start_c) & (ck < start_c + NA_COLS)
    neg = jnp.full(shape, NEG_INF, F32)
    toeplitz = []
    for dr in range(n_dr):
        t = jnp.zeros(shape, F32)
        for m in range(n_dc):
            t = jnp.where(dc == m, rpb_ref[base + dr * n_dc + m], t)
        toeplitz.append(jnp.where(col_valid, t, neg))
    left = lane < GRID_W
    rows = DEC_SEQ // GRID_W
    for n in range(DEC_SEQ // ATT_BLK):
        for rq in range(NA_BLK_ROWS):
            r = NA_BLK_ROWS * n + rq
            start_r = min(max(r - NA_ROWS // 2, 0), rows - NA_ROWS)
            for jp in range(NA_WIN_ROWS // 2):
                pair = []
                for j in (2 * jp, 2 * jp + 1):
                    key_row = NA_WIN_START[n] + j
                    valid = start_r <= key_row < start_r + NA_ROWS
                    pair.append(toeplitz[key_row - r + NA_ROWS - 1] if valid else neg)
                tile = pair[0] if pair[0] is pair[1] else jnp.where(left, pair[0], pair[1])
                o_ref[n, rq * GRID_W:(rq + 1) * GRID_W, jp * LANES:(jp + 1) * LANES] = tile


def _dbias_call(rpb_d):
    nblk = DEC_SEQ // ATT_BLK
    return pl.pallas_call(
        _dbias_kernel,
        out_shape=jax.ShapeDtypeStruct((DEPTH, 4, nblk, ATT_BLK, NA_WIN), F32),
        grid=(DEPTH, 4),
        in_specs=[pl.BlockSpec(memory_space=pltpu.SMEM)],
        out_specs=pl.BlockSpec((None, None, nblk, ATT_BLK, NA_WIN), lambda l, h: (l, h, 0, 0, 0)),
        name="dbias",
    )(rpb_d.reshape(-1))


def _swap_halves(v, half):
    lane = lax.broadcasted_iota(jnp.int32, v.shape, 1)
    up = pltpu.roll(v, LANES - half, 1)
    dn = pltpu.roll(v, half, 1)
    return jnp.where((lane & (2 * half - 1)) < half, up, dn)


def _rope(v, cos, sin, half):
    return v * cos + _swap_halves(v, half) * sin


def _pre_kernel(*refs, prompt, first):
    if prompt:
        (x_ref, sh_ref, sc_ref, g1_ref, w_ref, gq_ref, gk_ref) = refs[:7]
        (qkv_ref, gates_ref, ka_ref, va_ref, kb_ref, vb_ref, kc_ref, vc_ref, kd_ref, vd_ref) = refs[-10:]
    else:
        (x_ref, sh_ref, sc_ref, g1_ref, w_ref, gq_ref, gk_ref,
         ca_ref, sa_ref, cb_ref, sb_ref, qkv_ref, gates_ref) = refs

    h = (_rms(x_ref[...], g1_ref[...], NORM_EPS) * (1.0 + sc_ref[...]) + sh_ref[...]).astype(BF16)
    ones = jnp.ones((h.shape[0], LANES), BF16)

    def rope_a(v):
        return v if prompt else _rope(v, ca_ref[...], sa_ref[...], 16)

    def rope_b(v):
        return v if prompt else _rope(v, cb_ref[...], sb_ref[...], 8)

    def put(col, v):
        qkv_ref[:, col:col + v.shape[1]] = v.astype(BF16)

    def put_values(col, v):
        for c in range(v.shape[1] // LANES):
            put(col + c * VAUG, v[:, c * LANES:(c + 1) * LANES])
            put(col + c * VAUG + LANES, ones)

    def put_heads(ref, v):
        for bi in range(v.shape[0] // SEQ):
            for pr in range(v.shape[1] // LANES):
                t = v[bi * SEQ:(bi + 1) * SEQ, pr * LANES:(pr + 1) * LANES].T
                for hh in range(2):
                    piece = t[hh * HEAD_DIM:(hh + 1) * HEAD_DIM]
                    if first:
                        ref[bi, 0, 2 * pr + hh] = piece
                        for later in range(1, DEPTH):
                            ref[bi, later, 2 * pr + hh] = jnp.zeros_like(piece)
                    else:
                        ref[bi, 2 * pr + hh] = piece

    scale = HEAD_DIM ** -0.5
    acc_a = _dot(h, w_ref[:, W_A:W_A + 512])
    acc = _dot(h, w_ref[:, W_B:W_B + 768])
    scale_b = B_HALF ** -0.5
    for c in range(2):
        put(B_Q + c * 128, rope_b(acc[:, c * 128:(c + 1) * 128]) * scale_b)
        put(B_K + c * 128, rope_b(acc[:, 256 + c * 128:256 + (c + 1) * 128]))
    put_values(B_V, acc[:, 512:768])
    if prompt:
        put_heads(kb_ref, acc[:, 256:512])
        put_heads(vb_ref, acc[:, 512:768])
    acc = _dot(h, w_ref[:, W_C:W_C + 512])
    q_e, q_o = _interleave_heads(acc[:, 0:128], acc[:, 128:256])
    put(C_QE, rope_a(q_e) * scale)
    put(C_QO, rope_a(q_o) * scale)
    put(C_K, rope_a(acc[:, 256:384]))
    put_values(C_V, acc[:, 384:512])
    if prompt:
        put_heads(kc_ref, acc[:, 256:384])
        put_heads(vc_ref, acc[:, 384:512])
    acc = _dot(h, w_ref[:, W_D:W_D + 768])
    put(D_Q, acc[:, 0:256] * scale)
    put(D_K, acc[:, 256:512])
    put_values(D_V, acc[:, 512:768])
    if prompt:
        put_heads(kd_ref, acc[:, 256:512])
        put_heads(vd_ref, acc[:, 512:768])
    gq = gq_ref[...]
    q_e, q_o = _interleave_heads(acc_a[:, 0:128], acc_a[:, 128:256])
    put(A_QE, rope_a(_head_rms(q_e, gq, NORM_EPS)) * scale)
    put(A_QO, rope_a(_head_rms(q_o, gq, NORM_EPS)) * scale)
    k_a = _head_rms(acc_a[:, 256:384], gk_ref[...], NORM_EPS)
    put(A_K, rope_a(k_a))
    put_values(A_V, acc_a[:, 384:512])
    if prompt:
        put_heads(ka_ref, k_a)
        put_heads(va_ref, acc_a[:, 384:512])
    for j in range(GATE_W // 512):
        g = _dot(h, w_ref[:, W_GATES + j * 512:W_GATES + (j + 1) * 512])
        gates_ref[:, j * 512:(j + 1) * 512] = (jnp.tanh(0.5 * g) + 1.0).astype(BF16)


def _pre_call(l, x, mod, g1, w_in, gq, gk, rope_tabs, kv_prev, *, prompt):
    rows = x.shape[0]
    tm = 512
    tq = tm
    bpb = tm // SEQ
    aliases = {}
    if prompt:
        mod_row = lambda i: CTX_ROW
    else:
        mod_row = lambda i: i // (DEC_SEQ // tm)
    in_specs = [
        pl.BlockSpec((tm, D_MODEL), lambda i: (i, 0)),
        pl.BlockSpec((None, None, 1, D_MODEL), lambda i: (l, mod_row(i), 0, 0)),
        pl.BlockSpec((None, None, 1, D_MODEL), lambda i: (l, mod_row(i), 0, 1)),
        pl.BlockSpec((None, 1, D_MODEL), lambda i: (l, 0, 0)),
        pl.BlockSpec((None, D_MODEL, IN_COLS), lambda i: (0, 0, 0), pipeline_mode=pl.Buffered(1)),
        pl.BlockSpec((None, 1, LANES), lambda i: (l, 0, 0)),
        pl.BlockSpec((None, 1, LANES), lambda i: (l, 0, 0)),
    ]
    args = [x, mod, mod, g1, w_in, gq, gk]
    out_shape = [jax.ShapeDtypeStruct((rows, QKV_W), BF16), jax.ShapeDtypeStruct((rows, GATE_W), BF16)]
    out_specs = [pl.BlockSpec((tm, QKV_W), lambda i: (i, 0)), pl.BlockSpec((tm, GATE_W), lambda i: (i, 0))]
    if prompt:
        for j, nh in enumerate((2, 2, 4, 4, 2, 2, 4, 4)):
            out_shape.append(jax.ShapeDtypeStruct((BATCH, DEPTH, nh, HEAD_DIM, SEQ), F32))
            if kv_prev is None:
                out_specs.append(pl.BlockSpec((bpb, DEPTH, nh, HEAD_DIM, SEQ), lambda i: (i, 0, 0, 0, 0)))
            else:
                out_specs.append(pl.BlockSpec((bpb, None, nh, HEAD_DIM, SEQ), lambda i: (i, l, 0, 0, 0)))
                aliases[len(args)] = 2 + j
                in_specs.append(pl.BlockSpec(memory_space=pl.ANY))
                args.append(kv_prev[j])
    else:
        nt = DEC_SEQ // tq
        for t in rope_tabs:
            in_specs.append(pl.BlockSpec((tq, LANES), lambda i: (i % nt, 0)))
            args.append(t)
    return pl.pallas_call(
        functools.partial(_pre_kernel, prompt=prompt, first=kv_prev is None),
        out_shape=out_shape,
        grid=(rows // tm,),
        in_specs=in_specs,
        out_specs=out_specs,
        input_output_aliases=aliases,
        compiler_params=pltpu.CompilerParams(vmem_limit_bytes=VMEM_LIMIT),
        name="pre_prompt" if prompt else "pre_latent",
    )(*args)


class _CastJob(NamedTuple):
    src: jax.Array
    layer: int
    chunks: int


def _cast_job_specs(jobs, n_steps, linear_step):
    in_specs, args, out_specs, out_shape = [], [], [], []
    for job in jobs:
        _, rows, cols = job.src.shape
        chunk_rows = rows // job.chunks
        per = n_steps // job.chunks

        def chunk_map(*idx, layer, per=per):
            return (layer, linear_step(*idx) // per, 0)

        in_specs.append(pl.BlockSpec((None, chunk_rows, cols), functools.partial(chunk_map, layer=job.layer)))
        args.append(job.src)
        out_specs.append(pl.BlockSpec((None, chunk_rows, cols), functools.partial(chunk_map, layer=0)))
        out_shape.append(jax.ShapeDtypeStruct((1, rows, cols), BF16))
    return in_specs, args, out_specs, out_shape


def _run_cast_jobs(in_refs, out_refs):
    for i_ref, o_ref in zip(in_refs, out_refs, strict=True):
        o_ref[...] = i_ref[...].astype(BF16)


class _Transposed:
    def __init__(self, a):
        self.a = a


def _scores(qs, keys, biases):
    out = []
    for k, b in zip(keys, biases):
        s = _dot(qs, k.a) if isinstance(k, _Transposed) else _dot_nt(qs, k)
        out.append(s if b is None else s + b)
    return out


def _row_max(s_list, sink):
    m = None
    for s in s_list:
        mi = jnp.max(s, axis=-1, keepdims=True)
        m = mi if m is None else jnp.maximum(m, mi)
    return m if sink is None else jnp.maximum(m, sink)


def _softmax_pv(s_list, vaugs, sink=None):
    m = _row_max(s_list, sink)
    r = None
    for s, v in zip(s_list, vaugs):
        p = jnp.exp((s - m).astype(BF16))
        ri = _dot_nt(p, v.a) if isinstance(v, _Transposed) else _dot(p, v)
        r = ri if r is None else r + ri
    den = r[:, LANES:]
    if sink is not None:
        den = den + jnp.exp(sink - m)
    return r[:, :LANES] / den


def _gqa(qe, qo, keys, vaugs, biases, sinks):
    mq = qe.shape[0]
    oe = jnp.zeros((mq, LANES), F32)
    oo = jnp.zeros((mq, LANES), F32)
    for g in range(2):
        lo, hi = HEAD_DIM * g, HEAD_DIM * (g + 1)
        qs = jnp.concatenate([_mask_q(qe, lo, hi), _mask_q(qo, lo, hi)], axis=0)
        sink = None
        if sinks is not None:
            row = lax.broadcasted_iota(jnp.int32, (2 * mq, 1), 0)
            sink = jnp.where(row < mq, sinks[2 * g], sinks[2 * g + 1])
        r = _softmax_pv(_scores(qs, keys, biases), vaugs, sink)
        msk = _lane_mask((mq, LANES), lo, hi)
        oe = jnp.where(msk, r[:mq], oe)
        oo = jnp.where(msk, r[mq:], oo)
    return _interleave_heads(oe, oo)


def _diff(q, keys, vaug_fn, lam):
    mq = q.shape[0]
    out = [jnp.zeros((mq, LANES), F32), jnp.zeros((mq, LANES), F32)]
    for hd in range(4):
        lo = HEAD_DIM * hd
        qs = jnp.concatenate([_mask_q(q, lo, lo + B_HALF), _mask_q(q, lo + B_HALF, lo + HEAD_DIM)], axis=0)
        o = _softmax_pv(_scores(qs, keys, [None] * len(keys)), vaug_fn(hd // 2))
        o = o[:mq] - lam * o[mq:]
        plo = HEAD_DIM * (hd % 2)
        out[hd // 2] = jnp.where(_lane_mask(o.shape, plo, plo + HEAD_DIM), o, out[hd // 2])
    return out


def _mha(q, keys, vaug_fn, bias_fn):
    mq = q.shape[0]
    out = [jnp.zeros((mq, LANES), F32), jnp.zeros((mq, LANES), F32)]
    for hd in range(4):
        lo = HEAD_DIM * hd
        qs = _mask_q(q, lo, lo + HEAD_DIM)
        o = _softmax_pv(_scores(qs, keys, bias_fn(hd)), vaug_fn(hd // 2))
        plo = HEAD_DIM * (hd % 2)
        out[hd // 2] = jnp.where(_lane_mask(o.shape, plo, plo + HEAD_DIM), o, out[hd // 2])
    return out


def _lambda(lam_ref, lam_init):
    lp = lam_ref[...]
    a = jnp.sum(lp[0:1, :] * lp[1:2, :], axis=-1, keepdims=True)
    b = jnp.sum(lp[2:3, :] * lp[3:4, :], axis=-1, keepdims=True)
    return jnp.exp(a) - jnp.exp(b) + lam_init


def _store_branches(br_ref, oa, ob, oc, od, bd_ref, gsub_ref, lam_init):
    bd = bd_ref[...]
    ob = [_head_rms_mxu(o, bd, gsub_ref[...], SUBLN_EPS) * (1.0 - lam_init) for o in ob]
    for j, o in enumerate((*oa, *ob, *oc, *od)):
        br_ref[:, j * LANES:(j + 1) * LANES] = o.astype(BF16)


def _attn_prompt_kernel(*refs, l, lam_init, n_jobs):
    qkv_ref, sink_ref, lam_ref, gsub_ref, bd_ref = refs[:5]
    br_ref = refs[5 + n_jobs]
    _run_cast_jobs(refs[5:5 + n_jobs], refs[6 + n_jobs:])
    sinks = [sink_ref[l, i] for i in range(4)]
    lam = _lambda(lam_ref, lam_init)
    for bi in range(qkv_ref.shape[0] // SEQ):
        r0 = bi * SEQ

        def cols(c, w):
            return qkv_ref[r0:r0 + SEQ, c:c + w]

        def qcols(c, w):
            return cols(c, w).astype(F32)

        oa = _gqa(qcols(A_QE, 128), qcols(A_QO, 128), [cols(A_K, 128)], [cols(A_V, VAUG)], [None], None)
        ob = _diff(qcols(B_Q, 256), [cols(B_K, 256)], lambda pr: [cols(B_V + pr * VAUG, VAUG)], lam)
        oc = _gqa(qcols(C_QE, 128), qcols(C_QO, 128), [cols(C_K, 128)], [cols(C_V, VAUG)], [None], sinks)
        od = _mha(qcols(D_Q, 256), [cols(D_K, 256)], lambda pr: [cols(D_V + pr * VAUG, VAUG)], lambda hd: [None])
        _store_branches(br_ref.at[r0:r0 + SEQ], oa, ob, oc, od, bd_ref, gsub_ref, lam_init)


def _attn_prompt_call(l, lam_init, qkv, sink_c, lam_b, gsub, bd128, jobs):
    rows = qkv.shape[0]
    tm = 4 * SEQ
    job_in, job_args, job_out, job_shape = _cast_job_specs(jobs, rows // tm, lambda b: b)
    outs = pl.pallas_call(
        functools.partial(_attn_prompt_kernel, l=l, lam_init=lam_init, n_jobs=len(jobs)),
        out_shape=[jax.ShapeDtypeStruct((rows, D_MODEL), BF16), *job_shape],
        grid=(rows // tm,),
        in_specs=[
            pl.BlockSpec((tm, QKV_W), lambda b: (b, 0)),
            pl.BlockSpec(memory_space=pltpu.SMEM),
            pl.BlockSpec((None, 4, B_HALF), lambda b: (l, 0, 0)),
            pl.BlockSpec((None, 1, LANES), lambda b: (l, 0, 0)),
            pl.BlockSpec((LANES, LANES), lambda b: (0, 0)),
            *job_in,
        ],
        out_specs=[pl.BlockSpec((tm, D_MODEL), lambda b: (b, 0)), *job_out],
        compiler_params=pltpu.CompilerParams(vmem_limit_bytes=VMEM_LIMIT),
        name="attn_prompt",
    )(qkv, sink_c, lam_b, gsub, bd128, *job_args)
    return outs[0], outs[1:]


def _attn_latent_kernel(*refs, l, lam_init, n_jobs):
    (kv_ref, xak_ref, xav_ref, xbk_ref, xbv_ref, xck_ref, xcv_ref, xdk_ref, xdv_ref,
     bias_ref, sink_ref, lam_ref, gsub_ref, bd_ref) = refs[:14]
    br_ref = refs[14 + n_jobs]
    _run_cast_jobs(refs[14:14 + n_jobs], refs[15 + n_jobs:])
    sinks = [sink_ref[l, i] for i in range(4)]
    lam = _lambda(lam_ref, lam_init)
    nsub = br_ref.shape[0] // ATT_BLK
    for sub in range(nsub):
        n = pl.program_id(1) * nsub + sub
        q0 = pl.multiple_of(n * ATT_BLK, ATT_BLK)

        def q(c, w):
            return kv_ref[pl.ds(q0, ATT_BLK), c:c + w].astype(F32)

        def lat(c, w):
            return kv_ref[:, c:c + w]

        def ctx_k(ref):
            return _Transposed(ref[...].reshape(ref.shape[0] * HEAD_DIM, PAST_LEN).astype(BF16))

        def ctx_v(ref, pr):
            vt = ref[2 * pr:2 * pr + 2].reshape(LANES, PAST_LEN).astype(BF16)
            return _Transposed(jnp.concatenate([vt, jnp.ones((LANES, PAST_LEN), BF16)], axis=0))

        oa = _gqa(q(A_QE, 128), q(A_QO, 128), [ctx_k(xak_ref), lat(A_K, 128)], [ctx_v(xav_ref, 0), lat(A_V, VAUG)],
                  [None, None], None)
        ob = _diff(q(B_Q, 256), [ctx_k(xbk_ref), lat(B_K, 256)],
                   lambda pr: [ctx_v(xbv_ref, pr), lat(B_V + pr * VAUG, VAUG)], lam)
        start_c = pl.multiple_of(jnp.clip(q0 - C_WINDOW, 0, DEC_SEQ - C_WIN), C_WINDOW)
        rowq = lax.broadcasted_iota(jnp.int32, (2 * ATT_BLK, C_WIN), 0) & (ATT_BLK - 1)
        colk = lax.broadcasted_iota(jnp.int32, (2 * ATT_BLK, C_WIN), 1)
        band = jnp.where(jnp.abs(rowq - colk + (q0 - start_c)) <= C_WINDOW, 0.0, NEG_INF)
        oc = _gqa(q(C_QE, 128), q(C_QO, 128),
                  [ctx_k(xck_ref), kv_ref[pl.ds(start_c, C_WIN), C_K:C_K + 128]],
                  [ctx_v(xcv_ref, 0), kv_ref[pl.ds(start_c, C_WIN), C_V:C_V + VAUG]],
                  [None, band], sinks)
        start_d = pl.multiple_of(jnp.where(n >= 2, NA_WIN_START[2] * GRID_W, 0), ATT_BLK)
        kwin = kv_ref[pl.ds(start_d, NA_WIN), D_K:D_K + 256]
        od = _mha(q(D_Q, 256), [ctx_k(xdk_ref), kwin],
                  lambda pr: [ctx_v(xdv_ref, pr),
                              kv_ref[pl.ds(start_d, NA_WIN), D_V + pr * VAUG:D_V + (pr + 1) * VAUG]],
                  lambda hd: [None, bias_ref[hd, sub]])
        _store_branches(br_ref.at[sub * ATT_BLK:(sub + 1) * ATT_BLK], oa, ob, oc, od, bd_ref, gsub_ref, lam_init)


def _attn_latent_call(l, lam_init, qkv, caches, dbias, sink_c, lam_b, gsub, bd128, jobs):
    rows = qkv.shape[0]
    nblk = DEC_SEQ // ATT_BLK
    nsub = 1
    steps_per_b = nblk // nsub
    cache_specs = [pl.BlockSpec((None, None, t.shape[2], HEAD_DIM, PAST_LEN), lambda b, n: (b, l, 0, 0, 0))
                   for t in caches]
    job_in, job_args, job_out, job_shape = _cast_job_specs(jobs, DEC_BATCH * steps_per_b,
                                                           lambda b, n: b * steps_per_b + n)
    outs = pl.pallas_call(
        functools.partial(_attn_latent_kernel, l=l, lam_init=lam_init, n_jobs=len(jobs)),
        out_shape=[jax.ShapeDtypeStruct((rows, D_MODEL), BF16), *job_shape],
        grid=(DEC_BATCH, steps_per_b),
        in_specs=[
            pl.BlockSpec((DEC_SEQ, QKV_W), lambda b, n: (b, 0)),
            *cache_specs,
            pl.BlockSpec((None, 4, nsub, ATT_BLK, NA_WIN), lambda b, n: (l, 0, n, 0, 0)),
            pl.BlockSpec(memory_space=pltpu.SMEM),
            pl.BlockSpec((None, 4, B_HALF), lambda b, n: (l, 0, 0)),
            pl.BlockSpec((None, 1, LANES), lambda b, n: (l, 0, 0)),
            pl.BlockSpec((LANES, LANES), lambda b, n: (0, 0)),
            *job_in,
        ],
        out_specs=[pl.BlockSpec((nsub * ATT_BLK, D_MODEL), lambda b, n: (b * steps_per_b + n, 0)), *job_out],
        compiler_params=pltpu.CompilerParams(vmem_limit_bytes=VMEM_LIMIT),
        name="attn_latent",
    )(qkv, *caches, dbias, sink_c, lam_b, gsub, bd128, *job_args)
    return outs[0], outs[1:]


MXU_TILE = 256
FFN_CHUNKS = ((0, 6 * MXU_TILE), (6 * MXU_TILE, D_FF))


def _post_kernel(x_ref, br_ref, gates_ref, gt1_ref, sh2_ref, sc2_ref, gt2_ref, g2_ref,
                 wb_ref, wo_ref, wfi_ref, wfo_ref, gf_ref, o_ref, *, final):
    merged = None
    for k in range(4):
        proj = _dot(br_ref[:, k * 256:(k + 1) * 256], wb_ref[k].astype(BF16))
        t = gates_ref[:, k * D_MODEL:(k + 1) * D_MODEL].astype(F32) * proj
        merged = t if merged is None else merged + t
    x1 = x_ref[...] + (0.5 * gt1_ref[...]) * _dot(merged.astype(BF16), wo_ref[...].astype(BF16))
    h2 = (_rms(x1, g2_ref[...], NORM_EPS) * (1.0 + sc2_ref[...]) + sh2_ref[...]).astype(BF16)
    acc = None
    for c0, c1 in FFN_CHUNKS:
        a = _dot(h2, wfi_ref[:, c0:c1])
        u = _dot(h2, wfi_ref[:, D_FF + c0:D_FF + c1])
        g = ((a * _sigmoid(a)) * u).astype(BF16)
        t = _dot(g, wfo_ref[c0:c1, :])
        acc = t if acc is None else acc + t
    xo = x1 + gt2_ref[...] * acc
    if final:
        xo = _rms(xo, gf_ref[...], NORM_EPS)
    o_ref[...] = xo


def _post_call(l, x, br, gates, mod, g2, wb, wo, wfi, wfo, gf, *, prompt, final):
    rows = x.shape[0]
    tm = 512
    if prompt:
        mod_row = lambda i: CTX_ROW
    else:
        mod_row = lambda i: i // (DEC_SEQ // tm)

    def mod_spec(chunk):
        return pl.BlockSpec((None, None, 1, D_MODEL), lambda i: (l, mod_row(i), 0, chunk))

    def resident(shape, layer):
        nd = len(shape)
        return pl.BlockSpec((None,) + shape, lambda i: (layer,) + (0,) * nd, pipeline_mode=pl.Buffered(1))

    return pl.pallas_call(
        functools.partial(_post_kernel, final=final),
        out_shape=jax.ShapeDtypeStruct((rows, D_MODEL), F32),
        grid=(rows // tm,),
        in_specs=[
            pl.BlockSpec((tm, D_MODEL), lambda i: (i, 0)),
            pl.BlockSpec((tm, D_MODEL), lambda i: (i, 0)),
            pl.BlockSpec((tm, GATE_W), lambda i: (i, 0)),
            mod_spec(2), mod_spec(3), mod_spec(4), mod_spec(5),
            pl.BlockSpec((None, 1, D_MODEL), lambda i: (l, 0, 0)),
            resident((4, 256, D_MODEL), l),
            resident((D_MODEL, D_MODEL), l),
            resident((D_MODEL, 2 * D_FF), 0),
            resident((D_FF, D_MODEL), 0),
            pl.BlockSpec((1, D_MODEL), lambda i: (0, 0)),
        ],
        out_specs=pl.BlockSpec((tm, D_MODEL), lambda i: (i, 0)),
        compiler_params=pltpu.CompilerParams(vmem_limit_bytes=VMEM_LIMIT),
        name="post_prompt" if prompt else "post_latent",
    )(x, br, gates, mod, mod, mod, mod, g2, wb, wo, wfi, wfo, gf)


def _rope_tables():
    t = jnp.arange(DEC_SEQ)
    row = (t // GRID_W).astype(F32)[:, None]
    col = (t % GRID_W).astype(F32)[:, None]
    tabs = []
    for d in (HEAD_DIM, B_HALF):
        quarter = d // 4
        inv = jnp.power(jnp.float32(ROPE_THETA), -jnp.arange(quarter, dtype=F32) / quarter)
        ar, ac = row * inv, col * inv
        cos = jnp.concatenate([jnp.cos(ar), jnp.cos(ar), jnp.cos(ac), jnp.cos(ac)], axis=-1)
        sin = jnp.concatenate([-jnp.sin(ar), jnp.sin(ar), -jnp.sin(ac), jnp.sin(ac)], axis=-1)
        reps = LANES // d
        tabs += [jnp.tile(cos, (1, reps)), jnp.tile(sin, (1, reps))]
    return tabs


def kernel(x_prompt, x_sample, cache_a_k, cache_a_v, cache_b_k, cache_b_v, cache_c_k, cache_c_v, cache_d_k, cache_d_v, c, c_ctx, w_ada, b_ada, g_norm1, w_in, g_q_a, g_k_a, lam_b, g_subln_b, sink_c, rpb_d, w_branch, w_out, g_norm2, w_ffn_in, w_ffn_out, g_final):
    w_in_l = w_in[0:1].astype(BF16)
    wfi_l = wfo_l = None

    def ffn_jobs(layer, chunks_in, chunks_out):
        return [_CastJob(w_ffn_in, layer, chunks_in), _CastJob(w_ffn_out, layer, chunks_out)]

    gq = jnp.tile(g_q_a, (1, 2)).reshape(DEPTH, 1, LANES)
    gk = jnp.tile(g_k_a, (1, 2)).reshape(DEPTH, 1, LANES)
    gsub = jnp.tile(g_subln_b, (1, 2)).reshape(DEPTH, 1, LANES)
    g1 = g_norm1.reshape(DEPTH, 1, D_MODEL)
    g2 = g_norm2.reshape(DEPTH, 1, D_MODEL)
    gf = g_final.reshape(1, D_MODEL)
    head_of_lane = jnp.arange(LANES) // HEAD_DIM
    bd128 = ((head_of_lane[:, None] == head_of_lane[None, :]).astype(F32) / HEAD_DIM).astype(BF16)
    rope_tabs = _rope_tables()
    caches = tuple(jnp.swapaxes(t, 3, 4) for t in (cache_a_k, cache_a_v, cache_b_k, cache_b_v,
                                                   cache_c_k, cache_c_v, cache_d_k, cache_d_v))

    cond = jnp.concatenate([c, c_ctx[None, :], jnp.zeros((MOD_ROWS - DEC_BATCH - 1, D_MODEL), F32)], axis=0)
    mod = _ada_call(cond, w_ada, b_ada).reshape(DEPTH, MOD_ROWS, 1, 6 * D_MODEL)
    dbias = _dbias_call(rpb_d)

    xp = x_prompt.reshape(BATCH * SEQ, D_MODEL)
    xs = x_sample.reshape(DEC_BATCH * DEC_SEQ, D_MODEL)
    new_kv = None
    for l in range(DEPTH):
        lam_init = 0.8 - 0.6 * math.exp(-0.3 * l)
        final = l == DEPTH - 1
        outs = _pre_call(l, xp, mod, g1, w_in_l, gq, gk, None, new_kv, prompt=True)
        qkv_p, gates_p, new_kv = outs[0], outs[1], outs[2:]
        jobs = ffn_jobs(0, 8, 8) if l == 0 else []
        br_p, cast = _attn_prompt_call(l, lam_init, qkv_p, sink_c, lam_b, gsub, bd128, jobs)
        if cast:
            wfi_l, wfo_l = cast
        xp = _post_call(l, xp, br_p, gates_p, mod, g2, w_branch, w_out, wfi_l, wfo_l, gf, prompt=True, final=final)

        qkv_s, gates_s = _pre_call(l, xs, mod, g1, w_in_l, gq, gk, rope_tabs, None, prompt=False)
        jobs = [_CastJob(w_in, l + 1, 32), *ffn_jobs(l + 1, 32, 16)] if l + 1 < DEPTH else []
        br_s, cast = _attn_latent_call(l, lam_init, qkv_s, caches, dbias, sink_c, lam_b, gsub, bd128, jobs)
        xs = _post_call(l, xs, br_s, gates_s, mod, g2, w_branch, w_out, wfi_l, wfo_l, gf, prompt=False, final=final)
        if cast:
            w_in_l, wfi_l, wfo_l = cast

    y_prompt = xp.reshape(BATCH, SEQ, D_MODEL)
    y_sample = xs.reshape(DEC_BATCH, DEC_SEQ, D_MODEL)
    return (y_prompt, y_sample, *(jnp.swapaxes(t, 3, 4) for t in new_kv))
```

```python
import functools
import math
from typing import NamedTuple

import numpy as np
import jax
import jax.numpy as jnp
from jax import lax
from jax.experimental import pallas as pl
from jax.experimental.pallas import tpu as pltpu

F32 = jnp.float32
BF16 = jnp.bfloat16

D_MODEL = 1024
BATCH = 32
SEQ = 256
DEPTH = 2
DEC_BATCH = 8
DEC_SEQ = 1024
PAST_LEN = 256
GRID_W = 64
HEAD_DIM = 64
B_HALF = HEAD_DIM // 2
C_WINDOW = 128
NA_ROWS = 8
NA_COLS = 16
D_FF = 2816
ROPE_THETA = 10000.0
NORM_EPS = 1e-6
SUBLN_EPS = 1e-5
NEG_INF = -1e30

GATE_W = 4 * D_MODEL
W_A, W_B, W_C, W_D, W_GATES = 0, 512, 1280, 1792, 2560
IN_COLS = W_GATES + GATE_W
MOD_ROWS = 16
CTX_ROW = DEC_BATCH
LANES = 128
ATT_BLK = 256
NA_BLK_ROWS = ATT_BLK // GRID_W
NA_WIN_ROWS = 12
NA_WIN = NA_WIN_ROWS * GRID_W
NA_WIN_START = (0, 0, 4, 4)
C_WIN = ATT_BLK + 2 * C_WINDOW
VMEM_LIMIT = 56 * 1024 * 1024

A_QE, A_QO, A_K, A_V = 0, 128, 256, 384
B_Q, B_K, B_V = 640, 896, 1152
C_QE, C_QO, C_K, C_V = 1664, 1792, 1920, 2048
D_Q, D_K, D_V = 2304, 2560, 2816
QKV_W = 3328
VAUG = 2 * LANES


def _dot(a, b):
    return jnp.dot(a, b, preferred_element_type=F32)


def _dot_nt(a, b):
    return lax.dot_general(a, b, (((1,), (1,)), ((), ())), preferred_element_type=F32)


def _sigmoid(x):
    return 0.5 * jnp.tanh(0.5 * x) + 0.5


def _rms(x, g, eps):
    ms = jnp.mean(x * x, axis=-1, keepdims=True)
    return x * lax.rsqrt(ms + eps) * g


def _head_rms_mxu(v, bd, g, eps):
    v2 = v * v
    hi = v2.astype(BF16)
    lo = (v2 - hi.astype(F32)).astype(BF16)
    ms = _dot(hi, bd) + _dot(lo, bd)
    return v * lax.rsqrt(ms + eps) * g


def _head_rms(v, g, eps):
    v2 = v * v
    left = _lane_mask(v.shape, 0, HEAD_DIM)
    s_left = jnp.sum(jnp.where(left, v2, 0.0), axis=-1, keepdims=True)
    s_right = jnp.sum(jnp.where(left, 0.0, v2), axis=-1, keepdims=True)
    ms = jnp.where(left, s_left, s_right) * (1.0 / HEAD_DIM)
    return v * lax.rsqrt(ms + eps) * g


def _lane_mask(shape, lo, hi):
    lane = lax.broadcasted_iota(jnp.int32, shape, 1)
    return (lane >= lo) & (lane < hi)


def _mask_q(qf, lo, hi):
    return jnp.where(_lane_mask(qf.shape, lo, hi), qf, 0.0).astype(BF16)


def _interleave_heads(lo, hi):
    left = _lane_mask(lo.shape, 0, HEAD_DIM)
    return (jnp.where(left, lo, pltpu.roll(hi, HEAD_DIM, 1)),
            jnp.where(left, pltpu.roll(lo, HEAD_DIM, 1), hi))


def _ada_kernel(cond_ref, w_ref, b_ref, o_ref):
    c = cond_ref[...]
    s = (c * _sigmoid(c)).astype(BF16)
    res = _dot(s, w_ref[...].astype(BF16)) + b_ref[...]
    for r in range(MOD_ROWS):
        o_ref[r] = res[r:r + 1]


def _ada_call(cond, w_ada, b_ada):
    tn = 1536
    return pl.pallas_call(
        _ada_kernel,
        out_shape=jax.ShapeDtypeStruct((DEPTH, MOD_ROWS, 1, 6 * D_MODEL), F32),
        grid=(DEPTH, 6 * D_MODEL // tn),
        in_specs=[
            pl.BlockSpec((MOD_ROWS, D_MODEL), lambda l, j: (0, 0)),
            pl.BlockSpec((None, D_MODEL, tn), lambda l, j: (l, 0, j)),
            pl.BlockSpec((None, 1, tn), lambda l, j: (l, 0, j)),
        ],
        out_specs=pl.BlockSpec((None, MOD_ROWS, 1, tn), lambda l, j: (l, 0, 0, j)),
        compiler_params=pltpu.CompilerParams(vmem_limit_bytes=VMEM_LIMIT),
        name="ada",
    )(cond, w_ada, b_ada.reshape(DEPTH, 1, 6 * D_MODEL))


def _dbias_kernel(*refs, n_jobs):
    rpb_ref, o_ref = refs[0], refs[1 + n_jobs]
    _run_cast_jobs(refs[1:1 + n_jobs], refs[2 + n_jobs:])
    n_dr, n_dc = 2 * NA_ROWS - 1, 2 * NA_COLS - 1
    base = (pl.program_id(0) * 4 + pl.program_id(1)) * (n_dr * n_dc)
    shape = (GRID_W, LANES)
    cq = lax.broadcasted_iota(jnp.int32, shape, 0)
    lane = lax.broadcasted_iota(jnp.int32, shape, 1)
    ck = lane & (GRID_W - 1)
    dc = jnp.clip(ck - cq, -(NA_COLS - 1), NA_COLS - 1) + (NA_COLS - 1)
    start_c = jnp.clip(cq - NA_COLS // 2, 0, GRID_W - NA_COLS)
    col_valid = (ck >= start_c) & (ck < start_c + NA_COLS)
    neg = jnp.full(shape, NEG_INF, F32)
    toeplitz = []
    for dr in range(n_dr):
        t = jnp.zeros(shape, F32)
        for m in range(n_dc):
            t = jnp.where(dc == m, rpb_ref[base + dr * n_dc + m], t)
        toeplitz.append(jnp.where(col_valid, t, neg))
    left = lane < GRID_W
    rows = DEC_SEQ // GRID_W
    for n in range(DEC_SEQ // ATT_BLK):
        for rq in range(NA_BLK_ROWS):
            r = NA_BLK_ROWS * n + rq
            start_r = min(max(r - NA_ROWS // 2, 0), rows - NA_ROWS)
            for jp in range(NA_WIN_ROWS // 2):
                pair = []
                for j in (2 * jp, 2 * jp + 1):
                    key_row = NA_WIN_START[n] + j
                    valid = start_r <= key_row < start_r + NA_ROWS
                    pair.append(toeplitz[key_row - r + NA_ROWS - 1] if valid else neg)
                tile = pair[0] if pair[0] is pair[1] else jnp.where(left, pair[0], pair[1])
                o_ref[n, rq * GRID_W:(rq + 1) * GRID_W, jp * LANES:(jp + 1) * LANES] = tile


def _dbias_call(rpb_d, jobs):
    nblk = DEC_SEQ // ATT_BLK
    job_in, job_args, job_out, job_shape = _cast_job_specs(jobs, DEPTH * 4, lambda l, h: l * 4 + h)
    outs = pl.pallas_call(
        functools.partial(_dbias_kernel, n_jobs=len(jobs)),
        out_shape=[jax.ShapeDtypeStruct((DEPTH, 4, nblk, ATT_BLK, NA_WIN), F32), *job_shape],
        grid=(DEPTH, 4),
        in_specs=[pl.BlockSpec(memory_space=pltpu.SMEM), *job_in],
        out_specs=[pl.BlockSpec((None, None, nblk, ATT_BLK, NA_WIN), lambda l, h: (l, h, 0, 0, 0)), *job_out],
        name="dbias",
    )(rpb_d.reshape(-1), *job_args)
    return outs[0], outs[1:]


def _swap_halves(v, half):
    lane = lax.broadcasted_iota(jnp.int32, v.shape, 1)
    up = pltpu.roll(v, LANES - half, 1)
    dn = pltpu.roll(v, half, 1)
    return jnp.where((lane & (2 * half - 1)) < half, up, dn)


def _rope(v, cos, sin, half):
    return v * cos + _swap_halves(v, half) * sin


def _pre_kernel(*refs, prompt, first):
    if prompt:
        (x_ref, sh_ref, sc_ref, g1_ref, w_ref, gq_ref, gk_ref) = refs[:7]
        (qkv_ref, gates_ref, ka_ref, va_ref, kb_ref, vb_ref, kc_ref, vc_ref, kd_ref, vd_ref) = refs[-10:]
    else:
        (x_ref, sh_ref, sc_ref, g1_ref, w_ref, gq_ref, gk_ref,
         ca_ref, sa_ref, cb_ref, sb_ref, qkv_ref, gates_ref) = refs

    h = (_rms(x_ref[...], g1_ref[...], NORM_EPS) * (1.0 + sc_ref[...]) + sh_ref[...]).astype(BF16)
    ones = jnp.ones((h.shape[0], LANES), BF16)

    def rope_a(v):
        return v if prompt else _rope(v, ca_ref[...], sa_ref[...], 16)

    def rope_b(v):
        return v if prompt else _rope(v, cb_ref[...], sb_ref[...], 8)

    def put(col, v):
        qkv_ref[:, col:col + v.shape[1]] = v.astype(BF16)

    def put_values(col, v):
        for c in range(v.shape[1] // LANES):
            put(col + c * VAUG, v[:, c * LANES:(c + 1) * LANES])
            put(col + c * VAUG + LANES, ones)

    def put_heads(ref, v):
        for bi in range(v.shape[0] // SEQ):
            for pr in range(v.shape[1] // LANES):
                t = v[bi * SEQ:(bi + 1) * SEQ, pr * LANES:(pr + 1) * LANES].T
                for hh in range(2):
                    piece = t[hh * HEAD_DIM:(hh + 1) * HEAD_DIM]
                    if first:
                        ref[bi, 0, 2 * pr + hh] = piece
                        for later in range(1, DEPTH):
                            ref[bi, later, 2 * pr + hh] = jnp.zeros_like(piece)
                    else:
                        ref[bi, 2 * pr + hh] = piece

    scale = HEAD_DIM ** -0.5
    acc_a = _dot(h, w_ref[:, W_A:W_A + 512])
    acc = _dot(h, w_ref[:, W_B:W_B + 768])
    scale_b = B_HALF ** -0.5
    for c in range(2):
        put(B_Q + c * 128, rope_b(acc[:, c * 128:(c + 1) * 128]) * scale_b)
        put(B_K + c * 128, rope_b(acc[:, 256 + c * 128:256 + (c + 1) * 128]))
    put_values(B_V, acc[:, 512:768])
    if prompt:
        put_heads(kb_ref, acc[:, 256:512])
        put_heads(vb_ref, acc[:, 512:768])
    acc = _dot(h, w_ref[:, W_C:W_C + 512])
    q_e, q_o = _interleave_heads(acc[:, 0:128], acc[:, 128:256])
    put(C_QE, rope_a(q_e) * scale)
    put(C_QO, rope_a(q_o) * scale)
    put(C_K, rope_a(acc[:, 256:384]))
    put_values(C_V, acc[:, 384:512])
    if prompt:
        put_heads(kc_ref, acc[:, 256:384])
        put_heads(vc_ref, acc[:, 384:512])
    acc = _dot(h, w_ref[:, W_D:W_D + 768])
    put(D_Q, acc[:, 0:256] * scale)
    put(D_K, acc[:, 256:512])
    put_values(D_V, acc[:, 512:768])
    if prompt:
        put_heads(kd_ref, acc[:, 256:512])
        put_heads(vd_ref, acc[:, 512:768])
    gq = gq_ref[...]
    q_e, q_o = _interleave_heads(acc_a[:, 0:128], acc_a[:, 128:256])
    put(A_QE, rope_a(_head_rms(q_e, gq, NORM_EPS)) * scale)
    put(A_QO, rope_a(_head_rms(q_o, gq, NORM_EPS)) * scale)
    k_a = _head_rms(acc_a[:, 256:384], gk_ref[...], NORM_EPS)
    put(A_K, rope_a(k_a))
    put_values(A_V, acc_a[:, 384:512])
    if prompt:
        put_heads(ka_ref, k_a)
        put_heads(va_ref, acc_a[:, 384:512])
    for j in range(GATE_W // 512):
        g = _dot(h, w_ref[:, W_GATES + j * 512:W_GATES + (j + 1) * 512])
        gates_ref[:, j * 512:(j + 1) * 512] = (jnp.tanh(0.5 * g) + 1.0).astype(BF16)


def _pre_call(l, x, mod, g1, w_in, gq, gk, rope_tabs, kv_prev, *, prompt):
    rows = x.shape[0]
    tm = 512
    tq = tm
    bpb = tm // SEQ
    aliases = {}
    if prompt:
        mod_row = lambda i: CTX_ROW
    else:
        mod_row = lambda i: i // (DEC_SEQ // tm)
    in_specs = [
        pl.BlockSpec((tm, D_MODEL), lambda i: (i, 0)),
        pl.BlockSpec((None, None, 1, D_MODEL), lambda i: (l, mod_row(i), 0, 0)),
        pl.BlockSpec((None, None, 1, D_MODEL), lambda i: (l, mod_row(i), 0, 1)),
        pl.BlockSpec((None, 1, D_MODEL), lambda i: (l, 0, 0)),
        pl.BlockSpec((None, D_MODEL, IN_COLS), lambda i: (0, 0, 0), pipeline_mode=pl.Buffered(1)),
        pl.BlockSpec((None, 1, LANES), lambda i: (l, 0, 0)),
        pl.BlockSpec((None, 1, LANES), lambda i: (l, 0, 0)),
    ]
    args = [x, mod, mod, g1, w_in, gq, gk]
    out_shape = [jax.ShapeDtypeStruct((rows, QKV_W), BF16), jax.ShapeDtypeStruct((rows, GATE_W), BF16)]
    out_specs = [pl.BlockSpec((tm, QKV_W), lambda i: (i, 0)), pl.BlockSpec((tm, GATE_W), lambda i: (i, 0))]
    if prompt:
        for j, nh in enumerate((2, 2, 4, 4, 2, 2, 4, 4)):
            out_shape.append(jax.ShapeDtypeStruct((BATCH, DEPTH, nh, HEAD_DIM, SEQ), F32))
            if kv_prev is None:
                out_specs.append(pl.BlockSpec((bpb, DEPTH, nh, HEAD_DIM, SEQ), lambda i: (i, 0, 0, 0, 0)))
            else:
                out_specs.append(pl.BlockSpec((bpb, None, nh, HEAD_DIM, SEQ), lambda i: (i, l, 0, 0, 0)))
                aliases[len(args)] = 2 + j
                in_specs.append(pl.BlockSpec(memory_space=pl.ANY))
                args.append(kv_prev[j])
    else:
        nt = DEC_SEQ // tq
        for t in rope_tabs:
            in_specs.append(pl.BlockSpec((tq, LANES), lambda i: (i % nt, 0)))
            args.append(t)
    return pl.pallas_call(
        functools.partial(_pre_kernel, prompt=prompt, first=kv_prev is None),
        out_shape=out_shape,
        grid=(rows // tm,),
        in_specs=in_specs,
        out_specs=out_specs,
        input_output_aliases=aliases,
        compiler_params=pltpu.CompilerParams(vmem_limit_bytes=VMEM_LIMIT),
        name="pre_prompt" if prompt else "pre_latent",
    )(*args)


class _CastJob(NamedTuple):
    src: jax.Array
    layer: int
    chunks: int


def _cast_job_specs(jobs, n_steps, linear_step):
    in_specs, args, out_specs, out_shape = [], [], [], []
    for job in jobs:
        _, rows, cols = job.src.shape
        chunk_rows = rows // job.chunks
        per = n_steps // job.chunks

        def chunk_map(*idx, layer, per=per):
            return (layer, linear_step(*idx) // per, 0)

        in_specs.append(pl.BlockSpec((None, chunk_rows, cols), functools.partial(chunk_map, layer=job.layer)))
        args.append(job.src)
        out_specs.append(pl.BlockSpec((None, chunk_rows, cols), functools.partial(chunk_map, layer=0)))
        out_shape.append(jax.ShapeDtypeStruct((1, rows, cols), BF16))
    return in_specs, args, out_specs, out_shape


def _run_cast_jobs(in_refs, out_refs):
    for i_ref, o_ref in zip(in_refs, out_refs, strict=True):
        o_ref[...] = i_ref[...].astype(BF16)


class _Transposed:
    def __init__(self, a):
        self.a = a


def _scores(qs, keys, biases):
    out = []
    for k, b in zip(keys, biases):
        s = _dot(qs, k.a) if isinstance(k, _Transposed) else _dot_nt(qs, k)
        out.append(s if b is None else s + b)
    return out


def _row_max(s_list, sink):
    m = None
    for s in s_list:
        mi = jnp.max(s, axis=-1, keepdims=True)
        m = mi if m is None else jnp.maximum(m, mi)
    return m if sink is None else jnp.maximum(m, sink)


def _softmax_pv(s_list, vaugs, sink=None):
    m = _row_max(s_list, sink)
    r = None
    for s, v in zip(s_list, vaugs):
        p = jnp.exp((s - m).astype(BF16))
        ri = _dot_nt(p, v.a) if isinstance(v, _Transposed) else _dot(p, v)
        r = ri if r is None else r + ri
    den = r[:, LANES:]
    if sink is not None:
        den = den + jnp.exp(sink - m)
    return r[:, :LANES] / den


def _gqa(qe, qo, keys, vaugs, biases, sinks):
    mq = qe.shape[0]
    oe = jnp.zeros((mq, LANES), F32)
    oo = jnp.zeros((mq, LANES), F32)
    for g in range(2):
        lo, hi = HEAD_DIM * g, HEAD_DIM * (g + 1)
        qs = jnp.concatenate([_mask_q(qe, lo, hi), _mask_q(qo, lo, hi)], axis=0)
        sink = None
        if sinks is not None:
            row = lax.broadcasted_iota(jnp.int32, (2 * mq, 1), 0)
            sink = jnp.where(row < mq, sinks[2 * g], sinks[2 * g + 1])
        r = _softmax_pv(_scores(qs, keys, biases), vaugs, sink)
        msk = _lane_mask((mq, LANES), lo, hi)
        oe = jnp.where(msk, r[:mq], oe)
        oo = jnp.where(msk, r[mq:], oo)
    return _interleave_heads(oe, oo)


def _diff(q, keys, vaug_fn, lam):
    mq = q.shape[0]
    out = [jnp.zeros((mq, LANES), F32), jnp.zeros((mq, LANES), F32)]
    for hd in range(4):
        lo = HEAD_DIM * hd
        qs = jnp.concatenate([_mask_q(q, lo, lo + B_HALF), _mask_q(q, lo + B_HALF, lo + HEAD_DIM)], axis=0)
        o = _softmax_pv(_scores(qs, keys, [None] * len(keys)), vaug_fn(hd // 2))
        o = o[:mq] - lam * o[mq:]
        plo = HEAD_DIM * (hd % 2)
        out[hd // 2] = jnp.where(_lane_mask(o.shape, plo, plo + HEAD_DIM), o, out[hd // 2])
    return out


def _mha(q, keys, vaug_fn, bias_fn):
    mq = q.shape[0]
    out = [jnp.zeros((mq, LANES), F32), jnp.zeros((mq, LANES), F32)]
    for hd in range(4):
        lo = HEAD_DIM * hd
        qs = _mask_q(q, lo, lo + HEAD_DIM)
        o = _softmax_pv(_scores(qs, keys, bias_fn(hd)), vaug_fn(hd // 2))
        plo = HEAD_DIM * (hd % 2)
        out[hd // 2] = jnp.where(_lane_mask(o.shape, plo, plo + HEAD_DIM), o, out[hd // 2])
    return out


def _lambda(lam_ref, lam_init):
    lp = lam_ref[...]
    a = jnp.sum(lp[0:1, :] * lp[1:2, :], axis=-1, keepdims=True)
    b = jnp.sum(lp[2:3, :] * lp[3:4, :], axis=-1, keepdims=True)
    return jnp.exp(a) - jnp.exp(b) + lam_init


def _store_branches(br_ref, oa, ob, oc, od, bd_ref, gsub_ref, lam_init):
    bd = bd_ref[...]
    ob = [_head_rms_mxu(o, bd, gsub_ref[...], SUBLN_EPS) * (1.0 - lam_init) for o in ob]
    for j, o in enumerate((*oa, *ob, *oc, *od)):
        br_ref[:, j * LANES:(j + 1) * LANES] = o.astype(BF16)


def _attn_prompt_kernel(*refs, l, lam_init, n_jobs):
    qkv_ref, sink_ref, lam_ref, gsub_ref, bd_ref = refs[:5]
    br_ref = refs[5 + n_jobs]
    _run_cast_jobs(refs[5:5 + n_jobs], refs[6 + n_jobs:])
    sinks = [sink_ref[l, i] for i in range(4)]
    lam = _lambda(lam_ref, lam_init)
    for bi in range(qkv_ref.shape[0] // SEQ):
        r0 = bi * SEQ

        def cols(c, w):
            return qkv_ref[r0:r0 + SEQ, c:c + w]

        def qcols(c, w):
            return cols(c, w).astype(F32)

        oa = _gqa(qcols(A_QE, 128), qcols(A_QO, 128), [cols(A_K, 128)], [cols(A_V, VAUG)], [None], None)
        ob = _diff(qcols(B_Q, 256), [cols(B_K, 256)], lambda pr: [cols(B_V + pr * VAUG, VAUG)], lam)
        oc = _gqa(qcols(C_QE, 128), qcols(C_QO, 128), [cols(C_K, 128)], [cols(C_V, VAUG)], [None], sinks)
        od = _mha(qcols(D_Q, 256), [cols(D_K, 256)], lambda pr: [cols(D_V + pr * VAUG, VAUG)], lambda hd: [None])
        _store_branches(br_ref.at[r0:r0 + SEQ], oa, ob, oc, od, bd_ref, gsub_ref, lam_init)


def _attn_prompt_call(l, lam_init, qkv, sink_c, lam_b, gsub, bd128, jobs):
    rows = qkv.shape[0]
    tm = 4 * SEQ
    job_in, job_args, job_out, job_shape = _cast_job_specs(jobs, rows // tm, lambda b: b)
    outs = pl.pallas_call(
        functools.partial(_attn_prompt_kernel, l=l, lam_init=lam_init, n_jobs=len(jobs)),
        out_shape=[jax.ShapeDtypeStruct((rows, D_MODEL), BF16), *job_shape],
        grid=(rows // tm,),
        in_specs=[
            pl.BlockSpec((tm, QKV_W), lambda b: (b, 0)),
            pl.BlockSpec(memory_space=pltpu.SMEM),
            pl.BlockSpec((None, 4, B_HALF), lambda b: (l, 0, 0)),
            pl.BlockSpec((None, 1, LANES), lambda b: (l, 0, 0)),
            pl.BlockSpec((LANES, LANES), lambda b: (0, 0)),
            *job_in,
        ],
        out_specs=[pl.BlockSpec((tm, D_MODEL), lambda b: (b, 0)), *job_out],
        compiler_params=pltpu.CompilerParams(vmem_limit_bytes=VMEM_LIMIT),
        name="attn_prompt",
    )(qkv, sink_c, lam_b, gsub, bd128, *job_args)
    return outs[0], outs[1:]


def _attn_latent_kernel(*refs, l, lam_init, n_jobs):
    (kv_ref, xak_ref, xav_ref, xbk_ref, xbv_ref, xck_ref, xcv_ref, xdk_ref, xdv_ref,
     bias_ref, sink_ref, lam_ref, gsub_ref, bd_ref) = refs[:14]
    br_ref = refs[14 + n_jobs]
    _run_cast_jobs(refs[14:14 + n_jobs], refs[15 + n_jobs:])
    sinks = [sink_ref[l, i] for i in range(4)]
    lam = _lambda(lam_ref, lam_init)
    nsub = br_ref.shape[0] // ATT_BLK
    for sub in range(nsub):
        n = pl.program_id(1) * nsub + sub
        q0 = pl.multiple_of(n * ATT_BLK, ATT_BLK)

        def q(c, w):
            return kv_ref[pl.ds(q0, ATT_BLK), c:c + w].astype(F32)

        def lat(c, w):
            return kv_ref[:, c:c + w]

        def ctx_k(ref):
            return _Transposed(ref[...].reshape(ref.shape[0] * HEAD_DIM, PAST_LEN).astype(BF16))

        def ctx_v(ref, pr):
            vt = ref[2 * pr:2 * pr + 2].reshape(LANES, PAST_LEN).astype(BF16)
            return _Transposed(jnp.concatenate([vt, jnp.ones((LANES, PAST_LEN), BF16)], axis=0))

        oa = _gqa(q(A_QE, 128), q(A_QO, 128), [ctx_k(xak_ref), lat(A_K, 128)], [ctx_v(xav_ref, 0), lat(A_V, VAUG)],
                  [None, None], None)
        ob = _diff(q(B_Q, 256), [ctx_k(xbk_ref), lat(B_K, 256)],
                   lambda pr: [ctx_v(xbv_ref, pr), lat(B_V + pr * VAUG, VAUG)], lam)
        start_c = pl.multiple_of(jnp.clip(q0 - C_WINDOW, 0, DEC_SEQ - C_WIN), C_WINDOW)
        rowq = lax.broadcasted_iota(jnp.int32, (2 * ATT_BLK, C_WIN), 0) & (ATT_BLK - 1)
        colk = lax.broadcasted_iota(jnp.int32, (2 * ATT_BLK, C_WIN), 1)
        band = jnp.where(jnp.abs(rowq - colk + (q0 - start_c)) <= C_WINDOW, 0.0, NEG_INF)
        oc = _gqa(q(C_QE, 128), q(C_QO, 128),
                  [ctx_k(xck_ref), kv_ref[pl.ds(start_c, C_WIN), C_K:C_K + 128]],
                  [ctx_v(xcv_ref, 0), kv_ref[pl.ds(start_c, C_WIN), C_V:C_V + VAUG]],
                  [None, band], sinks)
        start_d = pl.multiple_of(jnp.where(n >= 2, NA_WIN_START[2] * GRID_W, 0), ATT_BLK)
        kwin = kv_ref[pl.ds(start_d, NA_WIN), D_K:D_K + 256]
        od = _mha(q(D_Q, 256), [ctx_k(xdk_ref), kwin],
                  lambda pr: [ctx_v(xdv_ref, pr),
                              kv_ref[pl.ds(start_d, NA_WIN), D_V + pr * VAUG:D_V + (pr + 1) * VAUG]],
                  lambda hd: [None, bias_ref[hd, sub]])
        _store_branches(br_ref.at[sub * ATT_BLK:(sub + 1) * ATT_BLK], oa, ob, oc, od, bd_ref, gsub_ref, lam_init)


def _attn_latent_call(l, lam_init, qkv, caches, dbias, sink_c, lam_b, gsub, bd128, jobs):
    rows = qkv.shape[0]
    nblk = DEC_SEQ // ATT_BLK
    nsub = 1
    steps_per_b = nblk // nsub
    cache_specs = [pl.BlockSpec((None, None, t.shape[2], HEAD_DIM, PAST_LEN), lambda b, n: (b, l, 0, 0, 0))
                   for t in caches]
    job_in, job_args, job_out, job_shape = _cast_job_specs(jobs, DEC_BATCH * steps_per_b,
                                                           lambda b, n: b * steps_per_b + n)
    outs = pl.pallas_call(
        functools.partial(_attn_latent_kernel, l=l, lam_init=lam_init, n_jobs=len(jobs)),
        out_shape=[jax.ShapeDtypeStruct((rows, D_MODEL), BF16), *job_shape],
        grid=(DEC_BATCH, steps_per_b),
        in_specs=[
            pl.BlockSpec((DEC_SEQ, QKV_W), lambda b, n: (b, 0)),
            *cache_specs,
            pl.BlockSpec((None, 4, nsub, ATT_BLK, NA_WIN), lambda b, n: (l, 0, n, 0, 0)),
            pl.BlockSpec(memory_space=pltpu.SMEM),
            pl.BlockSpec((None, 4, B_HALF), lambda b, n: (l, 0, 0)),
            pl.BlockSpec((None, 1, LANES), lambda b, n: (l, 0, 0)),
            pl.BlockSpec((LANES, LANES), lambda b, n: (0, 0)),
            *job_in,
        ],
        out_specs=[pl.BlockSpec((nsub * ATT_BLK, D_MODEL), lambda b, n: (b * steps_per_b + n, 0)), *job_out],
        compiler_params=pltpu.CompilerParams(vmem_limit_bytes=VMEM_LIMIT),
        name="attn_latent",
    )(qkv, *caches, dbias, sink_c, lam_b, gsub, bd128, *job_args)
    return outs[0], outs[1:]


MXU_TILE = 256
FFN_CHUNKS = ((0, 6 * MXU_TILE), (6 * MXU_TILE, D_FF))


def _post_kernel(x_ref, br_ref, gates_ref, gt1_ref, sh2_ref, sc2_ref, gt2_ref, g2_ref,
                 wb_ref, wo_ref, wfi_ref, wfo_ref, gf_ref, o_ref, *, final):
    merged = None
    for k in range(4):
        proj = _dot(br_ref[:, k * 256:(k + 1) * 256], wb_ref[k].astype(BF16))
        t = gates_ref[:, k * D_MODEL:(k + 1) * D_MODEL].astype(F32) * proj
        merged = t if merged is None else merged + t
    x1 = x_ref[...] + (0.5 * gt1_ref[...]) * _dot(merged.astype(BF16), wo_ref[...].astype(BF16))
    h2 = (_rms(x1, g2_ref[...], NORM_EPS) * (1.0 + sc2_ref[...]) + sh2_ref[...]).astype(BF16)
    acc = None
    for c0, c1 in FFN_CHUNKS:
        a = _dot(h2, wfi_ref[:, c0:c1])
        u = _dot(h2, wfi_ref[:, D_FF + c0:D_FF + c1])
        g = ((a * _sigmoid(a)) * u).astype(BF16)
        t = _dot(g, wfo_ref[c0:c1, :])
        acc = t if acc is None else acc + t
    xo = x1 + gt2_ref[...] * acc
    if final:
        xo = _rms(xo, gf_ref[...], NORM_EPS)
    o_ref[...] = xo


def _post_call(l, x, br, gates, mod, g2, wb, wo, wfi, wfo, gf, *, prompt, final):
    rows = x.shape[0]
    tm = 512
    if prompt:
        mod_row = lambda i: CTX_ROW
    else:
        mod_row = lambda i: i // (DEC_SEQ // tm)

    def mod_spec(chunk):
        return pl.BlockSpec((None, None, 1, D_MODEL), lambda i: (l, mod_row(i), 0, chunk))

    def resident(shape, layer):
        nd = len(shape)
        return pl.BlockSpec((None,) + shape, lambda i: (layer,) + (0,) * nd, pipeline_mode=pl.Buffered(1))

    return pl.pallas_call(
        functools.partial(_post_kernel, final=final),
        out_shape=jax.ShapeDtypeStruct((rows, D_MODEL), F32),
        grid=(rows // tm,),
        in_specs=[
            pl.BlockSpec((tm, D_MODEL), lambda i: (i, 0)),
            pl.BlockSpec((tm, D_MODEL), lambda i: (i, 0)),
            pl.BlockSpec((tm, GATE_W), lambda i: (i, 0)),
            mod_spec(2), mod_spec(3), mod_spec(4), mod_spec(5),
            pl.BlockSpec((None, 1, D_MODEL), lambda i: (l, 0, 0)),
            resident((4, 256, D_MODEL), l),
            resident((D_MODEL, D_MODEL), l),
            resident((D_MODEL, 2 * D_FF), 0),
            resident((D_FF, D_MODEL), 0),
            pl.BlockSpec((1, D_MODEL), lambda i: (0, 0)),
        ],
        out_specs=pl.BlockSpec((tm, D_MODEL), lambda i: (i, 0)),
        compiler_params=pltpu.CompilerParams(vmem_limit_bytes=VMEM_LIMIT),
        name="post_prompt" if prompt else "post_latent",
    )(x, br, gates, mod, mod, mod, mod, g2, wb, wo, wfi, wfo, gf)


def _rope_tables():
    t = np.arange(DEC_SEQ)
    row = (t // GRID_W).astype(np.float32)[:, None]
    col = (t % GRID_W).astype(np.float32)[:, None]
    tabs = []
    for d in (HEAD_DIM, B_HALF):
        quarter = d // 4
        inv = np.power(np.float32(ROPE_THETA), -np.arange(quarter, dtype=np.float32) / np.float32(quarter))
        ar, ac = row * inv, col * inv
        cos = np.concatenate([np.cos(ar), np.cos(ar), np.cos(ac), np.cos(ac)], axis=-1)
        sin = np.concatenate([-np.sin(ar), np.sin(ar), -np.sin(ac), np.sin(ac)], axis=-1)
        reps = LANES // d
        tabs += [jnp.asarray(np.tile(cos, (1, reps)), F32), jnp.asarray(np.tile(sin, (1, reps)), F32)]
    return tabs


def kernel(x_prompt, x_sample, cache_a_k, cache_a_v, cache_b_k, cache_b_v, cache_c_k, cache_c_v, cache_d_k, cache_d_v, c, c_ctx, w_ada, b_ada, g_norm1, w_in, g_q_a, g_k_a, lam_b, g_subln_b, sink_c, rpb_d, w_branch, w_out, g_norm2, w_ffn_in, w_ffn_out, g_final):
    wfi_l = wfo_l = None

    def ffn_jobs(layer, chunks_in, chunks_out):
        return [_CastJob(w_ffn_in, layer, chunks_in), _CastJob(w_ffn_out, layer, chunks_out)]

    gq = jnp.tile(g_q_a, (1, 2)).reshape(DEPTH, 1, LANES)
    gk = jnp.tile(g_k_a, (1, 2)).reshape(DEPTH, 1, LANES)
    gsub = jnp.tile(g_subln_b, (1, 2)).reshape(DEPTH, 1, LANES)
    g1 = g_norm1.reshape(DEPTH, 1, D_MODEL)
    g2 = g_norm2.reshape(DEPTH, 1, D_MODEL)
    gf = g_final.reshape(1, D_MODEL)
    head_of_lane = np.arange(LANES) // HEAD_DIM
    bd128 = jnp.asarray((head_of_lane[:, None] == head_of_lane[None, :]).astype(np.float32) / HEAD_DIM, BF16)
    rope_tabs = _rope_tables()
    caches = tuple(jnp.swapaxes(t, 3, 4) for t in (cache_a_k, cache_a_v, cache_b_k, cache_b_v,
                                                   cache_c_k, cache_c_v, cache_d_k, cache_d_v))

    cond = jnp.concatenate([c, c_ctx[None, :], jnp.zeros((MOD_ROWS - DEC_BATCH - 1, D_MODEL), F32)], axis=0)
    mod = _ada_call(cond, w_ada, b_ada)
    dbias, (w_in_l,) = _dbias_call(rpb_d, [_CastJob(w_in, 0, DEPTH * 4)])

    xp = x_prompt.reshape(BATCH * SEQ, D_MODEL)
    xs = x_sample.reshape(DEC_BATCH * DEC_SEQ, D_MODEL)
    new_kv = None
    for l in range(DEPTH):
        lam_init = 0.8 - 0.6 * math.exp(-0.3 * l)
        final = l == DEPTH - 1
        outs = _pre_call(l, xp, mod, g1, w_in_l, gq, gk, None, new_kv, prompt=True)
        qkv_p, gates_p, new_kv = outs[0], outs[1], outs[2:]
        jobs = ffn_jobs(0, 8, 8) if l == 0 else []
        br_p, cast = _attn_prompt_call(l, lam_init, qkv_p, sink_c, lam_b, gsub, bd128, jobs)
        if cast:
            wfi_l, wfo_l = cast
        xp = _post_call(l, xp, br_p, gates_p, mod, g2, w_branch, w_out, wfi_l, wfo_l, gf, prompt=True, final=final)

        qkv_s, gates_s = _pre_call(l, xs, mod, g1, w_in_l, gq, gk, rope_tabs, None, prompt=False)
        jobs = [_CastJob(w_in, l + 1, 32), *ffn_jobs(l + 1, 32, 16)] if l + 1 < DEPTH else []
        br_s, cast = _attn_latent_call(l, lam_init, qkv_s, caches, dbias, sink_c, lam_b, gsub, bd128, jobs)
        xs = _post_call(l, xs, br_s, gates_s, mod, g2, w_branch, w_out, wfi_l, wfo_l, gf, prompt=False, final=final)
        if cast:
            w_in_l, wfi_l, wfo_l = cast

    y_prompt = xp.reshape(BATCH, SEQ, D_MODEL)
    y_sample = xs.reshape(DEC_BATCH, DEC_SEQ, D_MODEL)
    return (y_prompt, y_sample, *(jnp.swapaxes(t, 3, 4) for t in new_kv))
```

```python
import functools
import math
from typing import NamedTuple

import numpy as np
import jax
import jax.numpy as jnp
from jax import lax
from jax.experimental import pallas as pl
from jax.experimental.pallas import tpu as pltpu

F32 = jnp.float32
BF16 = jnp.bfloat16

D_MODEL = 1024
BATCH = 32
SEQ = 256
DEPTH = 2
DEC_BATCH = 8
DEC_SEQ = 1024
PAST_LEN = 256
GRID_W = 64
HEAD_DIM = 64
B_HALF = HEAD_DIM // 2
C_WINDOW = 128
NA_ROWS = 8
NA_COLS = 16
D_FF = 2816
ROPE_THETA = 10000.0
NORM_EPS = 1e-6
SUBLN_EPS = 1e-5
NEG_INF = -1e30

GATE_W = 4 * D_MODEL
W_A, W_B, W_C, W_D, W_GATES = 0, 512, 1280, 1792, 2560
IN_COLS = W_GATES + GATE_W
MOD_ROWS = 16
CTX_ROW = DEC_BATCH
LANES = 128
ATT_BLK = 256
NA_BLK_ROWS = ATT_BLK // GRID_W
NA_WIN_ROWS = 12
NA_WIN = NA_WIN_ROWS * GRID_W
NA_WIN_START = (0, 0, 4, 4)
C_WIN = ATT_BLK + 2 * C_WINDOW
VMEM_LIMIT = 56 * 1024 * 1024

A_QE, A_QO, A_K, A_V = 0, 128, 256, 384
B_Q, B_K, B_V = 640, 896, 1152
C_QE, C_QO, C_K, C_V = 1664, 1792, 1920, 2048
D_Q, D_K, D_V = 2304, 2560, 2816
QKV_W = 3328
VAUG = 2 * LANES


def _dot(a, b):
    return jnp.dot(a, b, preferred_element_type=F32)


def _dot_nt(a, b):
    return lax.dot_general(a, b, (((1,), (1,)), ((), ())), preferred_element_type=F32)


def _sigmoid(x):
    return 0.5 * jnp.tanh(0.5 * x) + 0.5


def _rms(x, g, eps):
    ms = jnp.mean(x * x, axis=-1, keepdims=True)
    return x * lax.rsqrt(ms + eps) * g


def _head_rms_mxu(v, bd, g, eps):
    v2 = v * v
    hi = v2.astype(BF16)
    lo = (v2 - hi.astype(F32)).astype(BF16)
    ms = _dot(hi, bd) + _dot(lo, bd)
    return v * lax.rsqrt(ms + eps) * g


def _head_rms(v, g, eps):
    v2 = v * v
    left = _lane_mask(v.shape, 0, HEAD_DIM)
    s_left = jnp.sum(jnp.where(left, v2, 0.0), axis=-1, keepdims=True)
    s_right = jnp.sum(jnp.where(left, 0.0, v2), axis=-1, keepdims=True)
    ms = jnp.where(left, s_left, s_right) * (1.0 / HEAD_DIM)
    return v * lax.rsqrt(ms + eps) * g


def _lane_mask(shape, lo, hi):
    lane = lax.broadcasted_iota(jnp.int32, shape, 1)
    return (lane >= lo) & (lane < hi)


def _mask_q(qf, lo, hi):
    return jnp.where(_lane_mask(qf.shape, lo, hi), qf, 0.0).astype(BF16)


def _interleave_heads(lo, hi):
    left = _lane_mask(lo.shape, 0, HEAD_DIM)
    return (jnp.where(left, lo, pltpu.roll(hi, HEAD_DIM, 1)),
            jnp.where(left, pltpu.roll(lo, HEAD_DIM, 1), hi))


def _ada_kernel(cond_ref, w_ref, b_ref, o_ref):
    c = cond_ref[...]
    s = (c * _sigmoid(c)).astype(BF16)
    res = _dot(s, w_ref[...].astype(BF16)) + b_ref[...]
    for r in range(MOD_ROWS):
        o_ref[r] = res[r:r + 1]


def _ada_call(cond, w_ada, b_ada):
    tn = 1536
    return pl.pallas_call(
        _ada_kernel,
        out_shape=jax.ShapeDtypeStruct((DEPTH, MOD_ROWS, 1, 6 * D_MODEL), F32),
        grid=(DEPTH, 6 * D_MODEL // tn),
        in_specs=[
            pl.BlockSpec((MOD_ROWS, D_MODEL), lambda l, j: (0, 0)),
            pl.BlockSpec((None, D_MODEL, tn), lambda l, j: (l, 0, j)),
            pl.BlockSpec((None, 1, tn), lambda l, j: (l, 0, j)),
        ],
        out_specs=pl.BlockSpec((None, MOD_ROWS, 1, tn), lambda l, j: (l, 0, 0, j)),
        compiler_params=pltpu.CompilerParams(vmem_limit_bytes=VMEM_LIMIT),
        name="ada",
    )(cond, w_ada, b_ada.reshape(DEPTH, 1, 6 * D_MODEL))


def _dbias_kernel(*refs, n_jobs):
    rpb_ref, o_ref = refs[0], refs[1 + n_jobs]
    _run_cast_jobs(refs[1:1 + n_jobs], refs[2 + n_jobs:])
    n_dr, n_dc = 2 * NA_ROWS - 1, 2 * NA_COLS - 1
    base = (pl.program_id(0) * 4 + pl.program_id(1)) * (n_dr * n_dc)
    shape = (GRID_W, LANES)
    cq = lax.broadcasted_iota(jnp.int32, shape, 0)
    lane = lax.broadcasted_iota(jnp.int32, shape, 1)
    ck = lane & (GRID_W - 1)
    dc = jnp.clip(ck - cq, -(NA_COLS - 1), NA_COLS - 1) + (NA_COLS - 1)
    start_c = jnp.clip(cq - NA_COLS // 2, 0, GRID_W - NA_COLS)
    col_valid = (ck >= start_c) & (ck < start_c + NA_COLS)
    neg = jnp.full(shape, NEG_INF, F32)
    toeplitz = []
    for dr in range(n_dr):
        t = jnp.zeros(shape, F32)
        for m in range(n_dc):
            t = jnp.where(dc == m, rpb_ref[base + dr * n_dc + m], t)
        toeplitz.append(jnp.where(col_valid, t, neg))
    left = lane < GRID_W
    rows = DEC_SEQ // GRID_W
    for n in range(DEC_SEQ // ATT_BLK):
        for rq in range(NA_BLK_ROWS):
            r = NA_BLK_ROWS * n + rq
            start_r = min(max(r - NA_ROWS // 2, 0), rows - NA_ROWS)
            for jp in range(NA_WIN_ROWS // 2):
                pair = []
                for j in (2 * jp, 2 * jp + 1):
                    key_row = NA_WIN_START[n] + j
                    valid = start_r <= key_row < start_r + NA_ROWS
                    pair.append(toeplitz[key_row - r + NA_ROWS - 1] if valid else neg)
                tile = pair[0] if pair[0] is pair[1] else jnp.where(left, pair[0], pair[1])
                o_ref[n, rq * GRID_W:(rq + 1) * GRID_W, jp * LANES:(jp + 1) * LANES] = tile


def _dbias_call(rpb_d, jobs):
    nblk = DEC_SEQ // ATT_BLK
    job_in, job_args, job_out, job_shape = _cast_job_specs(jobs, DEPTH * 4, lambda l, h: l * 4 + h)
    outs = pl.pallas_call(
        functools.partial(_dbias_kernel, n_jobs=len(jobs)),
        out_shape=[jax.ShapeDtypeStruct((DEPTH, 4, nblk, ATT_BLK, NA_WIN), F32), *job_shape],
        grid=(DEPTH, 4),
        in_specs=[pl.BlockSpec(memory_space=pltpu.SMEM), *job_in],
        out_specs=[pl.BlockSpec((None, None, nblk, ATT_BLK, NA_WIN), lambda l, h: (l, h, 0, 0, 0)), *job_out],
        name="dbias",
    )(rpb_d.reshape(-1), *job_args)
    return outs[0], outs[1:]


def _swap_halves(v, half):
    lane = lax.broadcasted_iota(jnp.int32, v.shape, 1)
    up = pltpu.roll(v, LANES - half, 1)
    dn = pltpu.roll(v, half, 1)
    return jnp.where((lane & (2 * half - 1)) < half, up, dn)


def _rope(v, cos, sin, half):
    return v * cos + _swap_halves(v, half) * sin


def _pre_kernel(*refs, prompt, first):
    if prompt:
        (x_ref, sh_ref, sc_ref, g1_ref, w_ref, gq_ref, gk_ref) = refs[:7]
        (qkv_ref, gates_ref, ka_ref, va_ref, kb_ref, vb_ref, kc_ref, vc_ref, kd_ref, vd_ref) = refs[-10:]
    else:
        (x_ref, sh_ref, sc_ref, g1_ref, w_ref, gq_ref, gk_ref,
         ca_ref, sa_ref, cb_ref, sb_ref, qkv_ref, gates_ref) = refs

    h = (_rms(x_ref[...], g1_ref[...], NORM_EPS) * (1.0 + sc_ref[...]) + sh_ref[...]).astype(BF16)
    ones = jnp.ones((h.shape[0], LANES), BF16)

    def rope_a(v):
        return v if prompt else _rope(v, ca_ref[...], sa_ref[...], 16)

    def rope_b(v):
        return v if prompt else _rope(v, cb_ref[...], sb_ref[...], 8)

    def put(col, v):
        qkv_ref[:, col:col + v.shape[1]] = v.astype(BF16)

    def put_values(col, v):
        for c in range(v.shape[1] // LANES):
            put(col + c * VAUG, v[:, c * LANES:(c + 1) * LANES])
            put(col + c * VAUG + LANES, ones)

    def put_heads(ref, v):
        for bi in range(v.shape[0] // SEQ):
            for pr in range(v.shape[1] // LANES):
                t = v[bi * SEQ:(bi + 1) * SEQ, pr * LANES:(pr + 1) * LANES].T
                for hh in range(2):
                    piece = t[hh * HEAD_DIM:(hh + 1) * HEAD_DIM]
                    if first:
                        ref[bi, 0, 2 * pr + hh] = piece
                        for later in range(1, DEPTH):
                            ref[bi, later, 2 * pr + hh] = jnp.zeros_like(piece)
                    else:
                        ref[bi, 2 * pr + hh] = piece

    scale = HEAD_DIM ** -0.5
    acc_a = _dot(h, w_ref[:, W_A:W_A + 512])
    acc = _dot(h, w_ref[:, W_B:W_B + 768])
    scale_b = B_HALF ** -0.5
    for c in range(2):
        put(B_Q + c * 128, rope_b(acc[:, c * 128:(c + 1) * 128]) * scale_b)
        put(B_K + c * 128, rope_b(acc[:, 256 + c * 128:256 + (c + 1) * 128]))
    put_values(B_V, acc[:, 512:768])
    if prompt:
        put_heads(kb_ref, acc[:, 256:512])
        put_heads(vb_ref, acc[:, 512:768])
    acc = _dot(h, w_ref[:, W_C:W_C + 512])
    q_e, q_o = _interleave_heads(acc[:, 0:128], acc[:, 128:256])
    put(C_QE, rope_a(q_e) * scale)
    put(C_QO, rope_a(q_o) * scale)
    put(C_K, rope_a(acc[:, 256:384]))
    put_values(C_V, acc[:, 384:512])
    if prompt:
        put_heads(kc_ref, acc[:, 256:384])
        put_heads(vc_ref, acc[:, 384:512])
    acc = _dot(h, w_ref[:, W_D:W_D + 768])
    put(D_Q, acc[:, 0:256] * scale)
    put(D_K, acc[:, 256:512])
    put_values(D_V, acc[:, 512:768])
    if prompt:
        put_heads(kd_ref, acc[:, 256:512])
        put_heads(vd_ref, acc[:, 512:768])
    gq = gq_ref[...]
    q_e, q_o = _interleave_heads(acc_a[:, 0:128], acc_a[:, 128:256])
    put(A_QE, rope_a(_head_rms(q_e, gq, NORM_EPS)) * scale)
    put(A_QO, rope_a(_head_rms(q_o, gq, NORM_EPS)) * scale)
    k_a = _head_rms(acc_a[:, 256:384], gk_ref[...], NORM_EPS)
    put(A_K, rope_a(k_a))
    put_values(A_V, acc_a[:, 384:512])
    if prompt:
        put_heads(ka_ref, k_a)
        put_heads(va_ref, acc_a[:, 384:512])
    for j in range(GATE_W // 512):
        g = _dot(h, w_ref[:, W_GATES + j * 512:W_GATES + (j + 1) * 512])
        gates_ref[:, j * 512:(j + 1) * 512] = (jnp.tanh(0.5 * g) + 1.0).astype(BF16)


def _pre_call(l, x, mod, g1, w_in, gq, gk, rope_tabs, kv_prev, *, prompt):
    rows = x.shape[0]
    tm = 512
    tq = tm
    bpb = tm // SEQ
    aliases = {}
    if prompt:
        mod_row = lambda i: CTX_ROW
    else:
        mod_row = lambda i: i // (DEC_SEQ // tm)
    in_specs = [
        pl.BlockSpec((tm, D_MODEL), lambda i: (i, 0)),
        pl.BlockSpec((None, None, 1, D_MODEL), lambda i: (l, mod_row(i), 0, 0)),
        pl.BlockSpec((None, None, 1, D_MODEL), lambda i: (l, mod_row(i), 0, 1)),
        pl.BlockSpec((None, 1, D_MODEL), lambda i: (l, 0, 0)),
        pl.BlockSpec((None, D_MODEL, IN_COLS), lambda i: (0, 0, 0), pipeline_mode=pl.Buffered(1)),
        pl.BlockSpec((None, 1, LANES), lambda i: (l, 0, 0)),
        pl.BlockSpec((None, 1, LANES), lambda i: (l, 0, 0)),
    ]
    args = [x, mod, mod, g1, w_in, gq, gk]
    out_shape = [jax.ShapeDtypeStruct((rows, QKV_W), BF16), jax.ShapeDtypeStruct((rows, GATE_W), BF16)]
    out_specs = [pl.BlockSpec((tm, QKV_W), lambda i: (i, 0)), pl.BlockSpec((tm, GATE_W), lambda i: (i, 0))]
    if prompt:
        for j, nh in enumerate((2, 2, 4, 4, 2, 2, 4, 4)):
            out_shape.append(jax.ShapeDtypeStruct((BATCH, DEPTH, nh, HEAD_DIM, SEQ), F32))
            if kv_prev is None:
                out_specs.append(pl.BlockSpec((bpb, DEPTH, nh, HEAD_DIM, SEQ), lambda i: (i, 0, 0, 0, 0)))
            else:
                out_specs.append(pl.BlockSpec((bpb, None, nh, HEAD_DIM, SEQ), lambda i: (i, l, 0, 0, 0)))
                aliases[len(args)] = 2 + j
                in_specs.append(pl.BlockSpec(memory_space=pl.ANY))
                args.append(kv_prev[j])
    else:
        nt = DEC_SEQ // tq
        for t in rope_tabs:
            in_specs.append(pl.BlockSpec((tq, LANES), lambda i: (i % nt, 0)))
            args.append(t)
    return pl.pallas_call(
        functools.partial(_pre_kernel, prompt=prompt, first=kv_prev is None),
        out_shape=out_shape,
        grid=(rows // tm,),
        in_specs=in_specs,
        out_specs=out_specs,
        input_output_aliases=aliases,
        compiler_params=pltpu.CompilerParams(vmem_limit_bytes=VMEM_LIMIT),
        name="pre_prompt" if prompt else "pre_latent",
    )(*args)


class _CastJob(NamedTuple):
    src: jax.Array
    layer: int
    chunks: int


def _cast_job_specs(jobs, n_steps, linear_step):
    in_specs, args, out_specs, out_shape = [], [], [], []
    for job in jobs:
        _, rows, cols = job.src.shape
        chunk_rows = rows // job.chunks
        per = n_steps // job.chunks

        def chunk_map(*idx, layer, per=per):
            return (layer, linear_step(*idx) // per, 0)

        in_specs.append(pl.BlockSpec((None, chunk_rows, cols), functools.partial(chunk_map, layer=job.layer)))
        args.append(job.src)
        out_specs.append(pl.BlockSpec((None, chunk_rows, cols), functools.partial(chunk_map, layer=0)))
        out_shape.append(jax.ShapeDtypeStruct((1, rows, cols), BF16))
    return in_specs, args, out_specs, out_shape


def _run_cast_jobs(in_refs, out_refs):
    for i_ref, o_ref in zip(in_refs, out_refs, strict=True):
        o_ref[...] = i_ref[...].astype(BF16)


class _Transposed:
    def __init__(self, a):
        self.a = a


def _scores(qs, keys, biases):
    out = []
    for k, b in zip(keys, biases):
        s = _dot(qs, k.a) if isinstance(k, _Transposed) else _dot_nt(qs, k)
        out.append(s if b is None else s + b)
    return out


def _row_max(s_list, sink):
    m = None
    for s in s_list:
        mi = jnp.max(s, axis=-1, keepdims=True)
        m = mi if m is None else jnp.maximum(m, mi)
    return m if sink is None else jnp.maximum(m, sink)


def _softmax_pv(s_list, vaugs, sink=None):
    m = _row_max(s_list, sink)
    r = None
    for s, v in zip(s_list, vaugs):
        p = jnp.exp((s - m).astype(BF16))
        ri = _dot_nt(p, v.a) if isinstance(v, _Transposed) else _dot(p, v)
        r = ri if r is None else r + ri
    den = r[:, LANES:]
    if sink is not None:
        den = den + jnp.exp(sink - m)
    return r[:, :LANES] / den


def _gqa(qe, qo, keys, vaugs, biases, sinks):
    mq = qe.shape[0]
    oe = jnp.zeros((mq, LANES), F32)
    oo = jnp.zeros((mq, LANES), F32)
    for g in range(2):
        lo, hi = HEAD_DIM * g, HEAD_DIM * (g + 1)
        qs = jnp.concatenate([_mask_q(qe, lo, hi), _mask_q(qo, lo, hi)], axis=0)
        sink = None
        if sinks is not None:
            row = lax.broadcasted_iota(jnp.int32, (2 * mq, 1), 0)
            sink = jnp.where(row < mq, sinks[2 * g], sinks[2 * g + 1])
        r = _softmax_pv(_scores(qs, keys, biases), vaugs, sink)
        msk = _lane_mask((mq, LANES), lo, hi)
        oe = jnp.where(msk, r[:mq], oe)
        oo = jnp.where(msk, r[mq:], oo)
    return _interleave_heads(oe, oo)


def _diff(q, keys, vaug_fn, lam):
    mq = q.shape[0]
    out = [jnp.zeros((mq, LANES), F32), jnp.zeros((mq, LANES), F32)]
    for hd in range(4):
        lo = HEAD_DIM * hd
        qs = jnp.concatenate([_mask_q(q, lo, lo + B_HALF), _mask_q(q, lo + B_HALF, lo + HEAD_DIM)], axis=0)
        o = _softmax_pv(_scores(qs, keys, [None] * len(keys)), vaug_fn(hd // 2))
        o = o[:mq] - lam * o[mq:]
        plo = HEAD_DIM * (hd % 2)
        out[hd // 2] = jnp.where(_lane_mask(o.shape, plo, plo + HEAD_DIM), o, out[hd // 2])
    return out


def _mha(q, keys, vaug_fn, bias_fn):
    mq = q.shape[0]
    out = [jnp.zeros((mq, LANES), F32), jnp.zeros((mq, LANES), F32)]
    for hd in range(4):
        lo = HEAD_DIM * hd
        qs = _mask_q(q, lo, lo + HEAD_DIM)
        o = _softmax_pv(_scores(qs, keys, bias_fn(hd)), vaug_fn(hd // 2))
        plo = HEAD_DIM * (hd % 2)
        out[hd // 2] = jnp.where(_lane_mask(o.shape, plo, plo + HEAD_DIM), o, out[hd // 2])
    return out


def _lambda(lam_ref, lam_init):
    lp = lam_ref[...]
    a = jnp.sum(lp[0:1, :] * lp[1:2, :], axis=-1, keepdims=True)
    b = jnp.sum(lp[2:3, :] * lp[3:4, :], axis=-1, keepdims=True)
    return jnp.exp(a) - jnp.exp(b) + lam_init


def _store_branches(br_ref, oa, ob, oc, od, bd_ref, gsub_ref, lam_init):
    bd = bd_ref[...]
    ob = [_head_rms_mxu(o, bd, gsub_ref[...], SUBLN_EPS) * (1.0 - lam_init) for o in ob]
    for j, o in enumerate((*oa, *ob, *oc, *od)):
        br_ref[:, j * LANES:(j + 1) * LANES] = o.astype(BF16)


def _attn_prompt_kernel(*refs, l, lam_init, n_jobs):
    qkv_ref, sink_ref, lam_ref, gsub_ref, bd_ref = refs[:5]
    br_ref = refs[5 + n_jobs]
    _run_cast_jobs(refs[5:5 + n_jobs], refs[6 + n_jobs:])
    sinks = [sink_ref[l, i] for i in range(4)]
    lam = _lambda(lam_ref, lam_init)
    for bi in range(qkv_ref.shape[0] // SEQ):
        r0 = bi * SEQ

        def cols(c, w):
            return qkv_ref[r0:r0 + SEQ, c:c + w]

        def qcols(c, w):
            return cols(c, w).astype(F32)

        oa = _gqa(qcols(A_QE, 128), qcols(A_QO, 128), [cols(A_K, 128)], [cols(A_V, VAUG)], [None], None)
        ob = _diff(qcols(B_Q, 256), [cols(B_K, 256)], lambda pr: [cols(B_V + pr * VAUG, VAUG)], lam)
        oc = _gqa(qcols(C_QE, 128), qcols(C_QO, 128), [cols(C_K, 128)], [cols(C_V, VAUG)], [None], sinks)
        od = _mha(qcols(D_Q, 256), [cols(D_K, 256)], lambda pr: [cols(D_V + pr * VAUG, VAUG)], lambda hd: [None])
        _store_branches(br_ref.at[r0:r0 + SEQ], oa, ob, oc, od, bd_ref, gsub_ref, lam_init)


def _attn_prompt_call(l, lam_init, qkv, sink_c, lam_b, gsub, bd128, jobs):
    rows = qkv.shape[0]
    tm = 4 * SEQ
    job_in, job_args, job_out, job_shape = _cast_job_specs(jobs, rows // tm, lambda b: b)
    outs = pl.pallas_call(
        functools.partial(_attn_prompt_kernel, l=l, lam_init=lam_init, n_jobs=len(jobs)),
        out_shape=[jax.ShapeDtypeStruct((rows, D_MODEL), BF16), *job_shape],
        grid=(rows // tm,),
        in_specs=[
            pl.BlockSpec((tm, QKV_W), lambda b: (b, 0)),
            pl.BlockSpec(memory_space=pltpu.SMEM),
            pl.BlockSpec((None, 4, B_HALF), lambda b: (l, 0, 0)),
            pl.BlockSpec((None, 1, LANES), lambda b: (l, 0, 0)),
            pl.BlockSpec((LANES, LANES), lambda b: (0, 0)),
            *job_in,
        ],
        out_specs=[pl.BlockSpec((tm, D_MODEL), lambda b: (b, 0)), *job_out],
        compiler_params=pltpu.CompilerParams(vmem_limit_bytes=VMEM_LIMIT),
        name="attn_prompt",
    )(qkv, sink_c, lam_b, gsub, bd128, *job_args)
    return outs[0], outs[1:]


def _attn_latent_kernel(*refs, l, lam_init, n_jobs):
    (kv_ref, xak_ref, xav_ref, xbk_ref, xbv_ref, xck_ref, xcv_ref, xdk_ref, xdv_ref,
     bias_ref, sink_ref, lam_ref, gsub_ref, bd_ref) = refs[:14]
    br_ref = refs[14 + n_jobs]
    _run_cast_jobs(refs[14:14 + n_jobs], refs[15 + n_jobs:])
    sinks = [sink_ref[l, i] for i in range(4)]
    lam = _lambda(lam_ref, lam_init)
    nsub = br_ref.shape[0] // ATT_BLK
    for sub in range(nsub):
        n = pl.program_id(1) * nsub + sub
        q0 = pl.multiple_of(n * ATT_BLK, ATT_BLK)

        def q(c, w):
            return kv_ref[pl.ds(q0, ATT_BLK), c:c + w].astype(F32)

        def lat(c, w):
            return kv_ref[:, c:c + w]

        def ctx_k(ref):
            return _Transposed(ref[...].reshape(ref.shape[0] * HEAD_DIM, PAST_LEN).astype(BF16))

        def ctx_v(ref, pr):
            vt = ref[2 * pr:2 * pr + 2].reshape(LANES, PAST_LEN).astype(BF16)
            return _Transposed(jnp.concatenate([vt, jnp.ones((LANES, PAST_LEN), BF16)], axis=0))

        oa = _gqa(q(A_QE, 128), q(A_QO, 128), [ctx_k(xak_ref), lat(A_K, 128)], [ctx_v(xav_ref, 0), lat(A_V, VAUG)],
                  [None, None], None)
        ob = _diff(q(B_Q, 256), [ctx_k(xbk_ref), lat(B_K, 256)],
                   lambda pr: [ctx_v(xbv_ref, pr), lat(B_V + pr * VAUG, VAUG)], lam)
        start_c = pl.multiple_of(jnp.clip(q0 - C_WINDOW, 0, DEC_SEQ - C_WIN), C_WINDOW)
        rowq = lax.broadcasted_iota(jnp.int32, (2 * ATT_BLK, C_WIN), 0) & (ATT_BLK - 1)
        colk = lax.broadcasted_iota(jnp.int32, (2 * ATT_BLK, C_WIN), 1)
        band = jnp.where(jnp.abs(rowq - colk + (q0 - start_c)) <= C_WINDOW, 0.0, NEG_INF)
        oc = _gqa(q(C_QE, 128), q(C_QO, 128),
                  [ctx_k(xck_ref), kv_ref[pl.ds(start_c, C_WIN), C_K:C_K + 128]],
                  [ctx_v(xcv_ref, 0), kv_ref[pl.ds(start_c, C_WIN), C_V:C_V + VAUG]],
                  [None, band], sinks)
        start_d = pl.multiple_of(jnp.where(n >= 2, NA_WIN_START[2] * GRID_W, 0), ATT_BLK)
        kwin = kv_ref[pl.ds(start_d, NA_WIN), D_K:D_K + 256]
        od = _mha(q(D_Q, 256), [ctx_k(xdk_ref), kwin],
                  lambda pr: [ctx_v(xdv_ref, pr),
                              kv_ref[pl.ds(start_d, NA_WIN), D_V + pr * VAUG:D_V + (pr + 1) * VAUG]],
                  lambda hd: [None, bias_ref[hd, sub]])
        _store_branches(br_ref.at[sub * ATT_BLK:(sub + 1) * ATT_BLK], oa, ob, oc, od, bd_ref, gsub_ref, lam_init)


def _attn_latent_call(l, lam_init, qkv, caches, dbias, sink_c, lam_b, gsub, bd128, jobs):
    rows = qkv.shape[0]
    nblk = DEC_SEQ // ATT_BLK
    nsub = 2
    steps_per_b = nblk // nsub
    cache_specs = [pl.BlockSpec((None, None, t.shape[2], HEAD_DIM, PAST_LEN), lambda b, n: (b, l, 0, 0, 0))
                   for t in caches]
    job_in, job_args, job_out, job_shape = _cast_job_specs(jobs, DEC_BATCH * steps_per_b,
                                                           lambda b, n: b * steps_per_b + n)
    outs = pl.pallas_call(
        functools.partial(_attn_latent_kernel, l=l, lam_init=lam_init, n_jobs=len(jobs)),
        out_shape=[jax.ShapeDtypeStruct((rows, D_MODEL), BF16), *job_shape],
        grid=(DEC_BATCH, steps_per_b),
        in_specs=[
            pl.BlockSpec((DEC_SEQ, QKV_W), lambda b, n: (b, 0)),
            *cache_specs,
            pl.BlockSpec((None, 4, nsub, ATT_BLK, NA_WIN), lambda b, n: (l, 0, n, 0, 0)),
            pl.BlockSpec(memory_space=pltpu.SMEM),
            pl.BlockSpec((None, 4, B_HALF), lambda b, n: (l, 0, 0)),
            pl.BlockSpec((None, 1, LANES), lambda b, n: (l, 0, 0)),
            pl.BlockSpec((LANES, LANES), lambda b, n: (0, 0)),
            *job_in,
        ],
        out_specs=[pl.BlockSpec((nsub * ATT_BLK, D_MODEL), lambda b, n: (b * steps_per_b + n, 0)), *job_out],
        compiler_params=pltpu.CompilerParams(vmem_limit_bytes=VMEM_LIMIT),
        name="attn_latent",
    )(qkv, *caches, dbias, sink_c, lam_b, gsub, bd128, *job_args)
    return outs[0], outs[1:]


MXU_TILE = 256
FFN_CHUNKS = ((0, 6 * MXU_TILE), (6 * MXU_TILE, D_FF))


def _post_kernel(x_ref, br_ref, gates_ref, gt1_ref, sh2_ref, sc2_ref, gt2_ref, g2_ref,
                 wb_ref, wo_ref, wfi_ref, wfo_ref, gf_ref, o_ref, *, final):
    merged = None
    for k in range(4):
        proj = _dot(br_ref[:, k * 256:(k + 1) * 256], wb_ref[k].astype(BF16))
        t = gates_ref[:, k * D_MODEL:(k + 1) * D_MODEL].astype(F32) * proj
        merged = t if merged is None else merged + t
    x1 = x_ref[...] + (0.5 * gt1_ref[...]) * _dot(merged.astype(BF16), wo_ref[...].astype(BF16))
    h2 = (_rms(x1, g2_ref[...], NORM_EPS) * (1.0 + sc2_ref[...]) + sh2_ref[...]).astype(BF16)
    acc = None
    for c0, c1 in FFN_CHUNKS:
        a = _dot(h2, wfi_ref[:, c0:c1])
        u = _dot(h2, wfi_ref[:, D_FF + c0:D_FF + c1])
        g = ((a * _sigmoid(a)) * u).astype(BF16)
        t = _dot(g, wfo_ref[c0:c1, :])
        acc = t if acc is None else acc + t
    xo = x1 + gt2_ref[...] * acc
    if final:
        xo = _rms(xo, gf_ref[...], NORM_EPS)
    o_ref[...] = xo


def _post_call(l, x, br, gates, mod, g2, wb, wo, wfi, wfo, gf, *, prompt, final):
    rows = x.shape[0]
    tm = 512
    if prompt:
        mod_row = lambda i: CTX_ROW
    else:
        mod_row = lambda i: i // (DEC_SEQ // tm)

    def mod_spec(chunk):
        return pl.BlockSpec((None, None, 1, D_MODEL), lambda i: (l, mod_row(i), 0, chunk))

    def resident(shape, layer):
        nd = len(shape)
        return pl.BlockSpec((None,) + shape, lambda i: (layer,) + (0,) * nd, pipeline_mode=pl.Buffered(1))

    return pl.pallas_call(
        functools.partial(_post_kernel, final=final),
        out_shape=jax.ShapeDtypeStruct((rows, D_MODEL), F32),
        grid=(rows // tm,),
        in_specs=[
            pl.BlockSpec((tm, D_MODEL), lambda i: (i, 0)),
            pl.BlockSpec((tm, D_MODEL), lambda i: (i, 0)),
            pl.BlockSpec((tm, GATE_W), lambda i: (i, 0)),
            mod_spec(2), mod_spec(3), mod_spec(4), mod_spec(5),
            pl.BlockSpec((None, 1, D_MODEL), lambda i: (l, 0, 0)),
            resident((4, 256, D_MODEL), l),
            resident((D_MODEL, D_MODEL), l),
            resident((D_MODEL, 2 * D_FF), 0),
            resident((D_FF, D_MODEL), 0),
            pl.BlockSpec((1, D_MODEL), lambda i: (0, 0)),
        ],
        out_specs=pl.BlockSpec((tm, D_MODEL), lambda i: (i, 0)),
        compiler_params=pltpu.CompilerParams(vmem_limit_bytes=VMEM_LIMIT),
        name="post_prompt" if prompt else "post_latent",
    )(x, br, gates, mod, mod, mod, mod, g2, wb, wo, wfi, wfo, gf)


def _rope_tables():
    t = np.arange(DEC_SEQ)
    row = (t // GRID_W).astype(np.float32)[:, None]
    col = (t % GRID_W).astype(np.float32)[:, None]
    tabs = []
    for d in (HEAD_DIM, B_HALF):
        quarter = d // 4
        inv = np.power(np.float32(ROPE_THETA), -np.arange(quarter, dtype=np.float32) / np.float32(quarter))
        ar, ac = row * inv, col * inv
        cos = np.concatenate([np.cos(ar), np.cos(ar), np.cos(ac), np.cos(ac)], axis=-1)
        sin = np.concatenate([-np.sin(ar), np.sin(ar), -np.sin(ac), np.sin(ac)], axis=-1)
        reps = LANES // d
        tabs += [jnp.asarray(np.tile(cos, (1, reps)), F32), jnp.asarray(np.tile(sin, (1, reps)), F32)]
    return tabs


def kernel(x_prompt, x_sample, cache_a_k, cache_a_v, cache_b_k, cache_b_v, cache_c_k, cache_c_v, cache_d_k, cache_d_v, c, c_ctx, w_ada, b_ada, g_norm1, w_in, g_q_a, g_k_a, lam_b, g_subln_b, sink_c, rpb_d, w_branch, w_out, g_norm2, w_ffn_in, w_ffn_out, g_final):
    wfi_l = wfo_l = None

    def ffn_jobs(layer, chunks_in, chunks_out):
        return [_CastJob(w_ffn_in, layer, chunks_in), _CastJob(w_ffn_out, layer, chunks_out)]

    gq = jnp.tile(g_q_a, (1, 2)).reshape(DEPTH, 1, LANES)
    gk = jnp.tile(g_k_a, (1, 2)).reshape(DEPTH, 1, LANES)
    gsub = jnp.tile(g_subln_b, (1, 2)).reshape(DEPTH, 1, LANES)
    g1 = g_norm1.reshape(DEPTH, 1, D_MODEL)
    g2 = g_norm2.reshape(DEPTH, 1, D_MODEL)
    gf = g_final.reshape(1, D_MODEL)
    head_of_lane = np.arange(LANES) // HEAD_DIM
    bd128 = jnp.asarray((head_of_lane[:, None] == head_of_lane[None, :]).astype(np.float32) / HEAD_DIM, BF16)
    rope_tabs = _rope_tables()
    caches = tuple(jnp.swapaxes(t, 3, 4) for t in (cache_a_k, cache_a_v, cache_b_k, cache_b_v,
                                                   cache_c_k, cache_c_v, cache_d_k, cache_d_v))

    cond = jnp.concatenate([c, c_ctx[None, :], jnp.zeros((MOD_ROWS - DEC_BATCH - 1, D_MODEL), F32)], axis=0)
    mod = _ada_call(cond, w_ada, b_ada)
    dbias, (w_in_l,) = _dbias_call(rpb_d, [_CastJob(w_in, 0, DEPTH * 4)])

    xp = x_prompt.reshape(BATCH * SEQ, D_MODEL)
    xs = x_sample.reshape(DEC_BATCH * DEC_SEQ, D_MODEL)
    new_kv = None
    for l in range(DEPTH):
        lam_init = 0.8 - 0.6 * math.exp(-0.3 * l)
        final = l == DEPTH - 1
        outs = _pre_call(l, xp, mod, g1, w_in_l, gq, gk, None, new_kv, prompt=True)
        qkv_p, gates_p, new_kv = outs[0], outs[1], outs[2:]
        jobs = ffn_jobs(0, 8, 8) if l == 0 else []
        br_p, cast = _attn_prompt_call(l, lam_init, qkv_p, sink_c, lam_b, gsub, bd128, jobs)
        if cast:
            wfi_l, wfo_l = cast
        xp = _post_call(l, xp, br_p, gates_p, mod, g2, w_branch, w_out, wfi_l, wfo_l, gf, prompt=True, final=final)

        qkv_s, gates_s = _pre_call(l, xs, mod, g1, w_in_l, gq, gk, rope_tabs, None, prompt=False)
        jobs = [_CastJob(w_in, l + 1, 16), *ffn_jobs(l + 1, 16, 16)] if l + 1 < DEPTH else []
        br_s, cast = _attn_latent_call(l, lam_init, qkv_s, caches, dbias, sink_c, lam_b, gsub, bd128, jobs)
        xs = _post_call(l, xs, br_s, gates_s, mod, g2, w_branch, w_out, wfi_l, wfo_l, gf, prompt=False, final=final)
        if cast:
            w_in_l, wfi_l, wfo_l = cast

    y_prompt = xp.reshape(BATCH, SEQ, D_MODEL)
    y_sample = xs.reshape(DEC_BATCH, DEC_SEQ, D_MODEL)
    return (y_prompt, y_sample, *(jnp.swapaxes(t, 3, 4) for t in new_kv))
```

```python
import functools
import math
from typing import NamedTuple

import numpy as np
import jax
import jax.numpy as jnp
from jax import lax
from jax.experimental import pallas as pl
from jax.experimental.pallas import tpu as pltpu

F32 = jnp.float32
BF16 = jnp.bfloat16

D_MODEL = 1024
BATCH = 32
SEQ = 256
DEPTH = 2
DEC_BATCH = 8
DEC_SEQ = 1024
PAST_LEN = 256
GRID_W = 64
HEAD_DIM = 64
B_HALF = HEAD_DIM // 2
C_WINDOW = 128
NA_ROWS = 8
NA_COLS = 16
D_FF = 2816
ROPE_THETA = 10000.0
NORM_EPS = 1e-6
SUBLN_EPS = 1e-5
NEG_INF = -1e30

GATE_W = 4 * D_MODEL
W_A, W_B, W_C, W_D, W_GATES = 0, 512, 1280, 1792, 2560
IN_COLS = W_GATES + GATE_W
MOD_ROWS = 16
CTX_ROW = DEC_BATCH
LANES = 128
ATT_BLK = 256
NA_BLK_ROWS = ATT_BLK // GRID_W
NA_WIN_ROWS = 12
NA_WIN = NA_WIN_ROWS * GRID_W
NA_WIN_START = (0, 0, 4, 4)
C_WIN = ATT_BLK + 2 * C_WINDOW
VMEM_LIMIT = 56 * 1024 * 1024

A_QE, A_QO, A_K, A_V = 0, 128, 256, 384
B_Q, B_K, B_V = 640, 896, 1152
C_QE, C_QO, C_K, C_V = 1664, 1792, 1920, 2048
D_Q, D_K, D_V = 2304, 2560, 2816
QKV_W = 3328
VAUG = 2 * LANES


def _dot(a, b):
    return jnp.dot(a, b, preferred_element_type=F32)


def _dot_nt(a, b):
    return lax.dot_general(a, b, (((1,), (1,)), ((), ())), preferred_element_type=F32)


def _sigmoid(x):
    return 0.5 * jnp.tanh(0.5 * x) + 0.5


def _rms(x, g, eps):
    ms = jnp.mean(x * x, axis=-1, keepdims=True)
    return x * lax.rsqrt(ms + eps) * g


def _head_rms_mxu(v, bd, g, eps):
    v2 = v * v
    hi = v2.astype(BF16)
    lo = (v2 - hi.astype(F32)).astype(BF16)
    ms = _dot(hi, bd) + _dot(lo, bd)
    return v * lax.rsqrt(ms + eps) * g


def _head_rms(v, g, eps):
    v2 = v * v
    left = _lane_mask(v.shape, 0, HEAD_DIM)
    s_left = jnp.sum(jnp.where(left, v2, 0.0), axis=-1, keepdims=True)
    s_right = jnp.sum(jnp.where(left, 0.0, v2), axis=-1, keepdims=True)
    ms = jnp.where(left, s_left, s_right) * (1.0 / HEAD_DIM)
    return v * lax.rsqrt(ms + eps) * g


def _lane_mask(shape, lo, hi):
    lane = lax.broadcasted_iota(jnp.int32, shape, 1)
    return (lane >= lo) & (lane < hi)


def _mask_q(qf, lo, hi):
    return jnp.where(_lane_mask(qf.shape, lo, hi), qf, 0.0).astype(BF16)


def _interleave_heads(lo, hi):
    left = _lane_mask(lo.shape, 0, HEAD_DIM)
    return (jnp.where(left, lo, pltpu.roll(hi, HEAD_DIM, 1)),
            jnp.where(left, pltpu.roll(lo, HEAD_DIM, 1), hi))


def _ada_kernel(cond_ref, w_ref, b_ref, o_ref):
    c = cond_ref[...]
    s = (c * _sigmoid(c)).astype(BF16)
    res = _dot(s, w_ref[...].astype(BF16)) + b_ref[...]
    for r in range(MOD_ROWS):
        o_ref[r] = res[r:r + 1]


def _ada_call(cond, w_ada, b_ada):
    tn = 1536
    return pl.pallas_call(
        _ada_kernel,
        out_shape=jax.ShapeDtypeStruct((DEPTH, MOD_ROWS, 1, 6 * D_MODEL), F32),
        grid=(DEPTH, 6 * D_MODEL // tn),
        in_specs=[
            pl.BlockSpec((MOD_ROWS, D_MODEL), lambda l, j: (0, 0)),
            pl.BlockSpec((None, D_MODEL, tn), lambda l, j: (l, 0, j)),
            pl.BlockSpec((None, 1, tn), lambda l, j: (l, 0, j)),
        ],
        out_specs=pl.BlockSpec((None, MOD_ROWS, 1, tn), lambda l, j: (l, 0, 0, j)),
        compiler_params=pltpu.CompilerParams(vmem_limit_bytes=VMEM_LIMIT),
        name="ada",
    )(cond, w_ada, b_ada.reshape(DEPTH, 1, 6 * D_MODEL))


def _dbias_kernel(*refs, n_jobs):
    rpb_ref, o_ref = refs[0], refs[1 + n_jobs]
    _run_cast_jobs(refs[1:1 + n_jobs], refs[2 + n_jobs:])
    n_dr, n_dc = 2 * NA_ROWS - 1, 2 * NA_COLS - 1
    base = (pl.program_id(0) * 4 + pl.program_id(1)) * (n_dr * n_dc)
    shape = (GRID_W, LANES)
    cq = lax.broadcasted_iota(jnp.int32, shape, 0)
    lane = lax.broadcasted_iota(jnp.int32, shape, 1)
    ck = lane & (GRID_W - 1)
    dc = jnp.clip(ck - cq, -(NA_COLS - 1), NA_COLS - 1) + (NA_COLS - 1)
    start_c = jnp.clip(cq - NA_COLS // 2, 0, GRID_W - NA_COLS)
    col_valid = (ck >= start_c) & (ck < start_c + NA_COLS)
    neg = jnp.full(shape, NEG_INF, F32)
    toeplitz = []
    for dr in range(n_dr):
        t = jnp.zeros(shape, F32)
        for m in range(n_dc):
            t = jnp.where(dc == m, rpb_ref[base + dr * n_dc + m], t)
        toeplitz.append(jnp.where(col_valid, t, neg))
    left = lane < GRID_W
    rows = DEC_SEQ // GRID_W
    for n in range(DEC_SEQ // ATT_BLK):
        for rq in range(NA_BLK_ROWS):
            r = NA_BLK_ROWS * n + rq
            start_r = min(max(r - NA_ROWS // 2, 0), rows - NA_ROWS)
            for jp in range(NA_WIN_ROWS // 2):
                pair = []
                for j in (2 * jp, 2 * jp + 1):
                    key_row = NA_WIN_START[n] + j
                    valid = start_r <= key_row < start_r + NA_ROWS
                    pair.append(toeplitz[key_row - r + NA_ROWS - 1] if valid else neg)
                tile = pair[0] if pair[0] is pair[1] else jnp.where(left, pair[0], pair[1])
                o_ref[n, rq * GRID_W:(rq + 1) * GRID_W, jp * LANES:(jp + 1) * LANES] = tile


def _dbias_call(rpb_d, jobs):
    nblk = DEC_SEQ // ATT_BLK
    job_in, job_args, job_out, job_shape = _cast_job_specs(jobs, DEPTH * 4, lambda l, h: l * 4 + h)
    outs = pl.pallas_call(
        functools.partial(_dbias_kernel, n_jobs=len(jobs)),
        out_shape=[jax.ShapeDtypeStruct((DEPTH, 4, nblk, ATT_BLK, NA_WIN), F32), *job_shape],
        grid=(DEPTH, 4),
        in_specs=[pl.BlockSpec(memory_space=pltpu.SMEM), *job_in],
        out_specs=[pl.BlockSpec((None, None, nblk, ATT_BLK, NA_WIN), lambda l, h: (l, h, 0, 0, 0)), *job_out],
        name="dbias",
    )(rpb_d.reshape(-1), *job_args)
    return outs[0], outs[1:]


def _swap_halves(v, half):
    lane = lax.broadcasted_iota(jnp.int32, v.shape, 1)
    up = pltpu.roll(v, LANES - half, 1)
    dn = pltpu.roll(v, half, 1)
    return jnp.where((lane & (2 * half - 1)) < half, up, dn)


def _rope(v, cos, sin, half):
    return v * cos + _swap_halves(v, half) * sin


def _pre_kernel(*refs, prompt, first):
    if prompt:
        (x_ref, sh_ref, sc_ref, g1_ref, w_ref, gq_ref, gk_ref) = refs[:7]
        (qkv_ref, gates_ref, ka_ref, va_ref, kb_ref, vb_ref, kc_ref, vc_ref, kd_ref, vd_ref) = refs[-10:]
    else:
        (x_ref, sh_ref, sc_ref, g1_ref, w_ref, gq_ref, gk_ref,
         ca_ref, sa_ref, cb_ref, sb_ref, qkv_ref, gates_ref) = refs

    h = (_rms(x_ref[...], g1_ref[...], NORM_EPS) * (1.0 + sc_ref[...]) + sh_ref[...]).astype(BF16)
    ones = jnp.ones((h.shape[0], LANES), BF16)

    def rope_a(v):
        return v if prompt else _rope(v, ca_ref[...], sa_ref[...], 16)

    def rope_b(v):
        return v if prompt else _rope(v, cb_ref[...], sb_ref[...], 8)

    def put(col, v):
        qkv_ref[:, col:col + v.shape[1]] = v.astype(BF16)

    def put_values(col, v):
        for c in range(v.shape[1] // LANES):
            put(col + c * VAUG, v[:, c * LANES:(c + 1) * LANES])
            put(col + c * VAUG + LANES, ones)

    def put_heads(ref, v):
        for bi in range(v.shape[0] // SEQ):
            for pr in range(v.shape[1] // LANES):
                t = v[bi * SEQ:(bi + 1) * SEQ, pr * LANES:(pr + 1) * LANES].T
                for hh in range(2):
                    piece = t[hh * HEAD_DIM:(hh + 1) * HEAD_DIM]
                    if first:
                        ref[bi, 0, 2 * pr + hh] = piece
                        for later in range(1, DEPTH):
                            ref[bi, later, 2 * pr + hh] = jnp.zeros_like(piece)
                    else:
                        ref[bi, 2 * pr + hh] = piece

    scale = HEAD_DIM ** -0.5
    acc_a = _dot(h, w_ref[:, W_A:W_A + 512])
    acc = _dot(h, w_ref[:, W_B:W_B + 768])
    scale_b = B_HALF ** -0.5
    for c in range(2):
        put(B_Q + c * 128, rope_b(acc[:, c * 128:(c + 1) * 128]) * scale_b)
        put(B_K + c * 128, rope_b(acc[:, 256 + c * 128:256 + (c + 1) * 128]))
    put_values(B_V, acc[:, 512:768])
    if prompt:
        put_heads(kb_ref, acc[:, 256:512])
        put_heads(vb_ref, acc[:, 512:768])
    acc = _dot(h, w_ref[:, W_C:W_C + 512])
    q_e, q_o = _interleave_heads(acc[:, 0:128], acc[:, 128:256])
    put(C_QE, rope_a(q_e) * scale)
    put(C_QO, rope_a(q_o) * scale)
    put(C_K, rope_a(acc[:, 256:384]))
    put_values(C_V, acc[:, 384:512])
    if prompt:
        put_heads(kc_ref, acc[:, 256:384])
        put_heads(vc_ref, acc[:, 384:512])
    acc = _dot(h, w_ref[:, W_D:W_D + 768])
    put(D_Q, acc[:, 0:256] * scale)
    put(D_K, acc[:, 256:512])
    put_values(D_V, acc[:, 512:768])
    if prompt:
        put_heads(kd_ref, acc[:, 256:512])
        put_heads(vd_ref, acc[:, 512:768])
    gq = gq_ref[...]
    q_e, q_o = _interleave_heads(acc_a[:, 0:128], acc_a[:, 128:256])
    put(A_QE, rope_a(_head_rms(q_e, gq, NORM_EPS)) * scale)
    put(A_QO, rope_a(_head_rms(q_o, gq, NORM_EPS)) * scale)
    k_a = _head_rms(acc_a[:, 256:384], gk_ref[...], NORM_EPS)
    put(A_K, rope_a(k_a))
    put_values(A_V, acc_a[:, 384:512])
    if prompt:
        put_heads(ka_ref, k_a)
        put_heads(va_ref, acc_a[:, 384:512])
    for j in range(GATE_W // 512):
        g = _dot(h, w_ref[:, W_GATES + j * 512:W_GATES + (j + 1) * 512])
        gates_ref[:, j * 512:(j + 1) * 512] = (jnp.tanh(0.5 * g) + 1.0).astype(BF16)


def _pre_call(l, x, mod, g1, w_in, gq, gk, rope_tabs, kv_prev, *, prompt):
    rows = x.shape[0]
    tm = 512
    tq = tm
    bpb = tm // SEQ
    aliases = {}
    if prompt:
        mod_row = lambda i: CTX_ROW
    else:
        mod_row = lambda i: i // (DEC_SEQ // tm)
    in_specs = [
        pl.BlockSpec((tm, D_MODEL), lambda i: (i, 0)),
        pl.BlockSpec((None, None, 1, D_MODEL), lambda i: (l, mod_row(i), 0, 0)),
        pl.BlockSpec((None, None, 1, D_MODEL), lambda i: (l, mod_row(i), 0, 1)),
        pl.BlockSpec((None, 1, D_MODEL), lambda i: (l, 0, 0)),
        pl.BlockSpec((None, D_MODEL, IN_COLS), lambda i: (0, 0, 0), pipeline_mode=pl.Buffered(1)),
        pl.BlockSpec((None, 1, LANES), lambda i: (l, 0, 0)),
        pl.BlockSpec((None, 1, LANES), lambda i: (l, 0, 0)),
    ]
    args = [x, mod, mod, g1, w_in, gq, gk]
    out_shape = [jax.ShapeDtypeStruct((rows, QKV_W), BF16), jax.ShapeDtypeStruct((rows, GATE_W), BF16)]
    out_specs = [pl.BlockSpec((tm, QKV_W), lambda i: (i, 0)), pl.BlockSpec((tm, GATE_W), lambda i: (i, 0))]
    if prompt:
        for j, nh in enumerate((2, 2, 4, 4, 2, 2, 4, 4)):
            out_shape.append(jax.ShapeDtypeStruct((BATCH, DEPTH, nh, HEAD_DIM, SEQ), F32))
            if kv_prev is None:
                out_specs.append(pl.BlockSpec((bpb, DEPTH, nh, HEAD_DIM, SEQ), lambda i: (i, 0, 0, 0, 0)))
            else:
                out_specs.append(pl.BlockSpec((bpb, None, nh, HEAD_DIM, SEQ), lambda i: (i, l, 0, 0, 0)))
                aliases[len(args)] = 2 + j
                in_specs.append(pl.BlockSpec(memory_space=pl.ANY))
                args.append(kv_prev[j])
    else:
        nt = DEC_SEQ // tq
        for t in rope_tabs:
            in_specs.append(pl.BlockSpec((tq, LANES), lambda i: (i % nt, 0)))
            args.append(t)
    return pl.pallas_call(
        functools.partial(_pre_kernel, prompt=prompt, first=kv_prev is None),
        out_shape=out_shape,
        grid=(rows // tm,),
        in_specs=in_specs,
        out_specs=out_specs,
        input_output_aliases=aliases,
        compiler_params=pltpu.CompilerParams(vmem_limit_bytes=VMEM_LIMIT),
        name="pre_prompt" if prompt else "pre_latent",
    )(*args)


class _CastJob(NamedTuple):
    src: jax.Array
    layer: int
    chunks: int


def _cast_job_specs(jobs, n_steps, linear_step):
    in_specs, args, out_specs, out_shape = [], [], [], []
    for job in jobs:
        _, rows, cols = job.src.shape
        chunk_rows = rows // job.chunks
        per = n_steps // job.chunks

        def chunk_map(*idx, layer, per=per):
            return (layer, linear_step(*idx) // per, 0)

        in_specs.append(pl.BlockSpec((None, chunk_rows, cols), functools.partial(chunk_map, layer=job.layer)))
        args.append(job.src)
        out_specs.append(pl.BlockSpec((None, chunk_rows, cols), functools.partial(chunk_map, layer=0)))
        out_shape.append(jax.ShapeDtypeStruct((1, rows, cols), BF16))
    return in_specs, args, out_specs, out_shape


def _run_cast_jobs(in_refs, out_refs):
    for i_ref, o_ref in zip(in_refs, out_refs, strict=True):
        o_ref[...] = i_ref[...].astype(BF16)


class _Transposed:
    def __init__(self, a):
        self.a = a


def _scores(qs, keys, biases):
    out = []
    for k, b in zip(keys, biases):
        s = _dot(qs, k.a) if isinstance(k, _Transposed) else _dot_nt(qs, k)
        out.append(s if b is None else s + b)
    return out


def _row_max(s_list, sink):
    m = None
    for s in s_list:
        mi = jnp.max(s, axis=-1, keepdims=True)
        m = mi if m is None else jnp.maximum(m, mi)
    return m if sink is None else jnp.maximum(m, sink)


def _softmax_pv(s_list, vaugs, sink=None):
    m = _row_max(s_list, sink)
    r = None
    for s, v in zip(s_list, vaugs):
        p = jnp.exp((s - m).astype(BF16))
        ri = _dot_nt(p, v.a) if isinstance(v, _Transposed) else _dot(p, v)
        r = ri if r is None else r + ri
    den = r[:, LANES:]
    if sink is not None:
        den = den + jnp.exp(sink - m)
    return r[:, :LANES] / den


def _gqa(qe, qo, keys, vaugs, biases, sinks):
    mq = qe.shape[0]
    oe = jnp.zeros((mq, LANES), F32)
    oo = jnp.zeros((mq, LANES), F32)
    for g in range(2):
        lo, hi = HEAD_DIM * g, HEAD_DIM * (g + 1)
        qs = jnp.concatenate([_mask_q(qe, lo, hi), _mask_q(qo, lo, hi)], axis=0)
        sink = None
        if sinks is not None:
            row = lax.broadcasted_iota(jnp.int32, (2 * mq, 1), 0)
            sink = jnp.where(row < mq, sinks[2 * g], sinks[2 * g + 1])
        r = _softmax_pv(_scores(qs, keys, biases), vaugs, sink)
        msk = _lane_mask((mq, LANES), lo, hi)
        oe = jnp.where(msk, r[:mq], oe)
        oo = jnp.where(msk, r[mq:], oo)
    return _interleave_heads(oe, oo)


def _diff(q, keys, vaug_fn, lam):
    mq = q.shape[0]
    out = [jnp.zeros((mq, LANES), F32), jnp.zeros((mq, LANES), F32)]
    for hd in range(4):
        lo = HEAD_DIM * hd
        qs = jnp.concatenate([_mask_q(q, lo, lo + B_HALF), _mask_q(q, lo + B_HALF, lo + HEAD_DIM)], axis=0)
        o = _softmax_pv(_scores(qs, keys, [None] * len(keys)), vaug_fn(hd // 2))
        o = o[:mq] - lam * o[mq:]
        plo = HEAD_DIM * (hd % 2)
        out[hd // 2] = jnp.where(_lane_mask(o.shape, plo, plo + HEAD_DIM), o, out[hd // 2])
    return out


def _mha(q, keys, vaug_fn, bias_fn):
    mq = q.shape[0]
    out = [jnp.zeros((mq, LANES), F32), jnp.zeros((mq, LANES), F32)]
    for hd in range(4):
        lo = HEAD_DIM * hd
        qs = _mask_q(q, lo, lo + HEAD_DIM)
        o = _softmax_pv(_scores(qs, keys, bias_fn(hd)), vaug_fn(hd // 2))
        plo = HEAD_DIM * (hd % 2)
        out[hd // 2] = jnp.where(_lane_mask(o.shape, plo, plo + HEAD_DIM), o, out[hd // 2])
    return out


def _lambda(lam_ref, lam_init):
    lp = lam_ref[...]
    a = jnp.sum(lp[0:1, :] * lp[1:2, :], axis=-1, keepdims=True)
    b = jnp.sum(lp[2:3, :] * lp[3:4, :], axis=-1, keepdims=True)
    return jnp.exp(a) - jnp.exp(b) + lam_init


def _store_branches(br_ref, oa, ob, oc, od, bd_ref, gsub_ref, lam_init):
    bd = bd_ref[...]
    ob = [_head_rms_mxu(o, bd, gsub_ref[...], SUBLN_EPS) * (1.0 - lam_init) for o in ob]
    for j, o in enumerate((*oa, *ob, *oc, *od)):
        br_ref[:, j * LANES:(j + 1) * LANES] = o.astype(BF16)


def _attn_prompt_kernel(*refs, l, lam_init, n_jobs):
    qkv_ref, sink_ref, lam_ref, gsub_ref, bd_ref = refs[:5]
    br_ref = refs[5 + n_jobs]
    _run_cast_jobs(refs[5:5 + n_jobs], refs[6 + n_jobs:])
    sinks = [sink_ref[l, i] for i in range(4)]
    lam = _lambda(lam_ref, lam_init)
    for bi in range(qkv_ref.shape[0] // SEQ):
        r0 = bi * SEQ

        def cols(c, w):
            return qkv_ref[r0:r0 + SEQ, c:c + w]

        def qcols(c, w):
            return cols(c, w).astype(F32)

        oa = _gqa(qcols(A_QE, 128), qcols(A_QO, 128), [cols(A_K, 128)], [cols(A_V, VAUG)], [None], None)
        ob = _diff(qcols(B_Q, 256), [cols(B_K, 256)], lambda pr: [cols(B_V + pr * VAUG, VAUG)], lam)
        oc = _gqa(qcols(C_QE, 128), qcols(C_QO, 128), [cols(C_K, 128)], [cols(C_V, VAUG)], [None], sinks)
        od = _mha(qcols(D_Q, 256), [cols(D_K, 256)], lambda pr: [cols(D_V + pr * VAUG, VAUG)], lambda hd: [None])
        _store_branches(br_ref.at[r0:r0 + SEQ], oa, ob, oc, od, bd_ref, gsub_ref, lam_init)


def _attn_prompt_call(l, lam_init, qkv, sink_c, lam_b, gsub, bd128, jobs):
    rows = qkv.shape[0]
    tm = 4 * SEQ
    job_in, job_args, job_out, job_shape = _cast_job_specs(jobs, rows // tm, lambda b: b)
    outs = pl.pallas_call(
        functools.partial(_attn_prompt_kernel, l=l, lam_init=lam_init, n_jobs=len(jobs)),
        out_shape=[jax.ShapeDtypeStruct((rows, D_MODEL), BF16), *job_shape],
        grid=(rows // tm,),
        in_specs=[
            pl.BlockSpec((tm, QKV_W), lambda b: (b, 0)),
            pl.BlockSpec(memory_space=pltpu.SMEM),
            pl.BlockSpec((None, 4, B_HALF), lambda b: (l, 0, 0)),
            pl.BlockSpec((None, 1, LANES), lambda b: (l, 0, 0)),
            pl.BlockSpec((LANES, LANES), lambda b: (0, 0)),
            *job_in,
        ],
        out_specs=[pl.BlockSpec((tm, D_MODEL), lambda b: (b, 0)), *job_out],
        compiler_params=pltpu.CompilerParams(vmem_limit_bytes=VMEM_LIMIT),
        name="attn_prompt",
    )(qkv, sink_c, lam_b, gsub, bd128, *job_args)
    return outs[0], outs[1:]


def _attn_latent_kernel(*refs, l, lam_init, n_jobs):
    (kv_ref, xak_ref, xav_ref, xbk_ref, xbv_ref, xck_ref, xcv_ref, xdk_ref, xdv_ref,
     bias_ref, sink_ref, lam_ref, gsub_ref, bd_ref) = refs[:14]
    br_ref = refs[14 + n_jobs]
    _run_cast_jobs(refs[14:14 + n_jobs], refs[15 + n_jobs:])
    sinks = [sink_ref[l, i] for i in range(4)]
    lam = _lambda(lam_ref, lam_init)
    nsub = br_ref.shape[0] // ATT_BLK
    for sub in range(nsub):
        n = pl.program_id(1) * nsub + sub
        q0 = pl.multiple_of(n * ATT_BLK, ATT_BLK)

        def q(c, w):
            return kv_ref[pl.ds(q0, ATT_BLK), c:c + w].astype(F32)

        def lat(c, w):
            return kv_ref[:, c:c + w]

        def ctx_k(ref):
            return _Transposed(ref[...].reshape(ref.shape[0] * HEAD_DIM, PAST_LEN).astype(BF16))

        def ctx_v(ref, pr):
            vt = ref[2 * pr:2 * pr + 2].reshape(LANES, PAST_LEN).astype(BF16)
            return _Transposed(jnp.concatenate([vt, jnp.ones((LANES, PAST_LEN), BF16)], axis=0))

        oa = _gqa(q(A_QE, 128), q(A_QO, 128), [ctx_k(xak_ref), lat(A_K, 128)], [ctx_v(xav_ref, 0), lat(A_V, VAUG)],
                  [None, None], None)
        ob = _diff(q(B_Q, 256), [ctx_k(xbk_ref), lat(B_K, 256)],
                   lambda pr: [ctx_v(xbv_ref, pr), lat(B_V + pr * VAUG, VAUG)], lam)
        start_c = pl.multiple_of(jnp.clip(q0 - C_WINDOW, 0, DEC_SEQ - C_WIN), C_WINDOW)
        rowq = lax.broadcasted_iota(jnp.int32, (2 * ATT_BLK, C_WIN), 0) & (ATT_BLK - 1)
        colk = lax.broadcasted_iota(jnp.int32, (2 * ATT_BLK, C_WIN), 1)
        band = jnp.where(jnp.abs(rowq - colk + (q0 - start_c)) <= C_WINDOW, 0.0, NEG_INF)
        oc = _gqa(q(C_QE, 128), q(C_QO, 128),
                  [ctx_k(xck_ref), kv_ref[pl.ds(start_c, C_WIN), C_K:C_K + 128]],
                  [ctx_v(xcv_ref, 0), kv_ref[pl.ds(start_c, C_WIN), C_V:C_V + VAUG]],
                  [None, band], sinks)
        start_d = pl.multiple_of(jnp.where(n >= 2, NA_WIN_START[2] * GRID_W, 0), ATT_BLK)
        kwin = kv_ref[pl.ds(start_d, NA_WIN), D_K:D_K + 256]
        od = _mha(q(D_Q, 256), [ctx_k(xdk_ref), kwin],
                  lambda pr: [ctx_v(xdv_ref, pr),
                              kv_ref[pl.ds(start_d, NA_WIN), D_V + pr * VAUG:D_V + (pr + 1) * VAUG]],
                  lambda hd: [None, bias_ref[hd, sub]])
        _store_branches(br_ref.at[sub * ATT_BLK:(sub + 1) * ATT_BLK], oa, ob, oc, od, bd_ref, gsub_ref, lam_init)


def _attn_latent_call(l, lam_init, qkv, caches, dbias, sink_c, lam_b, gsub, bd128, jobs):
    rows = qkv.shape[0]
    nblk = DEC_SEQ // ATT_BLK
    nsub = 1
    steps_per_b = nblk // nsub
    cache_specs = [pl.BlockSpec((None, None, t.shape[2], HEAD_DIM, PAST_LEN), lambda b, n: (b, l, 0, 0, 0))
                   for t in caches]
    job_in, job_args, job_out, job_shape = _cast_job_specs(jobs, DEC_BATCH * steps_per_b,
                                                           lambda b, n: b * steps_per_b + n)
    outs = pl.pallas_call(
        functools.partial(_attn_latent_kernel, l=l, lam_init=lam_init, n_jobs=len(jobs)),
        out_shape=[jax.ShapeDtypeStruct((rows, D_MODEL), BF16), *job_shape],
        grid=(DEC_BATCH, steps_per_b),
        in_specs=[
            pl.BlockSpec((DEC_SEQ, QKV_W), lambda b, n: (b, 0)),
            *cache_specs,
            pl.BlockSpec((None, 4, nsub, ATT_BLK, NA_WIN), lambda b, n: (l, 0, n, 0, 0)),
            pl.BlockSpec(memory_space=pltpu.SMEM),
            pl.BlockSpec((None, 4, B_HALF), lambda b, n: (l, 0, 0)),
            pl.BlockSpec((None, 1, LANES), lambda b, n: (l, 0, 0)),
            pl.BlockSpec((LANES, LANES), lambda b, n: (0, 0)),
            *job_in,
        ],
        out_specs=[pl.BlockSpec((nsub * ATT_BLK, D_MODEL), lambda b, n: (b * steps_per_b + n, 0)), *job_out],
        compiler_params=pltpu.CompilerParams(vmem_limit_bytes=VMEM_LIMIT),
        name="attn_latent",
    )(qkv, *caches, dbias, sink_c, lam_b, gsub, bd128, *job_args)
    return outs[0], outs[1:]


MXU_TILE = 256
FFN_CHUNKS = ((0, 6 * MXU_TILE), (6 * MXU_TILE, D_FF))


def _post_kernel(x_ref, br_ref, gates_ref, gt1_ref, sh2_ref, sc2_ref, gt2_ref, g2_ref,
                 wb_ref, wo_ref, wfi_ref, wfo_ref, gf_ref, o_ref, *, final):
    merged = None
    for k in range(4):
        proj = _dot(br_ref[:, k * 256:(k + 1) * 256], wb_ref[k].astype(BF16))
        t = gates_ref[:, k * D_MODEL:(k + 1) * D_MODEL].astype(F32) * proj
        merged = t if merged is None else merged + t
    x1 = x_ref[...] + (0.5 * gt1_ref[...]) * _dot(merged.astype(BF16), wo_ref[...].astype(BF16))
    h2 = (_rms(x1, g2_ref[...], NORM_EPS) * (1.0 + sc2_ref[...]) + sh2_ref[...]).astype(BF16)
    acc = None
    for c0, c1 in FFN_CHUNKS:
        a = _dot(h2, wfi_ref[:, c0:c1])
        u = _dot(h2, wfi_ref[:, D_FF + c0:D_FF + c1])
        g = ((a * _sigmoid(a)) * u).astype(BF16)
        t = _dot(g, wfo_ref[c0:c1, :])
        acc = t if acc is None else acc + t
    xo = x1 + gt2_ref[...] * acc
    if final:
        xo = _rms(xo, gf_ref[...], NORM_EPS)
    o_ref[...] = xo


def _post_call(l, x, br, gates, mod, g2, wb, wo, wfi, wfo, gf, *, prompt, final):
    rows = x.shape[0]
    tm = 512
    if prompt:
        mod_row = lambda i: CTX_ROW
    else:
        mod_row = lambda i: i // (DEC_SEQ // tm)

    def mod_spec(chunk):
        return pl.BlockSpec((None, None, 1, D_MODEL), lambda i: (l, mod_row(i), 0, chunk))

    def resident(shape, layer):
        nd = len(shape)
        return pl.BlockSpec((None,) + shape, lambda i: (layer,) + (0,) * nd, pipeline_mode=pl.Buffered(1))

    return pl.pallas_call(
        functools.partial(_post_kernel, final=final),
        out_shape=jax.ShapeDtypeStruct((rows, D_MODEL), F32),
        grid=(rows // tm,),
        in_specs=[
            pl.BlockSpec((tm, D_MODEL), lambda i: (i, 0)),
            pl.BlockSpec((tm, D_MODEL), lambda i: (i, 0)),
            pl.BlockSpec((tm, GATE_W), lambda i: (i, 0)),
            mod_spec(2), mod_spec(3), mod_spec(4), mod_spec(5),
            pl.BlockSpec((None, 1, D_MODEL), lambda i: (l, 0, 0)),
            resident((4, 256, D_MODEL), l),
            resident((D_MODEL, D_MODEL), l),
            resident((D_MODEL, 2 * D_FF), 0),
            resident((D_FF, D_MODEL), 0),
            pl.BlockSpec((1, D_MODEL), lambda i: (0, 0)),
        ],
        out_specs=pl.BlockSpec((tm, D_MODEL), lambda i: (i, 0)),
        compiler_params=pltpu.CompilerParams(vmem_limit_bytes=VMEM_LIMIT),
        name="post_prompt" if prompt else "post_latent",
    )(x, br, gates, mod, mod, mod, mod, g2, wb, wo, wfi, wfo, gf)


def _rope_tables():
    t = np.arange(DEC_SEQ)
    row = (t // GRID_W).astype(np.float32)[:, None]
    col = (t % GRID_W).astype(np.float32)[:, None]
    tabs = []
    for d in (HEAD_DIM, B_HALF):
        quarter = d // 4
        inv = np.power(np.float32(ROPE_THETA), -np.arange(quarter, dtype=np.float32) / np.float32(quarter))
        ar, ac = row * inv, col * inv
        cos = np.concatenate([np.cos(ar), np.cos(ar), np.cos(ac), np.cos(ac)], axis=-1)
        sin = np.concatenate([-np.sin(ar), np.sin(ar), -np.sin(ac), np.sin(ac)], axis=-1)
        reps = LANES // d
        tabs += [jnp.asarray(np.tile(cos, (1, reps)), F32), jnp.asarray(np.tile(sin, (1, reps)), F32)]
    return tabs


def kernel(x_prompt, x_sample, cache_a_k, cache_a_v, cache_b_k, cache_b_v, cache_c_k, cache_c_v, cache_d_k, cache_d_v, c, c_ctx, w_ada, b_ada, g_norm1, w_in, g_q_a, g_k_a, lam_b, g_subln_b, sink_c, rpb_d, w_branch, w_out, g_norm2, w_ffn_in, w_ffn_out, g_final):
    wfi_l = wfo_l = None

    def ffn_jobs(layer, chunks_in, chunks_out):
        return [_CastJob(w_ffn_in, layer, chunks_in), _CastJob(w_ffn_out, layer, chunks_out)]

    gq = jnp.tile(g_q_a, (1, 2)).reshape(DEPTH, 1, LANES)
    gk = jnp.tile(g_k_a, (1, 2)).reshape(DEPTH, 1, LANES)
    gsub = jnp.tile(g_subln_b, (1, 2)).reshape(DEPTH, 1, LANES)
    g1 = g_norm1.reshape(DEPTH, 1, D_MODEL)
    g2 = g_norm2.reshape(DEPTH, 1, D_MODEL)
    gf = g_final.reshape(1, D_MODEL)
    head_of_lane = np.arange(LANES) // HEAD_DIM
    bd128 = jnp.asarray((head_of_lane[:, None] == head_of_lane[None, :]).astype(np.float32) / HEAD_DIM, BF16)
    rope_tabs = _rope_tables()
    caches = tuple(jnp.swapaxes(t, 3, 4) for t in (cache_a_k, cache_a_v, cache_b_k, cache_b_v,
                                                   cache_c_k, cache_c_v, cache_d_k, cache_d_v))

    cond = jnp.concatenate([c, c_ctx[None, :], jnp.zeros((MOD_ROWS - DEC_BATCH - 1, D_MODEL), F32)], axis=0)
    mod = _ada_call(cond, w_ada, b_ada)
    dbias, (w_in_l,) = _dbias_call(rpb_d, [_CastJob(w_in, 0, DEPTH * 4)])

    xp = x_prompt.reshape(BATCH * SEQ, D_MODEL)
    xs = x_sample.reshape(DEC_BATCH * DEC_SEQ, D_MODEL)
    new_kv = None
    for l in range(DEPTH):
        lam_init = 0.8 - 0.6 * math.exp(-0.3 * l)
        final = l == DEPTH - 1
        outs = _pre_call(l, xp, mod, g1, w_in_l, gq, gk, None, new_kv, prompt=True)
        qkv_p, gates_p, new_kv = outs[0], outs[1], outs[2:]
        jobs = ffn_jobs(0, 8, 8) if l == 0 else []
        br_p, cast = _attn_prompt_call(l, lam_init, qkv_p, sink_c, lam_b, gsub, bd128, jobs)
        if cast:
            wfi_l, wfo_l = cast
        xp = _post_call(l, xp, br_p, gates_p, mod, g2, w_branch, w_out, wfi_l, wfo_l, gf, prompt=True, final=final)

        qkv_s, gates_s = _pre_call(l, xs, mod, g1, w_in_l, gq, gk, rope_tabs, None, prompt=False)
        jobs = [_CastJob(w_in, l + 1, 32), *ffn_jobs(l + 1, 32, 16)] if l + 1 < DEPTH else []
        br_s, cast = _attn_latent_call(l, lam_init, qkv_s, caches, dbias, sink_c, lam_b, gsub, bd128, jobs)
        xs = _post_call(l, xs, br_s, gates_s, mod, g2, w_branch, w_out, wfi_l, wfo_l, gf, prompt=False, final=final)
        if cast:
            w_in_l, wfi_l, wfo_l = cast

    y_prompt = xp.reshape(BATCH, SEQ, D_MODEL)
    y_sample = xs.reshape(DEC_BATCH, DEC_SEQ, D_MODEL)
    return (y_prompt, y_sample, *(jnp.swapaxes(t, 3, 4) for t in new_kv))
```

```python
import functools
import math
from typing import NamedTuple

import numpy as np
import jax
import jax.numpy as jnp
from jax import lax
from jax.experimental import pallas as pl
from jax.experimental.pallas import tpu as pltpu

F32 = jnp.float32
BF16 = jnp.bfloat16

D_MODEL = 1024
BATCH = 32
SEQ = 256
DEPTH = 2
DEC_BATCH = 8
DEC_SEQ = 1024
PAST_LEN = 256
GRID_W = 64
HEAD_DIM = 64
B_HALF = HEAD_DIM // 2
C_WINDOW = 128
NA_ROWS = 8
NA_COLS = 16
D_FF = 2816
ROPE_THETA = 10000.0
NORM_EPS = 1e-6
SUBLN_EPS = 1e-5
NEG_INF = -1e30

GATE_W = 4 * D_MODEL
W_A, W_B, W_C, W_D, W_GATES = 0, 512, 1280, 1792, 2560
IN_COLS = W_GATES + GATE_W
MOD_ROWS = 16
CTX_ROW = DEC_BATCH
LANES = 128
ATT_BLK = 256
NA_BLK_ROWS = ATT_BLK // GRID_W
NA_WIN_ROWS = 12
NA_WIN = NA_WIN_ROWS * GRID_W
NA_WIN_START = (0, 0, 4, 4)
C_WIN = ATT_BLK + 2 * C_WINDOW
VMEM_LIMIT = 56 * 1024 * 1024

A_QE, A_QO, A_K, A_V = 0, 128, 256, 384
B_Q, B_K, B_V = 640, 896, 1152
C_QE, C_QO, C_K, C_V = 1664, 1792, 1920, 2048
D_Q, D_K, D_V = 2304, 2560, 2816
QKV_W = 3328
VAUG = 2 * LANES


def _dot(a, b):
    return jnp.dot(a, b, preferred_element_type=F32)


def _dot_nt(a, b):
    return lax.dot_general(a, b, (((1,), (1,)), ((), ())), preferred_element_type=F32)


def _sigmoid(x):
    return 0.5 * jnp.tanh(0.5 * x) + 0.5


def _rms(x, g, eps):
    ms = jnp.mean(x * x, axis=-1, keepdims=True)
    return x * lax.rsqrt(ms + eps) * g


def _head_rms_mxu(v, bd, g, eps):
    v2 = v * v
    hi = v2.astype(BF16)
    lo = (v2 - hi.astype(F32)).astype(BF16)
    ms = _dot(hi, bd) + _dot(lo, bd)
    return v * lax.rsqrt(ms + eps) * g


def _head_rms(v, g, eps):
    v2 = v * v
    left = _lane_mask(v.shape, 0, HEAD_DIM)
    s_left = jnp.sum(jnp.where(left, v2, 0.0), axis=-1, keepdims=True)
    s_right = jnp.sum(jnp.where(left, 0.0, v2), axis=-1, keepdims=True)
    ms = jnp.where(left, s_left, s_right) * (1.0 / HEAD_DIM)
    return v * lax.rsqrt(ms + eps) * g


def _lane_mask(shape, lo, hi):
    lane = lax.broadcasted_iota(jnp.int32, shape, 1)
    return (lane >= lo) & (lane < hi)


def _mask_q(qf, lo, hi):
    return jnp.where(_lane_mask(qf.shape, lo, hi), qf, 0.0).astype(BF16)


def _interleave_heads(lo, hi):
    left = _lane_mask(lo.shape, 0, HEAD_DIM)
    return (jnp.where(left, lo, pltpu.roll(hi, HEAD_DIM, 1)),
            jnp.where(left, pltpu.roll(lo, HEAD_DIM, 1), hi))


def _ada_kernel(cond_ref, w_ref, b_ref, o_ref):
    c = cond_ref[...]
    s = (c * _sigmoid(c)).astype(BF16)
    res = _dot(s, w_ref[...].astype(BF16)) + b_ref[...]
    for r in range(MOD_ROWS):
        o_ref[r] = res[r:r + 1]


def _ada_call(cond, w_ada, b_ada):
    tn = 1536
    return pl.pallas_call(
        _ada_kernel,
        out_shape=jax.ShapeDtypeStruct((DEPTH, MOD_ROWS, 1, 6 * D_MODEL), F32),
        grid=(DEPTH, 6 * D_MODEL // tn),
        in_specs=[
            pl.BlockSpec((MOD_ROWS, D_MODEL), lambda l, j: (0, 0)),
            pl.BlockSpec((None, D_MODEL, tn), lambda l, j: (l, 0, j)),
            pl.BlockSpec((None, 1, tn), lambda l, j: (l, 0, j)),
        ],
        out_specs=pl.BlockSpec((None, MOD_ROWS, 1, tn), lambda l, j: (l, 0, 0, j)),
        compiler_params=pltpu.CompilerParams(vmem_limit_bytes=VMEM_LIMIT),
        name="ada",
    )(cond, w_ada, b_ada.reshape(DEPTH, 1, 6 * D_MODEL))


def _dbias_kernel(*refs, n_jobs):
    rpb_ref, o_ref = refs[0], refs[1 + n_jobs]
    _run_cast_jobs(refs[1:1 + n_jobs], refs[2 + n_jobs:])
    n_dr, n_dc = 2 * NA_ROWS - 1, 2 * NA_COLS - 1
    base = (pl.program_id(0) * 4 + pl.program_id(1)) * (n_dr * n_dc)
    shape = (GRID_W, LANES)
    cq = lax.broadcasted_iota(jnp.int32, shape, 0)
    lane = lax.broadcasted_iota(jnp.int32, shape, 1)
    ck = lane & (GRID_W - 1)
    dc = jnp.clip(ck - cq, -(NA_COLS - 1), NA_COLS - 1) + (NA_COLS - 1)
    start_c = jnp.clip(cq - NA_COLS // 2, 0, GRID_W - NA_COLS)
    col_valid = (ck >= start_c) & (ck < start_c + NA_COLS)
    neg = jnp.full(shape, NEG_INF, F32)
    toeplitz = []
    for dr in range(n_dr):
        t = jnp.zeros(shape, F32)
        for m in range(n_dc):
            t = jnp.where(dc == m, rpb_ref[base + dr * n_dc + m], t)
        toeplitz.append(jnp.where(col_valid, t, neg))
    left = lane < GRID_W
    rows = DEC_SEQ // GRID_W
    for n in range(DEC_SEQ // ATT_BLK):
        for rq in range(NA_BLK_ROWS):
            r = NA_BLK_ROWS * n + rq
            start_r = min(max(r - NA_ROWS // 2, 0), rows - NA_ROWS)
            for jp in range(NA_WIN_ROWS // 2):
                pair = []
                for j in (2 * jp, 2 * jp + 1):
                    key_row = NA_WIN_START[n] + j
                    valid = start_r <= key_row < start_r + NA_ROWS
                    pair.append(toeplitz[key_row - r + NA_ROWS - 1] if valid else neg)
                tile = pair[0] if pair[0] is pair[1] else jnp.where(left, pair[0], pair[1])
                o_ref[n, rq * GRID_W:(rq + 1) * GRID_W, jp * LANES:(jp + 1) * LANES] = tile


def _dbias_call(rpb_d, jobs):
    nblk = DEC_SEQ // ATT_BLK
    job_in, job_args, job_out, job_shape = _cast_job_specs(jobs, DEPTH * 4, lambda l, h: l * 4 + h)
    outs = pl.pallas_call(
        functools.partial(_dbias_kernel, n_jobs=len(jobs)),
        out_shape=[jax.ShapeDtypeStruct((DEPTH, 4, nblk, ATT_BLK, NA_WIN), F32), *job_shape],
        grid=(DEPTH, 4),
        in_specs=[pl.BlockSpec(memory_space=pltpu.SMEM), *job_in],
        out_specs=[pl.BlockSpec((None, None, nblk, ATT_BLK, NA_WIN), lambda l, h: (l, h, 0, 0, 0)), *job_out],
        name="dbias",
    )(rpb_d.reshape(-1), *job_args)
    return outs[0], outs[1:]


def _swap_halves(v, half):
    lane = lax.broadcasted_iota(jnp.int32, v.shape, 1)
    up = pltpu.roll(v, LANES - half, 1)
    dn = pltpu.roll(v, half, 1)
    return jnp.where((lane & (2 * half - 1)) < half, up, dn)


def _rope(v, cos, sin, half):
    return v * cos + _swap_halves(v, half) * sin


def _pre_kernel(*refs, prompt, first):
    if prompt:
        (x_ref, sh_ref, sc_ref, g1_ref, w_ref, gq_ref, gk_ref) = refs[:7]
        (qkv_ref, gates_ref, ka_ref, va_ref, kb_ref, vb_ref, kc_ref, vc_ref, kd_ref, vd_ref) = refs[-10:]
    else:
        (x_ref, sh_ref, sc_ref, g1_ref, w_ref, gq_ref, gk_ref,
         ca_ref, sa_ref, cb_ref, sb_ref, qkv_ref, gates_ref) = refs

    h = (_rms(x_ref[...], g1_ref[...], NORM_EPS) * (1.0 + sc_ref[...]) + sh_ref[...]).astype(BF16)
    ones = jnp.ones((h.shape[0], LANES), BF16)

    def rope_a(v):
        return v if prompt else _rope(v, ca_ref[...], sa_ref[...], 16)

    def rope_b(v):
        return v if prompt else _rope(v, cb_ref[...], sb_ref[...], 8)

    def put(col, v):
        qkv_ref[:, col:col + v.shape[1]] = v.astype(BF16)

    def put_values(col, v):
        for c in range(v.shape[1] // LANES):
            put(col + c * VAUG, v[:, c * LANES:(c + 1) * LANES])
            put(col + c * VAUG + LANES, ones)

    def put_heads(ref, v):
        for bi in range(v.shape[0] // SEQ):
            for pr in range(v.shape[1] // LANES):
                t = v[bi * SEQ:(bi + 1) * SEQ, pr * LANES:(pr + 1) * LANES].T
                for hh in range(2):
                    piece = t[hh * HEAD_DIM:(hh + 1) * HEAD_DIM]
                    if first:
                        ref[bi, 0, 2 * pr + hh] = piece
                        for later in range(1, DEPTH):
                            ref[bi, later, 2 * pr + hh] = jnp.zeros_like(piece)
                    else:
                        ref[bi, 2 * pr + hh] = piece

    scale = HEAD_DIM ** -0.5
    acc_a = _dot(h, w_ref[:, W_A:W_A + 512])
    acc = _dot(h, w_ref[:, W_B:W_B + 768])
    scale_b = B_HALF ** -0.5
    for c in range(2):
        put(B_Q + c * 128, rope_b(acc[:, c * 128:(c + 1) * 128]) * scale_b)
        put(B_K + c * 128, rope_b(acc[:, 256 + c * 128:256 + (c + 1) * 128]))
    put_values(B_V, acc[:, 512:768])
    if prompt:
        put_heads(kb_ref, acc[:, 256:512])
        put_heads(vb_ref, acc[:, 512:768])
    acc = _dot(h, w_ref[:, W_C:W_C + 512])
    q_e, q_o = _interleave_heads(acc[:, 0:128], acc[:, 128:256])
    put(C_QE, rope_a(q_e) * scale)
    put(C_QO, rope_a(q_o) * scale)
    put(C_K, rope_a(acc[:, 256:384]))
    put_values(C_V, acc[:, 384:512])
    if prompt:
        put_heads(kc_ref, acc[:, 256:384])
        put_heads(vc_ref, acc[:, 384:512])
    acc = _dot(h, w_ref[:, W_D:W_D + 768])
    put(D_Q, acc[:, 0:256] * scale)
    put(D_K, acc[:, 256:512])
    put_values(D_V, acc[:, 512:768])
    if prompt:
        put_heads(kd_ref, acc[:, 256:512])
        put_heads(vd_ref, acc[:, 512:768])
    gq = gq_ref[...]
    q_e, q_o = _interleave_heads(acc_a[:, 0:128], acc_a[:, 128:256])
    put(A_QE, rope_a(_head_rms(q_e, gq, NORM_EPS)) * scale)
    put(A_QO, rope_a(_head_rms(q_o, gq, NORM_EPS)) * scale)
    k_a = _head_rms(acc_a[:, 256:384], gk_ref[...], NORM_EPS)
    put(A_K, rope_a(k_a))
    put_values(A_V, acc_a[:, 384:512])
    if prompt:
        put_heads(ka_ref, k_a)
        put_heads(va_ref, acc_a[:, 384:512])
    for j in range(GATE_W // 512):
        g = _dot(h, w_ref[:, W_GATES + j * 512:W_GATES + (j + 1) * 512])
        gates_ref[:, j * 512:(j + 1) * 512] = (jnp.tanh(0.5 * g) + 1.0).astype(BF16)


def _pre_call(l, x, mod, g1, w_in, gq, gk, rope_tabs, kv_prev, *, prompt):
    rows = x.shape[0]
    tm = 512
    tq = tm
    bpb = tm // SEQ
    aliases = {}
    if prompt:
        mod_row = lambda i: CTX_ROW
    else:
        mod_row = lambda i: i // (DEC_SEQ // tm)
    in_specs = [
        pl.BlockSpec((tm, D_MODEL), lambda i: (i, 0)),
        pl.BlockSpec((None, None, 1, D_MODEL), lambda i: (l, mod_row(i), 0, 0)),
        pl.BlockSpec((None, None, 1, D_MODEL), lambda i: (l, mod_row(i), 0, 1)),
        pl.BlockSpec((None, 1, D_MODEL), lambda i: (l, 0, 0)),
        pl.BlockSpec((None, D_MODEL, IN_COLS), lambda i: (0, 0, 0), pipeline_mode=pl.Buffered(1)),
        pl.BlockSpec((None, 1, LANES), lambda i: (l, 0, 0)),
        pl.BlockSpec((None, 1, LANES), lambda i: (l, 0, 0)),
    ]
    args = [x, mod, mod, g1, w_in, gq, gk]
    out_shape = [jax.ShapeDtypeStruct((rows, QKV_W), BF16), jax.ShapeDtypeStruct((rows, GATE_W), BF16)]
    out_specs = [pl.BlockSpec((tm, QKV_W), lambda i: (i, 0)), pl.BlockSpec((tm, GATE_W), lambda i: (i, 0))]
    if prompt:
        for j, nh in enumerate((2, 2, 4, 4, 2, 2, 4, 4)):
            out_shape.append(jax.ShapeDtypeStruct((BATCH, DEPTH, nh, HEAD_DIM, SEQ), F32))
            if kv_prev is None:
                out_specs.append(pl.BlockSpec((bpb, DEPTH, nh, HEAD_DIM, SEQ), lambda i: (i, 0, 0, 0, 0)))
            else:
                out_specs.append(pl.BlockSpec((bpb, None, nh, HEAD_DIM, SEQ), lambda i: (i, l, 0, 0, 0)))
                aliases[len(args)] = 2 + j
                in_specs.append(pl.BlockSpec(memory_space=pl.ANY))
                args.append(kv_prev[j])
    else:
        nt = DEC_SEQ // tq
        for t in rope_tabs:
            in_specs.append(pl.BlockSpec((tq, LANES), lambda i: (i % nt, 0)))
            args.append(t)
    return pl.pallas_call(
        functools.partial(_pre_kernel, prompt=prompt, first=kv_prev is None),
        out_shape=out_shape,
        grid=(rows // tm,),
        in_specs=in_specs,
        out_specs=out_specs,
        input_output_aliases=aliases,
        compiler_params=pltpu.CompilerParams(vmem_limit_bytes=VMEM_LIMIT),
        name="pre_prompt" if prompt else "pre_latent",
    )(*args)


class _CastJob(NamedTuple):
    src: jax.Array
    layer: int
    chunks: int


def _cast_job_specs(jobs, n_steps, linear_step):
    in_specs, args, out_specs, out_shape = [], [], [], []
    for job in jobs:
        _, rows, cols = job.src.shape
        chunk_rows = rows // job.chunks
        per = n_steps // job.chunks

        def chunk_map(*idx, layer, per=per):
            return (layer, linear_step(*idx) // per, 0)

        in_specs.append(pl.BlockSpec((None, chunk_rows, cols), functools.partial(chunk_map, layer=job.layer)))
        args.append(job.src)
        out_specs.append(pl.BlockSpec((None, chunk_rows, cols), functools.partial(chunk_map, layer=0)))
        out_shape.append(jax.ShapeDtypeStruct((1, rows, cols), BF16))
    return in_specs, args, out_specs, out_shape


def _run_cast_jobs(in_refs, out_refs):
    for i_ref, o_ref in zip(in_refs, out_refs, strict=True):
        o_ref[...] = i_ref[...].astype(BF16)


class _Transposed:
    def __init__(self, a):
        self.a = a


def _scores(qs, keys, biases):
    out = []
    for k, b in zip(keys, biases):
        s = _dot(qs, k.a) if isinstance(k, _Transposed) else _dot_nt(qs, k)
        out.append(s if b is None else s + b)
    return out


def _row_max(s_list, sink):
    m = None
    for s in s_list:
        mi = jnp.max(s, axis=-1, keepdims=True)
        m = mi if m is None else jnp.maximum(m, mi)
    return m if sink is None else jnp.maximum(m, sink)


def _softmax_pv(s_list, vaugs, sink=None):
    m = _row_max(s_list, sink)
    r = None
    for s, v in zip(s_list, vaugs):
        p = jnp.exp((s - m).astype(BF16))
        ri = _dot_nt(p, v.a) if isinstance(v, _Transposed) else _dot(p, v)
        r = ri if r is None else r + ri
    den = r[:, LANES:]
    if sink is not None:
        den = den + jnp.exp(sink - m)
    return r[:, :LANES] / den


def _gqa(qe, qo, keys, vaugs, biases, sinks):
    mq = qe.shape[0]
    oe = jnp.zeros((mq, LANES), F32)
    oo = jnp.zeros((mq, LANES), F32)
    for g in range(2):
        lo, hi = HEAD_DIM * g, HEAD_DIM * (g + 1)
        qs = jnp.concatenate([_mask_q(qe, lo, hi), _mask_q(qo, lo, hi)], axis=0)
        sink = None
        if sinks is not None:
            row = lax.broadcasted_iota(jnp.int32, (2 * mq, 1), 0)
            sink = jnp.where(row < mq, sinks[2 * g], sinks[2 * g + 1])
        r = _softmax_pv(_scores(qs, keys, biases), vaugs, sink)
        msk = _lane_mask((mq, LANES), lo, hi)
        oe = jnp.where(msk, r[:mq], oe)
        oo = jnp.where(msk, r[mq:], oo)
    return _interleave_heads(oe, oo)


def _diff(q, keys, vaug_fn, lam):
    mq = q.shape[0]
    out = [jnp.zeros((mq, LANES), F32), jnp.zeros((mq, LANES), F32)]
    for hd in range(4):
        lo = HEAD_DIM * hd
        qs = jnp.concatenate([_mask_q(q, lo, lo + B_HALF), _mask_q(q, lo + B_HALF, lo + HEAD_DIM)], axis=0)
        o = _softmax_pv(_scores(qs, keys, [None] * len(keys)), vaug_fn(hd // 2))
        o = o[:mq] - lam * o[mq:]
        plo = HEAD_DIM * (hd % 2)
        out[hd // 2] = jnp.where(_lane_mask(o.shape, plo, plo + HEAD_DIM), o, out[hd // 2])
    return out


def _mha(q, keys, vaug_fn, bias_fn):
    mq = q.shape[0]
    out = [jnp.zeros((mq, LANES), F32), jnp.zeros((mq, LANES), F32)]
    for hd in range(4):
        lo = HEAD_DIM * hd
        qs = _mask_q(q, lo, lo + HEAD_DIM)
        o = _softmax_pv(_scores(qs, keys, bias_fn(hd)), vaug_fn(hd // 2))
        plo = HEAD_DIM * (hd % 2)
        out[hd // 2] = jnp.where(_lane_mask(o.shape, plo, plo + HEAD_DIM), o, out[hd // 2])
    return out


def _lambda(lam_ref, lam_init):
    lp = lam_ref[...]
    a = jnp.sum(lp[0:1, :] * lp[1:2, :], axis=-1, keepdims=True)
    b = jnp.sum(lp[2:3, :] * lp[3:4, :], axis=-1, keepdims=True)
    return jnp.exp(a) - jnp.exp(b) + lam_init


def _store_branches(br_ref, oa, ob, oc, od, bd_ref, gsub_ref, lam_init):
    bd = bd_ref[...]
    ob = [_head_rms_mxu(o, bd, gsub_ref[...], SUBLN_EPS) * (1.0 - lam_init) for o in ob]
    for j, o in enumerate((*oa, *ob, *oc, *od)):
        br_ref[:, j * LANES:(j + 1) * LANES] = o.astype(BF16)


def _attn_prompt_kernel(*refs, l, lam_init, n_jobs):
    qkv_ref, sink_ref, lam_ref, gsub_ref, bd_ref = refs[:5]
    br_ref = refs[5 + n_jobs]
    _run_cast_jobs(refs[5:5 + n_jobs], refs[6 + n_jobs:])
    sinks = [sink_ref[l, i] for i in range(4)]
    lam = _lambda(lam_ref, lam_init)
    for bi in range(qkv_ref.shape[0] // SEQ):
        r0 = bi * SEQ

        def cols(c, w):
            return qkv_ref[r0:r0 + SEQ, c:c + w]

        def qcols(c, w):
            return cols(c, w).astype(F32)

        oa = _gqa(qcols(A_QE, 128), qcols(A_QO, 128), [cols(A_K, 128)], [cols(A_V, VAUG)], [None], None)
        ob = _diff(qcols(B_Q, 256), [cols(B_K, 256)], lambda pr: [cols(B_V + pr * VAUG, VAUG)], lam)
        oc = _gqa(qcols(C_QE, 128), qcols(C_QO, 128), [cols(C_K, 128)], [cols(C_V, VAUG)], [None], sinks)
        od = _mha(qcols(D_Q, 256), [cols(D_K, 256)], lambda pr: [cols(D_V + pr * VAUG, VAUG)], lambda hd: [None])
        _store_branches(br_ref.at[r0:r0 + SEQ], oa, ob, oc, od, bd_ref, gsub_ref, lam_init)


def _attn_prompt_call(l, lam_init, qkv, sink_c, lam_b, gsub, bd128, jobs):
    rows = qkv.shape[0]
    tm = 4 * SEQ
    job_in, job_args, job_out, job_shape = _cast_job_specs(jobs, rows // tm, lambda b: b)
    outs = pl.pallas_call(
        functools.partial(_attn_prompt_kernel, l=l, lam_init=lam_init, n_jobs=len(jobs)),
        out_shape=[jax.ShapeDtypeStruct((rows, D_MODEL), BF16), *job_shape],
        grid=(rows // tm,),
        in_specs=[
            pl.BlockSpec((tm, QKV_W), lambda b: (b, 0)),
            pl.BlockSpec(memory_space=pltpu.SMEM),
            pl.BlockSpec((None, 4, B_HALF), lambda b: (l, 0, 0)),
            pl.BlockSpec((None, 1, LANES), lambda b: (l, 0, 0)),
            pl.BlockSpec((LANES, LANES), lambda b: (0, 0)),
            *job_in,
        ],
        out_specs=[pl.BlockSpec((tm, D_MODEL), lambda b: (b, 0)), *job_out],
        compiler_params=pltpu.CompilerParams(vmem_limit_bytes=VMEM_LIMIT),
        name="attn_prompt",
    )(qkv, sink_c, lam_b, gsub, bd128, *job_args)
    return outs[0], outs[1:]


def _attn_latent_kernel(*refs, l, lam_init, n_jobs):
    (kv_ref, xak_ref, xav_ref, xbk_ref, xbv_ref, xck_ref, xcv_ref, xdk_ref, xdv_ref,
     bias_ref, sink_ref, lam_ref, gsub_ref, bd_ref) = refs[:14]
    br_ref = refs[14 + n_jobs]
    _run_cast_jobs(refs[14:14 + n_jobs], refs[15 + n_jobs:])
    sinks = [sink_ref[l, i] for i in range(4)]
    lam = _lambda(lam_ref, lam_init)
    nsub = br_ref.shape[0] // ATT_BLK
    for sub in range(nsub):
        n = pl.program_id(1) * nsub + sub
        q0 = pl.multiple_of(n * ATT_BLK, ATT_BLK)

        def q(c, w):
            return kv_ref[pl.ds(q0, ATT_BLK), c:c + w].astype(F32)

        def lat(c, w):
            return kv_ref[:, c:c + w]

        def ctx_k(ref):
            return _Transposed(ref[...].reshape(ref.shape[0] * HEAD_DIM, PAST_LEN).astype(BF16))

        def ctx_v(ref, pr):
            vt = ref[2 * pr:2 * pr + 2].reshape(LANES, PAST_LEN).astype(BF16)
            return _Transposed(jnp.concatenate([vt, jnp.ones((LANES, PAST_LEN), BF16)], axis=0))

        oa = _gqa(q(A_QE, 128), q(A_QO, 128), [ctx_k(xak_ref), lat(A_K, 128)], [ctx_v(xav_ref, 0), lat(A_V, VAUG)],
                  [None, None], None)
        ob = _diff(q(B_Q, 256), [ctx_k(xbk_ref), lat(B_K, 256)],
                   lambda pr: [ctx_v(xbv_ref, pr), lat(B_V + pr * VAUG, VAUG)], lam)
        start_c = pl.multiple_of(jnp.clip(q0 - C_WINDOW, 0, DEC_SEQ - C_WIN), C_WINDOW)
        rowq = lax.broadcasted_iota(jnp.int32, (2 * ATT_BLK, C_WIN), 0) & (ATT_BLK - 1)
        colk = lax.broadcasted_iota(jnp.int32, (2 * ATT_BLK, C_WIN), 1)
        band = jnp.where(jnp.abs(rowq - colk + (q0 - start_c)) <= C_WINDOW, 0.0, NEG_INF)
        oc = _gqa(q(C_QE, 128), q(C_QO, 128),
                  [ctx_k(xck_ref), kv_ref[pl.ds(start_c, C_WIN), C_K:C_K + 128]],
                  [ctx_v(xcv_ref, 0), kv_ref[pl.ds(start_c, C_WIN), C_V:C_V + VAUG]],
                  [None, band], sinks)
        start_d = pl.multiple_of(jnp.where(n >= 2, NA_WIN_START[2] * GRID_W, 0), ATT_BLK)
        kwin = kv_ref[pl.ds(start_d, NA_WIN), D_K:D_K + 256]
        od = _mha(q(D_Q, 256), [ctx_k(xdk_ref), kwin],
                  lambda pr: [ctx_v(xdv_ref, pr),
                              kv_ref[pl.ds(start_d, NA_WIN), D_V + pr * VAUG:D_V + (pr + 1) * VAUG]],
                  lambda hd: [None, bias_ref[hd, sub]])
        _store_branches(br_ref.at[sub * ATT_BLK:(sub + 1) * ATT_BLK], oa, ob, oc, od, bd_ref, gsub_ref, lam_init)


def _attn_latent_call(l, lam_init, qkv, caches, dbias, sink_c, lam_b, gsub, bd128, jobs):
    rows = qkv.shape[0]
    nblk = DEC_SEQ // ATT_BLK
    nsub = 1
    steps_per_b = nblk // nsub
    cache_specs = [pl.BlockSpec((None, None, t.shape[2], HEAD_DIM, PAST_LEN), lambda b, n: (b, l, 0, 0, 0))
                   for t in caches]
    job_in, job_args, job_out, job_shape = _cast_job_specs(jobs, DEC_BATCH * steps_per_b,
                                                           lambda b, n: b * steps_per_b + n)
    outs = pl.pallas_call(
        functools.partial(_attn_latent_kernel, l=l, lam_init=lam_init, n_jobs=len(jobs)),
        out_shape=[jax.ShapeDtypeStruct((rows, D_MODEL), BF16), *job_shape],
        grid=(DEC_BATCH, steps_per_b),
        in_specs=[
            pl.BlockSpec((DEC_SEQ, QKV_W), lambda b, n: (b, 0)),
            *cache_specs,
            pl.BlockSpec((None, 4, nsub, ATT_BLK, NA_WIN), lambda b, n: (l, 0, n, 0, 0)),
            pl.BlockSpec(memory_space=pltpu.SMEM),
            pl.BlockSpec((None, 4, B_HALF), lambda b, n: (l, 0, 0)),
            pl.BlockSpec((None, 1, LANES), lambda b, n: (l, 0, 0)),
            pl.BlockSpec((LANES, LANES), lambda b, n: (0, 0)),
            *job_in,
        ],
        out_specs=[pl.BlockSpec((nsub * ATT_BLK, D_MODEL), lambda b, n: (b * steps_per_b + n, 0)), *job_out],
        compiler_params=pltpu.CompilerParams(vmem_limit_bytes=VMEM_LIMIT),
        name="attn_latent",
    )(qkv, *caches, dbias, sink_c, lam_b, gsub, bd128, *job_args)
    return outs[0], outs[1:]


MXU_TILE = 256
FFN_CHUNKS = ((0, 6 * MXU_TILE), (6 * MXU_TILE, D_FF))


def _post_kernel(x_ref, br_ref, gates_ref, gt1_ref, sh2_ref, sc2_ref, gt2_ref, g2_ref,
                 wb_ref, wo_ref, wfi_hbm, wfo_hbm, gf_ref, o_ref, wfi_ref, wfo_ref, w_sem, *, final):
    first_step = pl.program_id(0) == 0

    def ffn_weight_copies():
        return (pltpu.make_async_copy(wfi_hbm.at[0], wfi_ref, w_sem.at[0]),
                pltpu.make_async_copy(wfo_hbm.at[0], wfo_ref, w_sem.at[1]))

    @pl.when(first_step)
    def _():
        for copy in ffn_weight_copies():
            copy.start()

    merged = None
    for k in range(4):
        proj = _dot(br_ref[:, k * 256:(k + 1) * 256], wb_ref[k].astype(BF16))
        t = gates_ref[:, k * D_MODEL:(k + 1) * D_MODEL].astype(F32) * proj
        merged = t if merged is None else merged + t
    x1 = x_ref[...] + (0.5 * gt1_ref[...]) * _dot(merged.astype(BF16), wo_ref[...].astype(BF16))
    h2 = (_rms(x1, g2_ref[...], NORM_EPS) * (1.0 + sc2_ref[...]) + sh2_ref[...]).astype(BF16)

    @pl.when(first_step)
    def _():
        for copy in ffn_weight_copies():
            copy.wait()

    acc = None
    for c0, c1 in FFN_CHUNKS:
        a = _dot(h2, wfi_ref[:, c0:c1])
        u = _dot(h2, wfi_ref[:, D_FF + c0:D_FF + c1])
        g = ((a * _sigmoid(a)) * u).astype(BF16)
        t = _dot(g, wfo_ref[c0:c1, :])
        acc = t if acc is None else acc + t
    xo = x1 + gt2_ref[...] * acc
    if final:
        xo = _rms(xo, gf_ref[...], NORM_EPS)
    o_ref[...] = xo


def _post_call(l, x, br, gates, mod, g2, wb, wo, wfi, wfo, gf, *, prompt, final):
    rows = x.shape[0]
    tm = 512
    if prompt:
        mod_row = lambda i: CTX_ROW
    else:
        mod_row = lambda i: i // (DEC_SEQ // tm)

    def mod_spec(chunk):
        return pl.BlockSpec((None, None, 1, D_MODEL), lambda i: (l, mod_row(i), 0, chunk))

    def resident(shape, layer):
        nd = len(shape)
        return pl.BlockSpec((None,) + shape, lambda i: (layer,) + (0,) * nd, pipeline_mode=pl.Buffered(1))

    return pl.pallas_call(
        functools.partial(_post_kernel, final=final),
        out_shape=jax.ShapeDtypeStruct((rows, D_MODEL), F32),
        grid=(rows // tm,),
        in_specs=[
            pl.BlockSpec((tm, D_MODEL), lambda i: (i, 0)),
            pl.BlockSpec((tm, D_MODEL), lambda i: (i, 0)),
            pl.BlockSpec((tm, GATE_W), lambda i: (i, 0)),
            mod_spec(2), mod_spec(3), mod_spec(4), mod_spec(5),
            pl.BlockSpec((None, 1, D_MODEL), lambda i: (l, 0, 0)),
            resident((4, 256, D_MODEL), l),
            resident((D_MODEL, D_MODEL), l),
            pl.BlockSpec(memory_space=pl.ANY),
            pl.BlockSpec(memory_space=pl.ANY),
            pl.BlockSpec((1, D_MODEL), lambda i: (0, 0)),
        ],
        out_specs=pl.BlockSpec((tm, D_MODEL), lambda i: (i, 0)),
        scratch_shapes=[pltpu.VMEM((D_MODEL, 2 * D_FF), BF16), pltpu.VMEM((D_FF, D_MODEL), BF16),
                        pltpu.SemaphoreType.DMA((2,))],
        compiler_params=pltpu.CompilerParams(dimension_semantics=("arbitrary",), vmem_limit_bytes=VMEM_LIMIT),
        name="post_prompt" if prompt else "post_latent",
    )(x, br, gates, mod, mod, mod, mod, g2, wb, wo, wfi, wfo, gf)


def _rope_tables():
    t = np.arange(DEC_SEQ)
    row = (t // GRID_W).astype(np.float32)[:, None]
    col = (t % GRID_W).astype(np.float32)[:, None]
    tabs = []
    for d in (HEAD_DIM, B_HALF):
        quarter = d // 4
        inv = np.power(np.float32(ROPE_THETA), -np.arange(quarter, dtype=np.float32) / np.float32(quarter))
        ar, ac = row * inv, col * inv
        cos = np.concatenate([np.cos(ar), np.cos(ar), np.cos(ac), np.cos(ac)], axis=-1)
        sin = np.concatenate([-np.sin(ar), np.sin(ar), -np.sin(ac), np.sin(ac)], axis=-1)
        reps = LANES // d
        tabs += [jnp.asarray(np.tile(cos, (1, reps)), F32), jnp.asarray(np.tile(sin, (1, reps)), F32)]
    return tabs


def kernel(x_prompt, x_sample, cache_a_k, cache_a_v, cache_b_k, cache_b_v, cache_c_k, cache_c_v, cache_d_k, cache_d_v, c, c_ctx, w_ada, b_ada, g_norm1, w_in, g_q_a, g_k_a, lam_b, g_subln_b, sink_c, rpb_d, w_branch, w_out, g_norm2, w_ffn_in, w_ffn_out, g_final):
    wfi_l = wfo_l = None

    def ffn_jobs(layer, chunks_in, chunks_out):
        return [_CastJob(w_ffn_in, layer, chunks_in), _CastJob(w_ffn_out, layer, chunks_out)]

    gq = jnp.tile(g_q_a, (1, 2)).reshape(DEPTH, 1, LANES)
    gk = jnp.tile(g_k_a, (1, 2)).reshape(DEPTH, 1, LANES)
    gsub = jnp.tile(g_subln_b, (1, 2)).reshape(DEPTH, 1, LANES)
    g1 = g_norm1.reshape(DEPTH, 1, D_MODEL)
    g2 = g_norm2.reshape(DEPTH, 1, D_MODEL)
    gf = g_final.reshape(1, D_MODEL)
    head_of_lane = np.arange(LANES) // HEAD_DIM
    bd128 = jnp.asarray((head_of_lane[:, None] == head_of_lane[None, :]).astype(np.float32) / HEAD_DIM, BF16)
    rope_tabs = _rope_tables()
    caches = tuple(jnp.swapaxes(t, 3, 4) for t in (cache_a_k, cache_a_v, cache_b_k, cache_b_v,
                                                   cache_c_k, cache_c_v, cache_d_k, cache_d_v))

    cond = jnp.concatenate([c, c_ctx[None, :], jnp.zeros((MOD_ROWS - DEC_BATCH - 1, D_MODEL), F32)], axis=0)
    mod = _ada_call(cond, w_ada, b_ada)
    dbias, (w_in_l,) = _dbias_call(rpb_d, [_CastJob(w_in, 0, DEPTH * 4)])

    xp = x_prompt.reshape(BATCH * SEQ, D_MODEL)
    xs = x_sample.reshape(DEC_BATCH * DEC_SEQ, D_MODEL)
    new_kv = None
    for l in range(DEPTH):
        lam_init = 0.8 - 0.6 * math.exp(-0.3 * l)
        final = l == DEPTH - 1
        outs = _pre_call(l, xp, mod, g1, w_in_l, gq, gk, None, new_kv, prompt=True)
        qkv_p, gates_p, new_kv = outs[0], outs[1], outs[2:]
        jobs = ffn_jobs(0, 8, 8) if l == 0 else []
        br_p, cast = _attn_prompt_call(l, lam_init, qkv_p, sink_c, lam_b, gsub, bd128, jobs)
        if cast:
            wfi_l, wfo_l = cast
        xp = _post_call(l, xp, br_p, gates_p, mod, g2, w_branch, w_out, wfi_l, wfo_l, gf, prompt=True, final=final)

        qkv_s, gates_s = _pre_call(l, xs, mod, g1, w_in_l, gq, gk, rope_tabs, None, prompt=False)
        jobs = [_CastJob(w_in, l + 1, 32), *ffn_jobs(l + 1, 32, 16)] if l + 1 < DEPTH else []
        br_s, cast = _attn_latent_call(l, lam_init, qkv_s, caches, dbias, sink_c, lam_b, gsub, bd128, jobs)
        xs = _post_call(l, xs, br_s, gates_s, mod, g2, w_branch, w_out, wfi_l, wfo_l, gf, prompt=False, final=final)
        if cast:
            w_in_l, wfi_l, wfo_l = cast

    y_prompt = xp.reshape(BATCH, SEQ, D_MODEL)
    y_sample = xs.reshape(DEC_BATCH, DEC_SEQ, D_MODEL)
    return (y_prompt, y_sample, *(jnp.swapaxes(t, 3, 4) for t in new_kv))
```

```python
import functools
import math
from typing import NamedTuple

import numpy as np
import jax
import jax.numpy as jnp
from jax import lax
from jax.experimental import pallas as pl
from jax.experimental.pallas import tpu as pltpu

F32 = jnp.float32
BF16 = jnp.bfloat16

D_MODEL = 1024
BATCH = 32
SEQ = 256
DEPTH = 2
DEC_BATCH = 8
DEC_SEQ = 1024
PAST_LEN = 256
GRID_W = 64
HEAD_DIM = 64
B_HALF = HEAD_DIM // 2
C_WINDOW = 128
NA_ROWS = 8
NA_COLS = 16
D_FF = 2816
ROPE_THETA = 10000.0
NORM_EPS = 1e-6
SUBLN_EPS = 1e-5
NEG_INF = -1e30

GATE_W = 4 * D_MODEL
W_A, W_B, W_C, W_D, W_GATES = 0, 512, 1280, 1792, 2560
IN_COLS = W_GATES + GATE_W
MOD_ROWS = 16
CTX_ROW = DEC_BATCH
LANES = 128
ATT_BLK = 256
NA_BLK_ROWS = ATT_BLK // GRID_W
NA_WIN_ROWS = 12
NA_WIN = NA_WIN_ROWS * GRID_W
NA_WIN_START = (0, 0, 4, 4)
C_WIN = ATT_BLK + 2 * C_WINDOW
VMEM_LIMIT = 56 * 1024 * 1024

A_QE, A_QO, A_K, A_V = 0, 128, 256, 384
B_Q, B_K, B_V = 640, 896, 1152
C_QE, C_QO, C_K, C_V = 1664, 1792, 1920, 2048
D_Q, D_K, D_V = 2304, 2560, 2816
QKV_W = 3328
VAUG = 2 * LANES


def _dot(a, b):
    return jnp.dot(a, b, preferred_element_type=F32)


def _dot_nt(a, b):
    return lax.dot_general(a, b, (((1,), (1,)), ((), ())), preferred_element_type=F32)


def _sigmoid(x):
    return 0.5 * jnp.tanh(0.5 * x) + 0.5


def _rms(x, g, eps):
    ms = jnp.mean(x * x, axis=-1, keepdims=True)
    return x * lax.rsqrt(ms + eps) * g


def _head_rms_mxu(v, bd, g, eps):
    v2 = v * v
    hi = v2.astype(BF16)
    lo = (v2 - hi.astype(F32)).astype(BF16)
    ms = _dot(hi, bd) + _dot(lo, bd)
    return v * lax.rsqrt(ms + eps) * g


def _head_rms(v, g, eps):
    v2 = v * v
    left = _lane_mask(v.shape, 0, HEAD_DIM)
    s_left = jnp.sum(jnp.where(left, v2, 0.0), axis=-1, keepdims=True)
    s_right = jnp.sum(jnp.where(left, 0.0, v2), axis=-1, keepdims=True)
    ms = jnp.where(left, s_left, s_right) * (1.0 / HEAD_DIM)
    return v * lax.rsqrt(ms + eps) * g


def _lane_mask(shape, lo, hi):
    lane = lax.broadcasted_iota(jnp.int32, shape, 1)
    return (lane >= lo) & (lane < hi)


def _mask_q(qf, lo, hi):
    return jnp.where(_lane_mask(qf.shape, lo, hi), qf, 0.0).astype(BF16)


def _interleave_heads(lo, hi):
    left = _lane_mask(lo.shape, 0, HEAD_DIM)
    return (jnp.where(left, lo, pltpu.roll(hi, HEAD_DIM, 1)),
            jnp.where(left, pltpu.roll(lo, HEAD_DIM, 1), hi))


def _ada_kernel(cond_ref, w_ref, b_ref, o_ref):
    c = cond_ref[...]
    s = (c * _sigmoid(c)).astype(BF16)
    res = _dot(s, w_ref[...].astype(BF16)) + b_ref[...]
    for r in range(MOD_ROWS):
        o_ref[r] = res[r:r + 1]


ADA_TILE = 6 * D_MODEL // 4


def _dbias_kernel(rpb_ref, o_ref):
    n_dr, n_dc = 2 * NA_ROWS - 1, 2 * NA_COLS - 1
    base = (pl.program_id(0) * 4 + pl.program_id(1)) * (n_dr * n_dc)
    shape = (GRID_W, LANES)
    cq = lax.broadcasted_iota(jnp.int32, shape, 0)
    lane = lax.broadcasted_iota(jnp.int32, shape, 1)
    ck = lane & (GRID_W - 1)
    dc = jnp.clip(ck - cq, -(NA_COLS - 1), NA_COLS - 1) + (NA_COLS - 1)
    start_c = jnp.clip(cq - NA_COLS // 2, 0, GRID_W - NA_COLS)
    col_valid = (ck >= start_c) & (ck < start_c + NA_COLS)
    neg = jnp.full(shape, NEG_INF, F32)
    toeplitz = []
    for dr in range(n_dr):
        t = jnp.zeros(shape, F32)
        for m in range(n_dc):
            t = jnp.where(dc == m, rpb_ref[base + dr * n_dc + m], t)
        toeplitz.append(jnp.where(col_valid, t, neg))
    left = lane < GRID_W
    rows = DEC_SEQ // GRID_W
    for n in range(DEC_SEQ // ATT_BLK):
        for rq in range(NA_BLK_ROWS):
            r = NA_BLK_ROWS * n + rq
            start_r = min(max(r - NA_ROWS // 2, 0), rows - NA_ROWS)
            for jp in range(NA_WIN_ROWS // 2):
                pair = []
                for j in (2 * jp, 2 * jp + 1):
                    key_row = NA_WIN_START[n] + j
                    valid = start_r <= key_row < start_r + NA_ROWS
                    pair.append(toeplitz[key_row - r + NA_ROWS - 1] if valid else neg)
                tile = pair[0] if pair[0] is pair[1] else jnp.where(left, pair[0], pair[1])
                o_ref[n, rq * GRID_W:(rq + 1) * GRID_W, jp * LANES:(jp + 1) * LANES] = tile


def _setup_kernel(*refs, n_jobs):
    cond_ref, w_ref, b_ref, rpb_ref = refs[:4]
    mod_ref, bias_ref = refs[4 + n_jobs:6 + n_jobs]
    _run_cast_jobs(refs[4:4 + n_jobs], refs[6 + n_jobs:])
    _ada_kernel(cond_ref, w_ref, b_ref, mod_ref)
    _dbias_kernel(rpb_ref, bias_ref)


def _setup_call(cond, w_ada, b_ada, rpb_d, jobs):
    nblk = DEC_SEQ // ATT_BLK
    job_in, job_args, job_out, job_shape = _cast_job_specs(jobs, DEPTH * 4, lambda l, j: l * 4 + j)
    outs = pl.pallas_call(
        functools.partial(_setup_kernel, n_jobs=len(jobs)),
        out_shape=[jax.ShapeDtypeStruct((DEPTH, MOD_ROWS, 1, 6 * D_MODEL), F32),
                   jax.ShapeDtypeStruct((DEPTH, 4, nblk, ATT_BLK, NA_WIN), F32), *job_shape],
        grid=(DEPTH, 4),
        in_specs=[
            pl.BlockSpec((MOD_ROWS, D_MODEL), lambda l, j: (0, 0)),
            pl.BlockSpec((None, D_MODEL, ADA_TILE), lambda l, j: (l, 0, j)),
            pl.BlockSpec((None, 1, ADA_TILE), lambda l, j: (l, 0, j)),
            pl.BlockSpec(memory_space=pltpu.SMEM),
            *job_in,
        ],
        out_specs=[pl.BlockSpec((None, MOD_ROWS, 1, ADA_TILE), lambda l, j: (l, 0, 0, j)),
                   pl.BlockSpec((None, None, nblk, ATT_BLK, NA_WIN), lambda l, j: (l, j, 0, 0, 0)), *job_out],
        compiler_params=pltpu.CompilerParams(vmem_limit_bytes=VMEM_LIMIT),
        name="setup",
    )(cond, w_ada, b_ada.reshape(DEPTH, 1, 6 * D_MODEL), rpb_d.reshape(-1), *job_args)
    return outs[0], outs[1], outs[2:]


def _swap_halves(v, half):
    lane = lax.broadcasted_iota(jnp.int32, v.shape, 1)
    up = pltpu.roll(v, LANES - half, 1)
    dn = pltpu.roll(v, half, 1)
    return jnp.where((lane & (2 * half - 1)) < half, up, dn)


def _rope(v, cos, sin, half):
    return v * cos + _swap_halves(v, half) * sin


def _pre_kernel(*refs, prompt, first):
    if prompt:
        (x_ref, sh_ref, sc_ref, g1_ref, w_ref, gq_ref, gk_ref) = refs[:7]
        (qkv_ref, gates_ref, ka_ref, va_ref, kb_ref, vb_ref, kc_ref, vc_ref, kd_ref, vd_ref) = refs[-10:]
    else:
        (x_ref, sh_ref, sc_ref, g1_ref, w_ref, gq_ref, gk_ref,
         ca_ref, sa_ref, cb_ref, sb_ref, qkv_ref, gates_ref) = refs

    h = (_rms(x_ref[...], g1_ref[...], NORM_EPS) * (1.0 + sc_ref[...]) + sh_ref[...]).astype(BF16)
    ones = jnp.ones((h.shape[0], LANES), BF16)

    def rope_a(v):
        return v if prompt else _rope(v, ca_ref[...], sa_ref[...], 16)

    def rope_b(v):
        return v if prompt else _rope(v, cb_ref[...], sb_ref[...], 8)

    def put(col, v):
        qkv_ref[:, col:col + v.shape[1]] = v.astype(BF16)

    def put_values(col, v):
        for c in range(v.shape[1] // LANES):
            put(col + c * VAUG, v[:, c * LANES:(c + 1) * LANES])
            put(col + c * VAUG + LANES, ones)

    def put_heads(ref, v):
        for bi in range(v.shape[0] // SEQ):
            for pr in range(v.shape[1] // LANES):
                t = v[bi * SEQ:(bi + 1) * SEQ, pr * LANES:(pr + 1) * LANES].T
                for hh in range(2):
                    piece = t[hh * HEAD_DIM:(hh + 1) * HEAD_DIM]
                    if first:
                        ref[bi, 0, 2 * pr + hh] = piece
                        for later in range(1, DEPTH):
                            ref[bi, later, 2 * pr + hh] = jnp.zeros_like(piece)
                    else:
                        ref[bi, 2 * pr + hh] = piece

    scale = HEAD_DIM ** -0.5
    acc_a = _dot(h, w_ref[:, W_A:W_A + 512])
    acc = _dot(h, w_ref[:, W_B:W_B + 768])
    scale_b = B_HALF ** -0.5
    for c in range(2):
        put(B_Q + c * 128, rope_b(acc[:, c * 128:(c + 1) * 128]) * scale_b)
        put(B_K + c * 128, rope_b(acc[:, 256 + c * 128:256 + (c + 1) * 128]))
    put_values(B_V, acc[:, 512:768])
    if prompt:
        put_heads(kb_ref, acc[:, 256:512])
        put_heads(vb_ref, acc[:, 512:768])
    acc = _dot(h, w_ref[:, W_C:W_C + 512])
    q_e, q_o = _interleave_heads(acc[:, 0:128], acc[:, 128:256])
    put(C_QE, rope_a(q_e) * scale)
    put(C_QO, rope_a(q_o) * scale)
    put(C_K, rope_a(acc[:, 256:384]))
    put_values(C_V, acc[:, 384:512])
    if prompt:
        put_heads(kc_ref, acc[:, 256:384])
        put_heads(vc_ref, acc[:, 384:512])
    acc = _dot(h, w_ref[:, W_D:W_D + 768])
    put(D_Q, acc[:, 0:256] * scale)
    put(D_K, acc[:, 256:512])
    put_values(D_V, acc[:, 512:768])
    if prompt:
        put_heads(kd_ref, acc[:, 256:512])
        put_heads(vd_ref, acc[:, 512:768])
    gq = gq_ref[...]
    q_e, q_o = _interleave_heads(acc_a[:, 0:128], acc_a[:, 128:256])
    put(A_QE, rope_a(_head_rms(q_e, gq, NORM_EPS)) * scale)
    put(A_QO, rope_a(_head_rms(q_o, gq, NORM_EPS)) * scale)
    k_a = _head_rms(acc_a[:, 256:384], gk_ref[...], NORM_EPS)
    put(A_K, rope_a(k_a))
    put_values(A_V, acc_a[:, 384:512])
    if prompt:
        put_heads(ka_ref, k_a)
        put_heads(va_ref, acc_a[:, 384:512])
    for j in range(GATE_W // 512):
        g = _dot(h, w_ref[:, W_GATES + j * 512:W_GATES + (j + 1) * 512])
        gates_ref[:, j * 512:(j + 1) * 512] = (jnp.tanh(0.5 * g) + 1.0).astype(BF16)


def _pre_call(l, x, mod, g1, w_in, gq, gk, rope_tabs, kv_prev, *, prompt):
    rows = x.shape[0]
    tm = 512
    tq = tm
    bpb = tm // SEQ
    aliases = {}
    if prompt:
        mod_row = lambda i: CTX_ROW
    else:
        mod_row = lambda i: i // (DEC_SEQ // tm)
    in_specs = [
        pl.BlockSpec((tm, D_MODEL), lambda i: (i, 0)),
        pl.BlockSpec((None, None, 1, D_MODEL), lambda i: (l, mod_row(i), 0, 0)),
        pl.BlockSpec((None, None, 1, D_MODEL), lambda i: (l, mod_row(i), 0, 1)),
        pl.BlockSpec((None, 1, D_MODEL), lambda i: (l, 0, 0)),
        pl.BlockSpec((None, D_MODEL, IN_COLS), lambda i: (0, 0, 0), pipeline_mode=pl.Buffered(1)),
        pl.BlockSpec((None, 1, LANES), lambda i: (l, 0, 0)),
        pl.BlockSpec((None, 1, LANES), lambda i: (l, 0, 0)),
    ]
    args = [x, mod, mod, g1, w_in, gq, gk]
    out_shape = [jax.ShapeDtypeStruct((rows, QKV_W), BF16), jax.ShapeDtypeStruct((rows, GATE_W), BF16)]
    out_specs = [pl.BlockSpec((tm, QKV_W), lambda i: (i, 0)), pl.BlockSpec((tm, GATE_W), lambda i: (i, 0))]
    if prompt:
        for j, nh in enumerate((2, 2, 4, 4, 2, 2, 4, 4)):
            out_shape.append(jax.ShapeDtypeStruct((BATCH, DEPTH, nh, HEAD_DIM, SEQ), F32))
            if kv_prev is None:
                out_specs.append(pl.BlockSpec((bpb, DEPTH, nh, HEAD_DIM, SEQ), lambda i: (i, 0, 0, 0, 0)))
            else:
                out_specs.append(pl.BlockSpec((bpb, None, nh, HEAD_DIM, SEQ), lambda i: (i, l, 0, 0, 0)))
                aliases[len(args)] = 2 + j
                in_specs.append(pl.BlockSpec(memory_space=pl.ANY))
                args.append(kv_prev[j])
    else:
        nt = DEC_SEQ // tq
        for t in rope_tabs:
            in_specs.append(pl.BlockSpec((tq, LANES), lambda i: (i % nt, 0)))
            args.append(t)
    return pl.pallas_call(
        functools.partial(_pre_kernel, prompt=prompt, first=kv_prev is None),
        out_shape=out_shape,
        grid=(rows // tm,),
        in_specs=in_specs,
        out_specs=out_specs,
        input_output_aliases=aliases,
        compiler_params=pltpu.CompilerParams(vmem_limit_bytes=VMEM_LIMIT),
        name="pre_prompt" if prompt else "pre_latent",
    )(*args)


class _CastJob(NamedTuple):
    src: jax.Array
    layer: int
    chunks: int


def _cast_job_specs(jobs, n_steps, linear_step):
    in_specs, args, out_specs, out_shape = [], [], [], []
    for job in jobs:
        _, rows, cols = job.src.shape
        chunk_rows = rows // job.chunks
        per = n_steps // job.chunks

        def chunk_map(*idx, layer, per=per):
            return (layer, linear_step(*idx) // per, 0)

        in_specs.append(pl.BlockSpec((None, chunk_rows, cols), functools.partial(chunk_map, layer=job.layer)))
        args.append(job.src)
        out_specs.append(pl.BlockSpec((None, chunk_rows, cols), functools.partial(chunk_map, layer=0)))
        out_shape.append(jax.ShapeDtypeStruct((1, rows, cols), BF16))
    return in_specs, args, out_specs, out_shape


def _run_cast_jobs(in_refs, out_refs):
    for i_ref, o_ref in zip(in_refs, out_refs, strict=True):
        o_ref[...] = i_ref[...].astype(BF16)


class _Transposed:
    def __init__(self, a):
        self.a = a


def _scores(qs, keys, biases):
    out = []
    for k, b in zip(keys, biases):
        s = _dot(qs, k.a) if isinstance(k, _Transposed) else _dot_nt(qs, k)
        out.append(s if b is None else s + b)
    return out


def _row_max(s_list, sink):
    m = None
    for s in s_list:
        mi = jnp.max(s, axis=-1, keepdims=True)
        m = mi if m is None else jnp.maximum(m, mi)
    return m if sink is None else jnp.maximum(m, sink)


def _softmax_pv(s_list, vaugs, sink=None):
    m = _row_max(s_list, sink)
    r = None
    for s, v in zip(s_list, vaugs):
        p = jnp.exp((s - m).astype(BF16))
        ri = _dot_nt(p, v.a) if isinstance(v, _Transposed) else _dot(p, v)
        r = ri if r is None else r + ri
    den = r[:, LANES:]
    if sink is not None:
        den = den + jnp.exp(sink - m)
    return r[:, :LANES] / den


def _gqa(qe, qo, keys, vaugs, biases, sinks):
    mq = qe.shape[0]
    oe = jnp.zeros((mq, LANES), F32)
    oo = jnp.zeros((mq, LANES), F32)
    for g in range(2):
        lo, hi = HEAD_DIM * g, HEAD_DIM * (g + 1)
        qs = jnp.concatenate([_mask_q(qe, lo, hi), _mask_q(qo, lo, hi)], axis=0)
        sink = None
        if sinks is not None:
            row = lax.broadcasted_iota(jnp.int32, (2 * mq, 1), 0)
            sink = jnp.where(row < mq, sinks[2 * g], sinks[2 * g + 1])
        r = _softmax_pv(_scores(qs, keys, biases), vaugs, sink)
        msk = _lane_mask((mq, LANES), lo, hi)
        oe = jnp.where(msk, r[:mq], oe)
        oo = jnp.where(msk, r[mq:], oo)
    return _interleave_heads(oe, oo)


def _diff(q, keys, vaug_fn, lam):
    mq = q.shape[0]
    out = [jnp.zeros((mq, LANES), F32), jnp.zeros((mq, LANES), F32)]
    for hd in range(4):
        lo = HEAD_DIM * hd
        qs = jnp.concatenate([_mask_q(q, lo, lo + B_HALF), _mask_q(q, lo + B_HALF, lo + HEAD_DIM)], axis=0)
        o = _softmax_pv(_scores(qs, keys, [None] * len(keys)), vaug_fn(hd // 2))
        o = o[:mq] - lam * o[mq:]
        plo = HEAD_DIM * (hd % 2)
        out[hd // 2] = jnp.where(_lane_mask(o.shape, plo, plo + HEAD_DIM), o, out[hd // 2])
    return out


def _mha(q, keys, vaug_fn, bias_fn):
    mq = q.shape[0]
    out = [jnp.zeros((mq, LANES), F32), jnp.zeros((mq, LANES), F32)]
    for hd in range(4):
        lo = HEAD_DIM * hd
        qs = _mask_q(q, lo, lo + HEAD_DIM)
        o = _softmax_pv(_scores(qs, keys, bias_fn(hd)), vaug_fn(hd // 2))
        plo = HEAD_DIM * (hd % 2)
        out[hd // 2] = jnp.where(_lane_mask(o.shape, plo, plo + HEAD_DIM), o, out[hd // 2])
    return out


def _lambda(lam_ref, lam_init):
    lp = lam_ref[...]
    a = jnp.sum(lp[0:1, :] * lp[1:2, :], axis=-1, keepdims=True)
    b = jnp.sum(lp[2:3, :] * lp[3:4, :], axis=-1, keepdims=True)
    return jnp.exp(a) - jnp.exp(b) + lam_init


def _store_branches(br_ref, oa, ob, oc, od, bd_ref, gsub_ref, lam_init):
    bd = bd_ref[...]
    ob = [_head_rms_mxu(o, bd, gsub_ref[...], SUBLN_EPS) * (1.0 - lam_init) for o in ob]
    for j, o in enumerate((*oa, *ob, *oc, *od)):
        br_ref[:, j * LANES:(j + 1) * LANES] = o.astype(BF16)


def _attn_prompt_kernel(*refs, l, lam_init, n_jobs):
    qkv_ref, sink_ref, lam_ref, gsub_ref, bd_ref = refs[:5]
    br_ref = refs[5 + n_jobs]
    _run_cast_jobs(refs[5:5 + n_jobs], refs[6 + n_jobs:])
    sinks = [sink_ref[l, i] for i in range(4)]
    lam = _lambda(lam_ref, lam_init)
    for bi in range(qkv_ref.shape[0] // SEQ):
        r0 = bi * SEQ

        def cols(c, w):
            return qkv_ref[r0:r0 + SEQ, c:c + w]

        def qcols(c, w):
            return cols(c, w).astype(F32)

        oa = _gqa(qcols(A_QE, 128), qcols(A_QO, 128), [cols(A_K, 128)], [cols(A_V, VAUG)], [None], None)
        ob = _diff(qcols(B_Q, 256), [cols(B_K, 256)], lambda pr: [cols(B_V + pr * VAUG, VAUG)], lam)
        oc = _gqa(qcols(C_QE, 128), qcols(C_QO, 128), [cols(C_K, 128)], [cols(C_V, VAUG)], [None], sinks)
        od = _mha(qcols(D_Q, 256), [cols(D_K, 256)], lambda pr: [cols(D_V + pr * VAUG, VAUG)], lambda hd: [None])
        _store_branches(br_ref.at[r0:r0 + SEQ], oa, ob, oc, od, bd_ref, gsub_ref, lam_init)


def _attn_prompt_call(l, lam_init, qkv, sink_c, lam_b, gsub, bd128, jobs):
    rows = qkv.shape[0]
    tm = 4 * SEQ
    job_in, job_args, job_out, job_shape = _cast_job_specs(jobs, rows // tm, lambda b: b)
    outs = pl.pallas_call(
        functools.partial(_attn_prompt_kernel, l=l, lam_init=lam_init, n_jobs=len(jobs)),
        out_shape=[jax.ShapeDtypeStruct((rows, D_MODEL), BF16), *job_shape],
        grid=(rows // tm,),
        in_specs=[
            pl.BlockSpec((tm, QKV_W), lambda b: (b, 0)),
            pl.BlockSpec(memory_space=pltpu.SMEM),
            pl.BlockSpec((None, 4, B_HALF), lambda b: (l, 0, 0)),
            pl.BlockSpec((None, 1, LANES), lambda b: (l, 0, 0)),
            pl.BlockSpec((LANES, LANES), lambda b: (0, 0)),
            *job_in,
        ],
        out_specs=[pl.BlockSpec((tm, D_MODEL), lambda b: (b, 0)), *job_out],
        compiler_params=pltpu.CompilerParams(vmem_limit_bytes=VMEM_LIMIT),
        name="attn_prompt",
    )(qkv, sink_c, lam_b, gsub, bd128, *job_args)
    return outs[0], outs[1:]


def _attn_latent_kernel(*refs, l, lam_init, n_jobs):
    (kv_ref, xak_ref, xav_ref, xbk_ref, xbv_ref, xck_ref, xcv_ref, xdk_ref, xdv_ref,
     bias_ref, sink_ref, lam_ref, gsub_ref, bd_ref) = refs[:14]
    br_ref = refs[14 + n_jobs]
    _run_cast_jobs(refs[14:14 + n_jobs], refs[15 + n_jobs:])
    sinks = [sink_ref[l, i] for i in range(4)]
    lam = _lambda(lam_ref, lam_init)
    nsub = br_ref.shape[0] // ATT_BLK
    for sub in range(nsub):
        n = pl.program_id(1) * nsub + sub
        q0 = pl.multiple_of(n * ATT_BLK, ATT_BLK)

        def q(c, w):
            return kv_ref[pl.ds(q0, ATT_BLK), c:c + w].astype(F32)

        def lat(c, w):
            return kv_ref[:, c:c + w]

        def ctx_k(ref):
            return _Transposed(ref[...].reshape(ref.shape[0] * HEAD_DIM, PAST_LEN).astype(BF16))

        def ctx_v(ref, pr):
            vt = ref[2 * pr:2 * pr + 2].reshape(LANES, PAST_LEN).astype(BF16)
            return _Transposed(jnp.concatenate([vt, jnp.ones((LANES, PAST_LEN), BF16)], axis=0))

        oa = _gqa(q(A_QE, 128), q(A_QO, 128), [ctx_k(xak_ref), lat(A_K, 128)], [ctx_v(xav_ref, 0), lat(A_V, VAUG)],
                  [None, None], None)
        ob = _diff(q(B_Q, 256), [ctx_k(xbk_ref), lat(B_K, 256)],
                   lambda pr: [ctx_v(xbv_ref, pr), lat(B_V + pr * VAUG, VAUG)], lam)
        start_c = pl.multiple_of(jnp.clip(q0 - C_WINDOW, 0, DEC_SEQ - C_WIN), C_WINDOW)
        rowq = lax.broadcasted_iota(jnp.int32, (2 * ATT_BLK, C_WIN), 0) & (ATT_BLK - 1)
        colk = lax.broadcasted_iota(jnp.int32, (2 * ATT_BLK, C_WIN), 1)
        band = jnp.where(jnp.abs(rowq - colk + (q0 - start_c)) <= C_WINDOW, 0.0, NEG_INF)
        oc = _gqa(q(C_QE, 128), q(C_QO, 128),
                  [ctx_k(xck_ref), kv_ref[pl.ds(start_c, C_WIN), C_K:C_K + 128]],
                  [ctx_v(xcv_ref, 0), kv_ref[pl.ds(start_c, C_WIN), C_V:C_V + VAUG]],
                  [None, band], sinks)
        start_d = pl.multiple_of(jnp.where(n >= 2, NA_WIN_START[2] * GRID_W, 0), ATT_BLK)
        kwin = kv_ref[pl.ds(start_d, NA_WIN), D_K:D_K + 256]
        od = _mha(q(D_Q, 256), [ctx_k(xdk_ref), kwin],
                  lambda pr: [ctx_v(xdv_ref, pr),
                              kv_ref[pl.ds(start_d, NA_WIN), D_V + pr * VAUG:D_V + (pr + 1) * VAUG]],
                  lambda hd: [None, bias_ref[hd, sub]])
        _store_branches(br_ref.at[sub * ATT_BLK:(sub + 1) * ATT_BLK], oa, ob, oc, od, bd_ref, gsub_ref, lam_init)


def _attn_latent_call(l, lam_init, qkv, caches, dbias, sink_c, lam_b, gsub, bd128, jobs):
    rows = qkv.shape[0]
    nblk = DEC_SEQ // ATT_BLK
    nsub = 1
    steps_per_b = nblk // nsub
    cache_specs = [pl.BlockSpec((None, None, t.shape[2], HEAD_DIM, PAST_LEN), lambda b, n: (b, l, 0, 0, 0))
                   for t in caches]
    job_in, job_args, job_out, job_shape = _cast_job_specs(jobs, DEC_BATCH * steps_per_b,
                                                           lambda b, n: b * steps_per_b + n)
    outs = pl.pallas_call(
        functools.partial(_attn_latent_kernel, l=l, lam_init=lam_init, n_jobs=len(jobs)),
        out_shape=[jax.ShapeDtypeStruct((rows, D_MODEL), BF16), *job_shape],
        grid=(DEC_BATCH, steps_per_b),
        in_specs=[
            pl.BlockSpec((DEC_SEQ, QKV_W), lambda b, n: (b, 0)),
            *cache_specs,
            pl.BlockSpec((None, 4, nsub, ATT_BLK, NA_WIN), lambda b, n: (l, 0, n, 0, 0)),
            pl.BlockSpec(memory_space=pltpu.SMEM),
            pl.BlockSpec((None, 4, B_HALF), lambda b, n: (l, 0, 0)),
            pl.BlockSpec((None, 1, LANES), lambda b, n: (l, 0, 0)),
            pl.BlockSpec((LANES, LANES), lambda b, n: (0, 0)),
            *job_in,
        ],
        out_specs=[pl.BlockSpec((nsub * ATT_BLK, D_MODEL), lambda b, n: (b * steps_per_b + n, 0)), *job_out],
        compiler_params=pltpu.CompilerParams(vmem_limit_bytes=VMEM_LIMIT),
        name="attn_latent",
    )(qkv, *caches, dbias, sink_c, lam_b, gsub, bd128, *job_args)
    return outs[0], outs[1:]


MXU_TILE = 256
FFN_CHUNKS = ((0, 6 * MXU_TILE), (6 * MXU_TILE, D_FF))


def _post_kernel(x_ref, br_ref, gates_ref, gt1_ref, sh2_ref, sc2_ref, gt2_ref, g2_ref,
                 wb_ref, wo_ref, wfi_ref, wfo_ref, gf_ref, o_ref, *, final):
    merged = None
    for k in range(4):
        proj = _dot(br_ref[:, k * 256:(k + 1) * 256], wb_ref[k].astype(BF16))
        t = gates_ref[:, k * D_MODEL:(k + 1) * D_MODEL].astype(F32) * proj
        merged = t if merged is None else merged + t
    x1 = x_ref[...] + (0.5 * gt1_ref[...]) * _dot(merged.astype(BF16), wo_ref[...].astype(BF16))
    h2 = (_rms(x1, g2_ref[...], NORM_EPS) * (1.0 + sc2_ref[...]) + sh2_ref[...]).astype(BF16)
    acc = None
    for c0, c1 in FFN_CHUNKS:
        a = _dot(h2, wfi_ref[:, c0:c1])
        u = _dot(h2, wfi_ref[:, D_FF + c0:D_FF + c1])
        g = ((a * _sigmoid(a)) * u).astype(BF16)
        t = _dot(g, wfo_ref[c0:c1, :])
        acc = t if acc is None else acc + t
    xo = x1 + gt2_ref[...] * acc
    if final:
        xo = _rms(xo, gf_ref[...], NORM_EPS)
    o_ref[...] = xo


def _post_call(l, x, br, gates, mod, g2, wb, wo, wfi, wfo, gf, *, prompt, final):
    rows = x.shape[0]
    tm = 512
    if prompt:
        mod_row = lambda i: CTX_ROW
    else:
        mod_row = lambda i: i // (DEC_SEQ // tm)

    def mod_spec(chunk):
        return pl.BlockSpec((None, None, 1, D_MODEL), lambda i: (l, mod_row(i), 0, chunk))

    def resident(shape, layer):
        nd = len(shape)
        return pl.BlockSpec((None,) + shape, lambda i: (layer,) + (0,) * nd, pipeline_mode=pl.Buffered(1))

    return pl.pallas_call(
        functools.partial(_post_kernel, final=final),
        out_shape=jax.ShapeDtypeStruct((rows, D_MODEL), F32),
        grid=(rows // tm,),
        in_specs=[
            pl.BlockSpec((tm, D_MODEL), lambda i: (i, 0)),
            pl.BlockSpec((tm, D_MODEL), lambda i: (i, 0)),
            pl.BlockSpec((tm, GATE_W), lambda i: (i, 0)),
            mod_spec(2), mod_spec(3), mod_spec(4), mod_spec(5),
            pl.BlockSpec((None, 1, D_MODEL), lambda i: (l, 0, 0)),
            resident((4, 256, D_MODEL), l),
            resident((D_MODEL, D_MODEL), l),
            resident((D_MODEL, 2 * D_FF), 0),
            resident((D_FF, D_MODEL), 0),
            pl.BlockSpec((1, D_MODEL), lambda i: (0, 0)),
        ],
        out_specs=pl.BlockSpec((tm, D_MODEL), lambda i: (i, 0)),
        compiler_params=pltpu.CompilerParams(vmem_limit_bytes=VMEM_LIMIT),
        name="post_prompt" if prompt else "post_latent",
    )(x, br, gates, mod, mod, mod, mod, g2, wb, wo, wfi, wfo, gf)


def _rope_tables():
    t = np.arange(DEC_SEQ)
    row = (t // GRID_W).astype(np.float32)[:, None]
    col = (t % GRID_W).astype(np.float32)[:, None]
    tabs = []
    for d in (HEAD_DIM, B_HALF):
        quarter = d // 4
        inv = np.power(np.float32(ROPE_THETA), -np.arange(quarter, dtype=np.float32) / np.float32(quarter))
        ar, ac = row * inv, col * inv
        cos = np.concatenate([np.cos(ar), np.cos(ar), np.cos(ac), np.cos(ac)], axis=-1)
        sin = np.concatenate([-np.sin(ar), np.sin(ar), -np.sin(ac), np.sin(ac)], axis=-1)
        reps = LANES // d
        tabs += [jnp.asarray(np.tile(cos, (1, reps)), F32), jnp.asarray(np.tile(sin, (1, reps)), F32)]
    return tabs


def kernel(x_prompt, x_sample, cache_a_k, cache_a_v, cache_b_k, cache_b_v, cache_c_k, cache_c_v, cache_d_k, cache_d_v, c, c_ctx, w_ada, b_ada, g_norm1, w_in, g_q_a, g_k_a, lam_b, g_subln_b, sink_c, rpb_d, w_branch, w_out, g_norm2, w_ffn_in, w_ffn_out, g_final):
    wfi_l = wfo_l = None

    def ffn_jobs(layer, chunks_in, chunks_out):
        return [_CastJob(w_ffn_in, layer, chunks_in), _CastJob(w_ffn_out, layer, chunks_out)]

    gq = jnp.tile(g_q_a, (1, 2)).reshape(DEPTH, 1, LANES)
    gk = jnp.tile(g_k_a, (1, 2)).reshape(DEPTH, 1, LANES)
    gsub = jnp.tile(g_subln_b, (1, 2)).reshape(DEPTH, 1, LANES)
    g1 = g_norm1.reshape(DEPTH, 1, D_MODEL)
    g2 = g_norm2.reshape(DEPTH, 1, D_MODEL)
    gf = g_final.reshape(1, D_MODEL)
    head_of_lane = np.arange(LANES) // HEAD_DIM
    bd128 = jnp.asarray((head_of_lane[:, None] == head_of_lane[None, :]).astype(np.float32) / HEAD_DIM, BF16)
    rope_tabs = _rope_tables()
    caches = tuple(jnp.swapaxes(t, 3, 4) for t in (cache_a_k, cache_a_v, cache_b_k, cache_b_v,
                                                   cache_c_k, cache_c_v, cache_d_k, cache_d_v))

    cond = jnp.concatenate([c, c_ctx[None, :], jnp.zeros((MOD_ROWS - DEC_BATCH - 1, D_MODEL), F32)], axis=0)
    mod, dbias, (w_in_l,) = _setup_call(cond, w_ada, b_ada, rpb_d, [_CastJob(w_in, 0, DEPTH * 4)])

    xp = x_prompt.reshape(BATCH * SEQ, D_MODEL)
    xs = x_sample.reshape(DEC_BATCH * DEC_SEQ, D_MODEL)
    new_kv = None
    for l in range(DEPTH):
        lam_init = 0.8 - 0.6 * math.exp(-0.3 * l)
        final = l == DEPTH - 1
        outs = _pre_call(l, xp, mod, g1, w_in_l, gq, gk, None, new_kv, prompt=True)
        qkv_p, gates_p, new_kv = outs[0], outs[1], outs[2:]
        jobs = ffn_jobs(0, 8, 8) if l == 0 else []
        br_p, cast = _attn_prompt_call(l, lam_init, qkv_p, sink_c, lam_b, gsub, bd128, jobs)
        if cast:
            wfi_l, wfo_l = cast
        xp = _post_call(l, xp, br_p, gates_p, mod, g2, w_branch, w_out, wfi_l, wfo_l, gf, prompt=True, final=final)

        qkv_s, gates_s = _pre_call(l, xs, mod, g1, w_in_l, gq, gk, rope_tabs, None, prompt=False)
        jobs = [_CastJob(w_in, l + 1, 32), *ffn_jobs(l + 1, 32, 16)] if l + 1 < DEPTH else []
        br_s, cast = _attn_latent_call(l, lam_init, qkv_s, caches, dbias, sink_c, lam_b, gsub, bd128, jobs)
        xs = _post_call(l, xs, br_s, gates_s, mod, g2, w_branch, w_out, wfi_l, wfo_l, gf, prompt=False, final=final)
        if cast:
            w_in_l, wfi_l, wfo_l = cast

    y_prompt = xp.reshape(BATCH, SEQ, D_MODEL)
    y_sample = xs.reshape(DEC_BATCH, DEC_SEQ, D_MODEL)
    return (y_prompt, y_sample, *(jnp.swapaxes(t, 3, 4) for t in new_kv))
```

```python
import functools
import math
from typing import NamedTuple

import numpy as np
import jax
import jax.numpy as jnp
from jax import lax
from jax.experimental import pallas as pl
from jax.experimental.pallas import tpu as pltpu

F32 = jnp.float32
BF16 = jnp.bfloat16

D_MODEL = 1024
BATCH = 32
SEQ = 256
DEPTH = 2
DEC_BATCH = 8
DEC_SEQ = 1024
PAST_LEN = 256
GRID_W = 64
HEAD_DIM = 64
B_HALF = HEAD_DIM // 2
C_WINDOW = 128
NA_ROWS = 8
NA_COLS = 16
D_FF = 2816
ROPE_THETA = 10000.0
NORM_EPS = 1e-6
SUBLN_EPS = 1e-5
NEG_INF = -1e30

GATE_W = 4 * D_MODEL
W_A, W_B, W_C, W_D, W_GATES = 0, 512, 1280, 1792, 2560
IN_COLS = W_GATES + GATE_W
MOD_ROWS = 16
CTX_ROW = DEC_BATCH
LANES = 128
ATT_BLK = 256
NA_BLK_ROWS = ATT_BLK // GRID_W
NA_WIN_ROWS = 12
NA_WIN = NA_WIN_ROWS * GRID_W
NA_WIN_START = (0, 0, 4, 4)
C_WIN = ATT_BLK + 2 * C_WINDOW
VMEM_LIMIT = 56 * 1024 * 1024

A_QE, A_QO, A_K, A_V = 0, 128, 256, 384
B_Q, B_K, B_V = 640, 896, 1152
C_QE, C_QO, C_K, C_V = 1664, 1792, 1920, 2048
D_Q, D_K, D_V = 2304, 2560, 2816
QKV_W = 3328
VAUG = 2 * LANES


def _dot(a, b):
    return jnp.dot(a, b, preferred_element_type=F32)


def _dot_nt(a, b):
    return lax.dot_general(a, b, (((1,), (1,)), ((), ())), preferred_element_type=F32)


def _sigmoid(x):
    return 0.5 * jnp.tanh(0.5 * x) + 0.5


def _rms(x, g, eps):
    ms = jnp.mean(x * x, axis=-1, keepdims=True)
    return x * lax.rsqrt(ms + eps) * g


def _head_rms_mxu(v, bd, g, eps):
    v2 = v * v
    hi = v2.astype(BF16)
    lo = (v2 - hi.astype(F32)).astype(BF16)
    ms = _dot(hi, bd) + _dot(lo, bd)
    return v * lax.rsqrt(ms + eps) * g


def _head_rms(v, g, eps):
    v2 = v * v
    left = _lane_mask(v.shape, 0, HEAD_DIM)
    s_left = jnp.sum(jnp.where(left, v2, 0.0), axis=-1, keepdims=True)
    s_right = jnp.sum(jnp.where(left, 0.0, v2), axis=-1, keepdims=True)
    ms = jnp.where(left, s_left, s_right) * (1.0 / HEAD_DIM)
    return v * lax.rsqrt(ms + eps) * g


def _lane_mask(shape, lo, hi):
    lane = lax.broadcasted_iota(jnp.int32, shape, 1)
    return (lane >= lo) & (lane < hi)


def _mask_q(qf, lo, hi):
    return jnp.where(_lane_mask(qf.shape, lo, hi), qf, 0.0).astype(BF16)


def _interleave_heads(lo, hi):
    left = _lane_mask(lo.shape, 0, HEAD_DIM)
    return (jnp.where(left, lo, pltpu.roll(hi, HEAD_DIM, 1)),
            jnp.where(left, pltpu.roll(lo, HEAD_DIM, 1), hi))


def _ada_kernel(cond_ref, w_ref, b_ref, o_ref):
    c = cond_ref[...]
    s = (c * _sigmoid(c)).astype(BF16)
    res = _dot(s, w_ref[...].astype(BF16)) + b_ref[...]
    for r in range(MOD_ROWS):
        o_ref[r] = res[r:r + 1]


ADA_TILE = 6 * D_MODEL // 4


def _dbias_kernel(rpb_ref, o_ref):
    n_dr, n_dc = 2 * NA_ROWS - 1, 2 * NA_COLS - 1
    base = (pl.program_id(0) * 4 + pl.program_id(1)) * (n_dr * n_dc)
    shape = (GRID_W, LANES)
    cq = lax.broadcasted_iota(jnp.int32, shape, 0)
    lane = lax.broadcasted_iota(jnp.int32, shape, 1)
    ck = lane & (GRID_W - 1)
    dc = jnp.clip(ck - cq, -(NA_COLS - 1), NA_COLS - 1) + (NA_COLS - 1)
    start_c = jnp.clip(cq - NA_COLS // 2, 0, GRID_W - NA_COLS)
    col_valid = (ck >= start_c) & (ck < start_c + NA_COLS)
    neg = jnp.full(shape, NEG_INF, F32)
    toeplitz = []
    for dr in range(n_dr):
        t = jnp.zeros(shape, F32)
        for m in range(n_dc):
            t = jnp.where(dc == m, rpb_ref[base + dr * n_dc + m], t)
        toeplitz.append(jnp.where(col_valid, t, neg))
    left = lane < GRID_W
    rows = DEC_SEQ // GRID_W
    for n in range(DEC_SEQ // ATT_BLK):
        for rq in range(NA_BLK_ROWS):
            r = NA_BLK_ROWS * n + rq
            start_r = min(max(r - NA_ROWS // 2, 0), rows - NA_ROWS)
            for jp in range(NA_WIN_ROWS // 2):
                pair = []
                for j in (2 * jp, 2 * jp + 1):
                    key_row = NA_WIN_START[n] + j
                    valid = start_r <= key_row < start_r + NA_ROWS
                    pair.append(toeplitz[key_row - r + NA_ROWS - 1] if valid else neg)
                tile = pair[0] if pair[0] is pair[1] else jnp.where(left, pair[0], pair[1])
                o_ref[n, rq * GRID_W:(rq + 1) * GRID_W, jp * LANES:(jp + 1) * LANES] = tile


def _setup_kernel(*refs, n_jobs):
    cond_ref, w_ref, b_ref, rpb_ref = refs[:4]
    mod_ref, bias_ref = refs[4 + n_jobs:6 + n_jobs]
    _run_cast_jobs(refs[4:4 + n_jobs], refs[6 + n_jobs:])
    _ada_kernel(cond_ref, w_ref, b_ref, mod_ref)
    _dbias_kernel(rpb_ref, bias_ref)


def _setup_call(cond, w_ada, b_ada, rpb_d, jobs):
    nblk = DEC_SEQ // ATT_BLK
    job_in, job_args, job_out, job_shape = _cast_job_specs(jobs, DEPTH * 4, lambda l, j: l * 4 + j)
    outs = pl.pallas_call(
        functools.partial(_setup_kernel, n_jobs=len(jobs)),
        out_shape=[jax.ShapeDtypeStruct((DEPTH, MOD_ROWS, 1, 6 * D_MODEL), F32),
                   jax.ShapeDtypeStruct((DEPTH, 4, nblk, ATT_BLK, NA_WIN), F32), *job_shape],
        grid=(DEPTH, 4),
        in_specs=[
            pl.BlockSpec((MOD_ROWS, D_MODEL), lambda l, j: (0, 0)),
            pl.BlockSpec((None, D_MODEL, ADA_TILE), lambda l, j: (l, 0, j)),
            pl.BlockSpec((None, 1, ADA_TILE), lambda l, j: (l, 0, j)),
            pl.BlockSpec(memory_space=pltpu.SMEM),
            *job_in,
        ],
        out_specs=[pl.BlockSpec((None, MOD_ROWS, 1, ADA_TILE), lambda l, j: (l, 0, 0, j)),
                   pl.BlockSpec((None, None, nblk, ATT_BLK, NA_WIN), lambda l, j: (l, j, 0, 0, 0)), *job_out],
        compiler_params=pltpu.CompilerParams(vmem_limit_bytes=VMEM_LIMIT),
        name="setup",
    )(cond, w_ada, b_ada.reshape(DEPTH, 1, 6 * D_MODEL), rpb_d.reshape(-1), *job_args)
    return outs[0], outs[1], outs[2:]


def _swap_halves(v, half):
    lane = lax.broadcasted_iota(jnp.int32, v.shape, 1)
    up = pltpu.roll(v, LANES - half, 1)
    dn = pltpu.roll(v, half, 1)
    return jnp.where((lane & (2 * half - 1)) < half, up, dn)


def _rope(v, cos, sin, half):
    return v * cos + _swap_halves(v, half) * sin


def _pre_kernel(*refs, prompt, first, n_jobs):
    (x_ref, sh_ref, sc_ref, g1_ref, w_ref, gq_ref, gk_ref) = refs[:7]
    _run_cast_jobs(refs[7:7 + n_jobs], refs[len(refs) - n_jobs:])
    rest = refs[7 + n_jobs:len(refs) - n_jobs]
    if prompt:
        (qkv_ref, gates_ref, ka_ref, va_ref, kb_ref, vb_ref, kc_ref, vc_ref, kd_ref, vd_ref) = rest[-10:]
    else:
        (ca_ref, sa_ref, cb_ref, sb_ref, qkv_ref, gates_ref) = rest

    h = (_rms(x_ref[...], g1_ref[...], NORM_EPS) * (1.0 + sc_ref[...]) + sh_ref[...]).astype(BF16)
    ones = jnp.ones((h.shape[0], LANES), BF16)

    def rope_a(v):
        return v if prompt else _rope(v, ca_ref[...], sa_ref[...], 16)

    def rope_b(v):
        return v if prompt else _rope(v, cb_ref[...], sb_ref[...], 8)

    def put(col, v):
        qkv_ref[:, col:col + v.shape[1]] = v.astype(BF16)

    def put_values(col, v):
        for c in range(v.shape[1] // LANES):
            put(col + c * VAUG, v[:, c * LANES:(c + 1) * LANES])
            put(col + c * VAUG + LANES, ones)

    def put_heads(ref, v):
        for bi in range(v.shape[0] // SEQ):
            for pr in range(v.shape[1] // LANES):
                t = v[bi * SEQ:(bi + 1) * SEQ, pr * LANES:(pr + 1) * LANES].T
                for hh in range(2):
                    piece = t[hh * HEAD_DIM:(hh + 1) * HEAD_DIM]
                    if first:
                        ref[bi, 0, 2 * pr + hh] = piece
                        for later in range(1, DEPTH):
                            ref[bi, later, 2 * pr + hh] = jnp.zeros_like(piece)
                    else:
                        ref[bi, 2 * pr + hh] = piece

    scale = HEAD_DIM ** -0.5
    acc_a = _dot(h, w_ref[:, W_A:W_A + 512])
    acc = _dot(h, w_ref[:, W_B:W_B + 768])
    scale_b = B_HALF ** -0.5
    for c in range(2):
        put(B_Q + c * 128, rope_b(acc[:, c * 128:(c + 1) * 128]) * scale_b)
        put(B_K + c * 128, rope_b(acc[:, 256 + c * 128:256 + (c + 1) * 128]))
    put_values(B_V, acc[:, 512:768])
    if prompt:
        put_heads(kb_ref, acc[:, 256:512])
        put_heads(vb_ref, acc[:, 512:768])
    acc = _dot(h, w_ref[:, W_C:W_C + 512])
    q_e, q_o = _interleave_heads(acc[:, 0:128], acc[:, 128:256])
    put(C_QE, rope_a(q_e) * scale)
    put(C_QO, rope_a(q_o) * scale)
    put(C_K, rope_a(acc[:, 256:384]))
    put_values(C_V, acc[:, 384:512])
    if prompt:
        put_heads(kc_ref, acc[:, 256:384])
        put_heads(vc_ref, acc[:, 384:512])
    acc = _dot(h, w_ref[:, W_D:W_D + 768])
    put(D_Q, acc[:, 0:256] * scale)
    put(D_K, acc[:, 256:512])
    put_values(D_V, acc[:, 512:768])
    if prompt:
        put_heads(kd_ref, acc[:, 256:512])
        put_heads(vd_ref, acc[:, 512:768])
    gq = gq_ref[...]
    q_e, q_o = _interleave_heads(acc_a[:, 0:128], acc_a[:, 128:256])
    put(A_QE, rope_a(_head_rms(q_e, gq, NORM_EPS)) * scale)
    put(A_QO, rope_a(_head_rms(q_o, gq, NORM_EPS)) * scale)
    k_a = _head_rms(acc_a[:, 256:384], gk_ref[...], NORM_EPS)
    put(A_K, rope_a(k_a))
    put_values(A_V, acc_a[:, 384:512])
    if prompt:
        put_heads(ka_ref, k_a)
        put_heads(va_ref, acc_a[:, 384:512])
    for j in range(GATE_W // 512):
        g = _dot(h, w_ref[:, W_GATES + j * 512:W_GATES + (j + 1) * 512])
        gates_ref[:, j * 512:(j + 1) * 512] = (jnp.tanh(0.5 * g) + 1.0).astype(BF16)


def _pre_call(l, x, mod, g1, w_in, gq, gk, rope_tabs, kv_prev, jobs, *, prompt):
    rows = x.shape[0]
    tm = 512
    tq = tm
    bpb = tm // SEQ
    aliases = {}
    job_in, job_args, job_out, job_shape = _cast_job_specs(jobs, rows // tm, lambda i: i)
    if prompt:
        mod_row = lambda i: CTX_ROW
    else:
        mod_row = lambda i: i // (DEC_SEQ // tm)
    in_specs = [
        pl.BlockSpec((tm, D_MODEL), lambda i: (i, 0)),
        pl.BlockSpec((None, None, 1, D_MODEL), lambda i: (l, mod_row(i), 0, 0)),
        pl.BlockSpec((None, None, 1, D_MODEL), lambda i: (l, mod_row(i), 0, 1)),
        pl.BlockSpec((None, 1, D_MODEL), lambda i: (l, 0, 0)),
        pl.BlockSpec((None, D_MODEL, IN_COLS), lambda i: (0, 0, 0), pipeline_mode=pl.Buffered(1)),
        pl.BlockSpec((None, 1, LANES), lambda i: (l, 0, 0)),
        pl.BlockSpec((None, 1, LANES), lambda i: (l, 0, 0)),
        *job_in,
    ]
    args = [x, mod, mod, g1, w_in, gq, gk, *job_args]
    out_shape = [jax.ShapeDtypeStruct((rows, QKV_W), BF16), jax.ShapeDtypeStruct((rows, GATE_W), BF16)]
    out_specs = [pl.BlockSpec((tm, QKV_W), lambda i: (i, 0)), pl.BlockSpec((tm, GATE_W), lambda i: (i, 0))]
    if prompt:
        for j, nh in enumerate((2, 2, 4, 4, 2, 2, 4, 4)):
            out_shape.append(jax.ShapeDtypeStruct((BATCH, DEPTH, nh, HEAD_DIM, SEQ), F32))
            if kv_prev is None:
                out_specs.append(pl.BlockSpec((bpb, DEPTH, nh, HEAD_DIM, SEQ), lambda i: (i, 0, 0, 0, 0)))
            else:
                out_specs.append(pl.BlockSpec((bpb, None, nh, HEAD_DIM, SEQ), lambda i: (i, l, 0, 0, 0)))
                aliases[len(args)] = 2 + j
                in_specs.append(pl.BlockSpec(memory_space=pl.ANY))
                args.append(kv_prev[j])
    else:
        nt = DEC_SEQ // tq
        for t in rope_tabs:
            in_specs.append(pl.BlockSpec((tq, LANES), lambda i: (i % nt, 0)))
            args.append(t)
    n_main = len(out_shape)
    outs = pl.pallas_call(
        functools.partial(_pre_kernel, prompt=prompt, first=kv_prev is None, n_jobs=len(jobs)),
        out_shape=[*out_shape, *job_shape],
        grid=(rows // tm,),
        in_specs=in_specs,
        out_specs=[*out_specs, *job_out],
        input_output_aliases=aliases,
        compiler_params=pltpu.CompilerParams(vmem_limit_bytes=VMEM_LIMIT),
        name="pre_prompt" if prompt else "pre_latent",
    )(*args)
    return outs[:n_main], outs[n_main:]


class _CastJob(NamedTuple):
    src: jax.Array
    layer: int
    chunks: int


def _cast_job_specs(jobs, n_steps, linear_step):
    in_specs, args, out_specs, out_shape = [], [], [], []
    for job in jobs:
        _, rows, cols = job.src.shape
        chunk_rows = rows // job.chunks
        per = n_steps // job.chunks

        def chunk_map(*idx, layer, per=per):
            return (layer, linear_step(*idx) // per, 0)

        in_specs.append(pl.BlockSpec((None, chunk_rows, cols), functools.partial(chunk_map, layer=job.layer)))
        args.append(job.src)
        out_specs.append(pl.BlockSpec((None, chunk_rows, cols), functools.partial(chunk_map, layer=0)))
        out_shape.append(jax.ShapeDtypeStruct((1, rows, cols), BF16))
    return in_specs, args, out_specs, out_shape


def _run_cast_jobs(in_refs, out_refs):
    for i_ref, o_ref in zip(in_refs, out_refs, strict=True):
        o_ref[...] = i_ref[...].astype(BF16)


class _Transposed:
    def __init__(self, a):
        self.a = a


def _scores(qs, keys, biases):
    out = []
    for k, b in zip(keys, biases):
        s = _dot(qs, k.a) if isinstance(k, _Transposed) else _dot_nt(qs, k)
        out.append(s if b is None else s + b)
    return out


def _row_max(s_list, sink):
    m = None
    for s in s_list:
        mi = jnp.max(s, axis=-1, keepdims=True)
        m = mi if m is None else jnp.maximum(m, mi)
    return m if sink is None else jnp.maximum(m, sink)


def _softmax_pv(s_list, vaugs, sink=None):
    m = _row_max(s_list, sink)
    r = None
    for s, v in zip(s_list, vaugs):
        p = jnp.exp((s - m).astype(BF16))
        ri = _dot_nt(p, v.a) if isinstance(v, _Transposed) else _dot(p, v)
        r = ri if r is None else r + ri
    den = r[:, LANES:]
    if sink is not None:
        den = den + jnp.exp(sink - m)
    return r[:, :LANES] / den


def _gqa(qe, qo, keys, vaugs, biases, sinks):
    mq = qe.shape[0]
    oe = jnp.zeros((mq, LANES), F32)
    oo = jnp.zeros((mq, LANES), F32)
    for g in range(2):
        lo, hi = HEAD_DIM * g, HEAD_DIM * (g + 1)
        qs = jnp.concatenate([_mask_q(qe, lo, hi), _mask_q(qo, lo, hi)], axis=0)
        sink = None
        if sinks is not None:
            row = lax.broadcasted_iota(jnp.int32, (2 * mq, 1), 0)
            sink = jnp.where(row < mq, sinks[2 * g], sinks[2 * g + 1])
        r = _softmax_pv(_scores(qs, keys, biases), vaugs, sink)
        msk = _lane_mask((mq, LANES), lo, hi)
        oe = jnp.where(msk, r[:mq], oe)
        oo = jnp.where(msk, r[mq:], oo)
    return _interleave_heads(oe, oo)


def _diff(q, keys, vaug_fn, lam):
    mq = q.shape[0]
    out = [jnp.zeros((mq, LANES), F32), jnp.zeros((mq, LANES), F32)]
    for hd in range(4):
        lo = HEAD_DIM * hd
        qs = jnp.concatenate([_mask_q(q, lo, lo + B_HALF), _mask_q(q, lo + B_HALF, lo + HEAD_DIM)], axis=0)
        o = _softmax_pv(_scores(qs, keys, [None] * len(keys)), vaug_fn(hd // 2))
        o = o[:mq] - lam * o[mq:]
        plo = HEAD_DIM * (hd % 2)
        out[hd // 2] = jnp.where(_lane_mask(o.shape, plo, plo + HEAD_DIM), o, out[hd // 2])
    return out


def _mha(q, keys, vaug_fn, bias_fn):
    mq = q.shape[0]
    out = [jnp.zeros((mq, LANES), F32), jnp.zeros((mq, LANES), F32)]
    for hd in range(4):
        lo = HEAD_DIM * hd
        qs = _mask_q(q, lo, lo + HEAD_DIM)
        o = _softmax_pv(_scores(qs, keys, bias_fn(hd)), vaug_fn(hd // 2))
        plo = HEAD_DIM * (hd % 2)
        out[hd // 2] = jnp.where(_lane_mask(o.shape, plo, plo + HEAD_DIM), o, out[hd // 2])
    return out


def _lambda(lam_ref, lam_init):
    lp = lam_ref[...]
    a = jnp.sum(lp[0:1, :] * lp[1:2, :], axis=-1, keepdims=True)
    b = jnp.sum(lp[2:3, :] * lp[3:4, :], axis=-1, keepdims=True)
    return jnp.exp(a) - jnp.exp(b) + lam_init


def _store_branches(br_ref, oa, ob, oc, od, bd_ref, gsub_ref, lam_init):
    bd = bd_ref[...]
    ob = [_head_rms_mxu(o, bd, gsub_ref[...], SUBLN_EPS) * (1.0 - lam_init) for o in ob]
    for j, o in enumerate((*oa, *ob, *oc, *od)):
        br_ref[:, j * LANES:(j + 1) * LANES] = o.astype(BF16)


def _attn_prompt_kernel(*refs, l, lam_init, n_jobs):
    qkv_ref, sink_ref, lam_ref, gsub_ref, bd_ref = refs[:5]
    br_ref = refs[5 + n_jobs]
    _run_cast_jobs(refs[5:5 + n_jobs], refs[6 + n_jobs:])
    sinks = [sink_ref[l, i] for i in range(4)]
    lam = _lambda(lam_ref, lam_init)
    for bi in range(qkv_ref.shape[0] // SEQ):
        r0 = bi * SEQ

        def cols(c, w):
            return qkv_ref[r0:r0 + SEQ, c:c + w]

        def qcols(c, w):
            return cols(c, w).astype(F32)

        oa = _gqa(qcols(A_QE, 128), qcols(A_QO, 128), [cols(A_K, 128)], [cols(A_V, VAUG)], [None], None)
        ob = _diff(qcols(B_Q, 256), [cols(B_K, 256)], lambda pr: [cols(B_V + pr * VAUG, VAUG)], lam)
        oc = _gqa(qcols(C_QE, 128), qcols(C_QO, 128), [cols(C_K, 128)], [cols(C_V, VAUG)], [None], sinks)
        od = _mha(qcols(D_Q, 256), [cols(D_K, 256)], lambda pr: [cols(D_V + pr * VAUG, VAUG)], lambda hd: [None])
        _store_branches(br_ref.at[r0:r0 + SEQ], oa, ob, oc, od, bd_ref, gsub_ref, lam_init)


def _attn_prompt_call(l, lam_init, qkv, sink_c, lam_b, gsub, bd128, jobs):
    rows = qkv.shape[0]
    tm = 4 * SEQ
    job_in, job_args, job_out, job_shape = _cast_job_specs(jobs, rows // tm, lambda b: b)
    outs = pl.pallas_call(
        functools.partial(_attn_prompt_kernel, l=l, lam_init=lam_init, n_jobs=len(jobs)),
        out_shape=[jax.ShapeDtypeStruct((rows, D_MODEL), BF16), *job_shape],
        grid=(rows // tm,),
        in_specs=[
            pl.BlockSpec((tm, QKV_W), lambda b: (b, 0)),
            pl.BlockSpec(memory_space=pltpu.SMEM),
            pl.BlockSpec((None, 4, B_HALF), lambda b: (l, 0, 0)),
            pl.BlockSpec((None, 1, LANES), lambda b: (l, 0, 0)),
            pl.BlockSpec((LANES, LANES), lambda b: (0, 0)),
            *job_in,
        ],
        out_specs=[pl.BlockSpec((tm, D_MODEL), lambda b: (b, 0)), *job_out],
        compiler_params=pltpu.CompilerParams(vmem_limit_bytes=VMEM_LIMIT),
        name="attn_prompt",
    )(qkv, sink_c, lam_b, gsub, bd128, *job_args)
    return outs[0], outs[1:]


def _attn_latent_kernel(*refs, l, lam_init, n_jobs):
    (kv_ref, xak_ref, xav_ref, xbk_ref, xbv_ref, xck_ref, xcv_ref, xdk_ref, xdv_ref,
     bias_ref, sink_ref, lam_ref, gsub_ref, bd_ref) = refs[:14]
    br_ref = refs[14 + n_jobs]
    _run_cast_jobs(refs[14:14 + n_jobs], refs[15 + n_jobs:])
    sinks = [sink_ref[l, i] for i in range(4)]
    lam = _lambda(lam_ref, lam_init)
    nsub = br_ref.shape[0] // ATT_BLK
    for sub in range(nsub):
        n = pl.program_id(1) * nsub + sub
        q0 = pl.multiple_of(n * ATT_BLK, ATT_BLK)

        def q(c, w):
            return kv_ref[pl.ds(q0, ATT_BLK), c:c + w].astype(F32)

        def lat(c, w):
            return kv_ref[:, c:c + w]

        def ctx_k(ref):
            return _Transposed(ref[...].reshape(ref.shape[0] * HEAD_DIM, PAST_LEN).astype(BF16))

        def ctx_v(ref, pr):
            vt = ref[2 * pr:2 * pr + 2].reshape(LANES, PAST_LEN).astype(BF16)
            return _Transposed(jnp.concatenate([vt, jnp.ones((LANES, PAST_LEN), BF16)], axis=0))

        oa = _gqa(q(A_QE, 128), q(A_QO, 128), [ctx_k(xak_ref), lat(A_K, 128)], [ctx_v(xav_ref, 0), lat(A_V, VAUG)],
                  [None, None], None)
        ob = _diff(q(B_Q, 256), [ctx_k(xbk_ref), lat(B_K, 256)],
                   lambda pr: [ctx_v(xbv_ref, pr), lat(B_V + pr * VAUG, VAUG)], lam)
        start_c = pl.multiple_of(jnp.clip(q0 - C_WINDOW, 0, DEC_SEQ - C_WIN), C_WINDOW)
        rowq = lax.broadcasted_iota(jnp.int32, (2 * ATT_BLK, C_WIN), 0) & (ATT_BLK - 1)
        colk = lax.broadcasted_iota(jnp.int32, (2 * ATT_BLK, C_WIN), 1)
        band = jnp.where(jnp.abs(rowq - colk + (q0 - start_c)) <= C_WINDOW, 0.0, NEG_INF)
        oc = _gqa(q(C_QE, 128), q(C_QO, 128),
                  [ctx_k(xck_ref), kv_ref[pl.ds(start_c, C_WIN), C_K:C_K + 128]],
                  [ctx_v(xcv_ref, 0), kv_ref[pl.ds(start_c, C_WIN), C_V:C_V + VAUG]],
                  [None, band], sinks)
        start_d = pl.multiple_of(jnp.where(n >= 2, NA_WIN_START[2] * GRID_W, 0), ATT_BLK)
        kwin = kv_ref[pl.ds(start_d, NA_WIN), D_K:D_K + 256]
        od = _mha(q(D_Q, 256), [ctx_k(xdk_ref), kwin],
                  lambda pr: [ctx_v(xdv_ref, pr),
                              kv_ref[pl.ds(start_d, NA_WIN), D_V + pr * VAUG:D_V + (pr + 1) * VAUG]],
                  lambda hd: [None, bias_ref[hd, sub]])
        _store_branches(br_ref.at[sub * ATT_BLK:(sub + 1) * ATT_BLK], oa, ob, oc, od, bd_ref, gsub_ref, lam_init)


def _attn_latent_call(l, lam_init, qkv, caches, dbias, sink_c, lam_b, gsub, bd128, jobs):
    rows = qkv.shape[0]
    nblk = DEC_SEQ // ATT_BLK
    nsub = 1
    steps_per_b = nblk // nsub
    cache_specs = [pl.BlockSpec((None, None, t.shape[2], HEAD_DIM, PAST_LEN), lambda b, n: (b, l, 0, 0, 0))
                   for t in caches]
    job_in, job_args, job_out, job_shape = _cast_job_specs(jobs, DEC_BATCH * steps_per_b,
                                                           lambda b, n: b * steps_per_b + n)
    outs = pl.pallas_call(
        functools.partial(_attn_latent_kernel, l=l, lam_init=lam_init, n_jobs=len(jobs)),
        out_shape=[jax.ShapeDtypeStruct((rows, D_MODEL), BF16), *job_shape],
        grid=(DEC_BATCH, steps_per_b),
        in_specs=[
            pl.BlockSpec((DEC_SEQ, QKV_W), lambda b, n: (b, 0)),
            *cache_specs,
            pl.BlockSpec((None, 4, nsub, ATT_BLK, NA_WIN), lambda b, n: (l, 0, n, 0, 0)),
            pl.BlockSpec(memory_space=pltpu.SMEM),
            pl.BlockSpec((None, 4, B_HALF), lambda b, n: (l, 0, 0)),
            pl.BlockSpec((None, 1, LANES), lambda b, n: (l, 0, 0)),
            pl.BlockSpec((LANES, LANES), lambda b, n: (0, 0)),
            *job_in,
        ],
        out_specs=[pl.BlockSpec((nsub * ATT_BLK, D_MODEL), lambda b, n: (b * steps_per_b + n, 0)), *job_out],
        compiler_params=pltpu.CompilerParams(vmem_limit_bytes=VMEM_LIMIT),
        name="attn_latent",
    )(qkv, *caches, dbias, sink_c, lam_b, gsub, bd128, *job_args)
    return outs[0], outs[1:]


MXU_TILE = 256
FFN_CHUNKS = ((0, 6 * MXU_TILE), (6 * MXU_TILE, D_FF))


def _post_kernel(x_ref, br_ref, gates_ref, gt1_ref, sh2_ref, sc2_ref, gt2_ref, g2_ref,
                 wb_ref, wo_ref, wfi_ref, wfo_ref, gf_ref, o_ref, *, final):
    merged = None
    for k in range(4):
        proj = _dot(br_ref[:, k * 256:(k + 1) * 256], wb_ref[k].astype(BF16))
        t = gates_ref[:, k * D_MODEL:(k + 1) * D_MODEL].astype(F32) * proj
        merged = t if merged is None else merged + t
    x1 = x_ref[...] + (0.5 * gt1_ref[...]) * _dot(merged.astype(BF16), wo_ref[...].astype(BF16))
    h2 = (_rms(x1, g2_ref[...], NORM_EPS) * (1.0 + sc2_ref[...]) + sh2_ref[...]).astype(BF16)
    acc = None
    for c0, c1 in FFN_CHUNKS:
        a = _dot(h2, wfi_ref[:, c0:c1])
        u = _dot(h2, wfi_ref[:, D_FF + c0:D_FF + c1])
        g = ((a * _sigmoid(a)) * u).astype(BF16)
        t = _dot(g, wfo_ref[c0:c1, :])
        acc = t if acc is None else acc + t
    xo = x1 + gt2_ref[...] * acc
    if final:
        xo = _rms(xo, gf_ref[...], NORM_EPS)
    o_ref[...] = xo


def _post_call(l, x, br, gates, mod, g2, wb, wo, wfi, wfo, gf, *, prompt, final):
    rows = x.shape[0]
    tm = 512
    if prompt:
        mod_row = lambda i: CTX_ROW
    else:
        mod_row = lambda i: i // (DEC_SEQ // tm)

    def mod_spec(chunk):
        return pl.BlockSpec((None, None, 1, D_MODEL), lambda i: (l, mod_row(i), 0, chunk))

    def resident(shape, layer):
        nd = len(shape)
        return pl.BlockSpec((None,) + shape, lambda i: (layer,) + (0,) * nd, pipeline_mode=pl.Buffered(1))

    return pl.pallas_call(
        functools.partial(_post_kernel, final=final),
        out_shape=jax.ShapeDtypeStruct((rows, D_MODEL), F32),
        grid=(rows // tm,),
        in_specs=[
            pl.BlockSpec((tm, D_MODEL), lambda i: (i, 0)),
            pl.BlockSpec((tm, D_MODEL), lambda i: (i, 0)),
            pl.BlockSpec((tm, GATE_W), lambda i: (i, 0)),
            mod_spec(2), mod_spec(3), mod_spec(4), mod_spec(5),
            pl.BlockSpec((None, 1, D_MODEL), lambda i: (l, 0, 0)),
            resident((4, 256, D_MODEL), l),
            resident((D_MODEL, D_MODEL), l),
            resident((D_MODEL, 2 * D_FF), 0),
            resident((D_FF, D_MODEL), 0),
            pl.BlockSpec((1, D_MODEL), lambda i: (0, 0)),
        ],
        out_specs=pl.BlockSpec((tm, D_MODEL), lambda i: (i, 0)),
        compiler_params=pltpu.CompilerParams(vmem_limit_bytes=VMEM_LIMIT),
        name="post_prompt" if prompt else "post_latent",
    )(x, br, gates, mod, mod, mod, mod, g2, wb, wo, wfi, wfo, gf)


def _rope_tables():
    t = np.arange(DEC_SEQ)
    row = (t // GRID_W).astype(np.float32)[:, None]
    col = (t % GRID_W).astype(np.float32)[:, None]
    tabs = []
    for d in (HEAD_DIM, B_HALF):
        quarter = d // 4
        inv = np.power(np.float32(ROPE_THETA), -np.arange(quarter, dtype=np.float32) / np.float32(quarter))
        ar, ac = row * inv, col * inv
        cos = np.concatenate([np.cos(ar), np.cos(ar), np.cos(ac), np.cos(ac)], axis=-1)
        sin = np.concatenate([-np.sin(ar), np.sin(ar), -np.sin(ac), np.sin(ac)], axis=-1)
        reps = LANES // d
        tabs += [jnp.asarray(np.tile(cos, (1, reps)), F32), jnp.asarray(np.tile(sin, (1, reps)), F32)]
    return tabs


def kernel(x_prompt, x_sample, cache_a_k, cache_a_v, cache_b_k, cache_b_v, cache_c_k, cache_c_v, cache_d_k, cache_d_v, c, c_ctx, w_ada, b_ada, g_norm1, w_in, g_q_a, g_k_a, lam_b, g_subln_b, sink_c, rpb_d, w_branch, w_out, g_norm2, w_ffn_in, w_ffn_out, g_final):
    wfi_l = wfo_l = None

    def ffn_jobs(layer, chunks_in, chunks_out):
        return [_CastJob(w_ffn_in, layer, chunks_in), _CastJob(w_ffn_out, layer, chunks_out)]

    gq = jnp.tile(g_q_a, (1, 2)).reshape(DEPTH, 1, LANES)
    gk = jnp.tile(g_k_a, (1, 2)).reshape(DEPTH, 1, LANES)
    gsub = jnp.tile(g_subln_b, (1, 2)).reshape(DEPTH, 1, LANES)
    g1 = g_norm1.reshape(DEPTH, 1, D_MODEL)
    g2 = g_norm2.reshape(DEPTH, 1, D_MODEL)
    gf = g_final.reshape(1, D_MODEL)
    head_of_lane = np.arange(LANES) // HEAD_DIM
    bd128 = jnp.asarray((head_of_lane[:, None] == head_of_lane[None, :]).astype(np.float32) / HEAD_DIM, BF16)
    rope_tabs = _rope_tables()
    caches = tuple(jnp.swapaxes(t, 3, 4) for t in (cache_a_k, cache_a_v, cache_b_k, cache_b_v,
                                                   cache_c_k, cache_c_v, cache_d_k, cache_d_v))

    cond = jnp.concatenate([c, c_ctx[None, :], jnp.zeros((MOD_ROWS - DEC_BATCH - 1, D_MODEL), F32)], axis=0)
    mod, dbias, (w_in_l,) = _setup_call(cond, w_ada, b_ada, rpb_d, [_CastJob(w_in, 0, DEPTH * 4)])

    xp = x_prompt.reshape(BATCH * SEQ, D_MODEL)
    xs = x_sample.reshape(DEC_BATCH * DEC_SEQ, D_MODEL)
    new_kv = None
    for l in range(DEPTH):
        lam_init = 0.8 - 0.6 * math.exp(-0.3 * l)
        final = l == DEPTH - 1
        jobs = ffn_jobs(0, 16, 16) if l == 0 else []
        outs, cast = _pre_call(l, xp, mod, g1, w_in_l, gq, gk, None, new_kv, jobs, prompt=True)
        qkv_p, gates_p, new_kv = outs[0], outs[1], outs[2:]
        if cast:
            wfi_l, wfo_l = cast
        br_p, _ = _attn_prompt_call(l, lam_init, qkv_p, sink_c, lam_b, gsub, bd128, [])
        xp = _post_call(l, xp, br_p, gates_p, mod, g2, w_branch, w_out, wfi_l, wfo_l, gf, prompt=True, final=final)

        jobs = [_CastJob(w_in, l + 1, 16), *ffn_jobs(l + 1, 16, 16)] if l + 1 < DEPTH else []
        (qkv_s, gates_s), cast = _pre_call(l, xs, mod, g1, w_in_l, gq, gk, rope_tabs, None, jobs, prompt=False)
        br_s, _ = _attn_latent_call(l, lam_init, qkv_s, caches, dbias, sink_c, lam_b, gsub, bd128, [])
        xs = _post_call(l, xs, br_s, gates_s, mod, g2, w_branch, w_out, wfi_l, wfo_l, gf, prompt=False, final=final)
        if cast:
            w_in_l, wfi_l, wfo_l = cast

    y_prompt = xp.reshape(BATCH, SEQ, D_MODEL)
    y_sample = xs.reshape(DEC_BATCH, DEC_SEQ, D_MODEL)
    return (y_prompt, y_sample, *(jnp.swapaxes(t, 3, 4) for t in new_kv))
```

```python
import functools
import math
from typing import NamedTuple

import numpy as np
import jax
import jax.numpy as jnp
from jax import lax
from jax.experimental import pallas as pl
from jax.experimental.pallas import tpu as pltpu

F32 = jnp.float32
BF16 = jnp.bfloat16

D_MODEL = 1024
BATCH = 32
SEQ = 256
DEPTH = 2
DEC_BATCH = 8
DEC_SEQ = 1024
PAST_LEN = 256
GRID_W = 64
HEAD_DIM = 64
B_HALF = HEAD_DIM // 2
C_WINDOW = 128
NA_ROWS = 8
NA_COLS = 16
D_FF = 2816
ROPE_THETA = 10000.0
NORM_EPS = 1e-6
SUBLN_EPS = 1e-5
NEG_INF = -1e30

GATE_W = 4 * D_MODEL
W_A, W_B, W_C, W_D, W_GATES = 0, 512, 1280, 1792, 2560
IN_COLS = W_GATES + GATE_W
MOD_ROWS = 16
CTX_ROW = DEC_BATCH
LANES = 128
ATT_BLK = 256
NA_BLK_ROWS = ATT_BLK // GRID_W
NA_WIN_ROWS = 12
NA_WIN = NA_WIN_ROWS * GRID_W
NA_WIN_START = (0, 0, 4, 4)
C_WIN = ATT_BLK + 2 * C_WINDOW
VMEM_LIMIT = 56 * 1024 * 1024

A_QE, A_QO, A_K, A_V = 0, 128, 256, 384
B_Q, B_K, B_V = 640, 896, 1152
C_QE, C_QO, C_K, C_V = 1664, 1792, 1920, 2048
D_Q, D_K, D_V = 2304, 2560, 2816
QKV_W = 3328
VAUG = 2 * LANES


def _dot(a, b):
    return jnp.dot(a, b, preferred_element_type=F32)


def _dot_nt(a, b):
    return lax.dot_general(a, b, (((1,), (1,)), ((), ())), preferred_element_type=F32)


def _sigmoid(x):
    return 0.5 * jnp.tanh(0.5 * x) + 0.5


def _rms(x, g, eps):
    ms = jnp.mean(x * x, axis=-1, keepdims=True)
    return x * lax.rsqrt(ms + eps) * g


def _head_rms_mxu(v, bd, g, eps):
    v2 = v * v
    hi = v2.astype(BF16)
    lo = (v2 - hi.astype(F32)).astype(BF16)
    ms = _dot(hi, bd) + _dot(lo, bd)
    return v * lax.rsqrt(ms + eps) * g


def _head_rms(v, g, eps):
    v2 = v * v
    left = _lane_mask(v.shape, 0, HEAD_DIM)
    s_left = jnp.sum(jnp.where(left, v2, 0.0), axis=-1, keepdims=True)
    s_right = jnp.sum(jnp.where(left, 0.0, v2), axis=-1, keepdims=True)
    ms = jnp.where(left, s_left, s_right) * (1.0 / HEAD_DIM)
    return v * lax.rsqrt(ms + eps) * g


def _lane_mask(shape, lo, hi):
    lane = lax.broadcasted_iota(jnp.int32, shape, 1)
    return (lane >= lo) & (lane < hi)


def _mask_q(qf, lo, hi):
    return jnp.where(_lane_mask(qf.shape, lo, hi), qf, 0.0).astype(BF16)


def _interleave_heads(lo, hi):
    left = _lane_mask(lo.shape, 0, HEAD_DIM)
    return (jnp.where(left, lo, pltpu.roll(hi, HEAD_DIM, 1)),
            jnp.where(left, pltpu.roll(lo, HEAD_DIM, 1), hi))


def _ada_kernel(cond_ref, w_ref, b_ref, o_ref):
    c = cond_ref[...]
    s = (c * _sigmoid(c)).astype(BF16)
    res = _dot(s, w_ref[...].astype(BF16)) + b_ref[...]
    for r in range(MOD_ROWS):
        o_ref[r] = res[r:r + 1]


ADA_TILE = 6 * D_MODEL // 4


def _dbias_kernel(rpb_ref, o_ref):
    n_dr, n_dc = 2 * NA_ROWS - 1, 2 * NA_COLS - 1
    base = (pl.program_id(0) * 4 + pl.program_id(1)) * (n_dr * n_dc)
    shape = (GRID_W, LANES)
    cq = lax.broadcasted_iota(jnp.int32, shape, 0)
    lane = lax.broadcasted_iota(jnp.int32, shape, 1)
    ck = lane & (GRID_W - 1)
    dc = jnp.clip(ck - cq, -(NA_COLS - 1), NA_COLS - 1) + (NA_COLS - 1)
    start_c = jnp.clip(cq - NA_COLS // 2, 0, GRID_W - NA_COLS)
    col_valid = (ck >= start_c) & (ck < start_c + NA_COLS)
    neg = jnp.full(shape, NEG_INF, F32)
    toeplitz = []
    for dr in range(n_dr):
        t = jnp.zeros(shape, F32)
        for m in range(n_dc):
            t = jnp.where(dc == m, rpb_ref[base + dr * n_dc + m], t)
        toeplitz.append(jnp.where(col_valid, t, neg))
    left = lane < GRID_W
    rows = DEC_SEQ // GRID_W
    for n in range(DEC_SEQ // ATT_BLK):
        for rq in range(NA_BLK_ROWS):
            r = NA_BLK_ROWS * n + rq
            start_r = min(max(r - NA_ROWS // 2, 0), rows - NA_ROWS)
            for jp in range(NA_WIN_ROWS // 2):
                pair = []
                for j in (2 * jp, 2 * jp + 1):
                    key_row = NA_WIN_START[n] + j
                    valid = start_r <= key_row < start_r + NA_ROWS
                    pair.append(toeplitz[key_row - r + NA_ROWS - 1] if valid else neg)
                tile = pair[0] if pair[0] is pair[1] else jnp.where(left, pair[0], pair[1])
                o_ref[n, rq * GRID_W:(rq + 1) * GRID_W, jp * LANES:(jp + 1) * LANES] = tile


def _setup_kernel(*refs, n_jobs):
    cond_ref, w_ref, b_ref, rpb_ref = refs[:4]
    mod_ref, bias_ref = refs[4 + n_jobs:6 + n_jobs]
    _run_cast_jobs(refs[4:4 + n_jobs], refs[6 + n_jobs:])
    _ada_kernel(cond_ref, w_ref, b_ref, mod_ref)
    _dbias_kernel(rpb_ref, bias_ref)


def _setup_call(cond, w_ada, b_ada, rpb_d, jobs):
    nblk = DEC_SEQ // ATT_BLK
    job_in, job_args, job_out, job_shape = _cast_job_specs(jobs, DEPTH * 4, lambda l, j: l * 4 + j)
    outs = pl.pallas_call(
        functools.partial(_setup_kernel, n_jobs=len(jobs)),
        out_shape=[jax.ShapeDtypeStruct((DEPTH, MOD_ROWS, 1, 6 * D_MODEL), F32),
                   jax.ShapeDtypeStruct((DEPTH, 4, nblk, ATT_BLK, NA_WIN), F32), *job_shape],
        grid=(DEPTH, 4),
        in_specs=[
            pl.BlockSpec((MOD_ROWS, D_MODEL), lambda l, j: (0, 0)),
            pl.BlockSpec((None, D_MODEL, ADA_TILE), lambda l, j: (l, 0, j)),
            pl.BlockSpec((None, 1, ADA_TILE), lambda l, j: (l, 0, j)),
            pl.BlockSpec(memory_space=pltpu.SMEM),
            *job_in,
        ],
        out_specs=[pl.BlockSpec((None, MOD_ROWS, 1, ADA_TILE), lambda l, j: (l, 0, 0, j)),
                   pl.BlockSpec((None, None, nblk, ATT_BLK, NA_WIN), lambda l, j: (l, j, 0, 0, 0)), *job_out],
        compiler_params=pltpu.CompilerParams(vmem_limit_bytes=VMEM_LIMIT),
        name="setup",
    )(cond, w_ada, b_ada.reshape(DEPTH, 1, 6 * D_MODEL), rpb_d.reshape(-1), *job_args)
    return outs[0], outs[1], outs[2:]


def _swap_halves(v, half):
    lane = lax.broadcasted_iota(jnp.int32, v.shape, 1)
    up = pltpu.roll(v, LANES - half, 1)
    dn = pltpu.roll(v, half, 1)
    return jnp.where((lane & (2 * half - 1)) < half, up, dn)


def _rope(v, cos, sin, half):
    return v * cos + _swap_halves(v, half) * sin


def _pre_kernel(*refs, prompt, first, n_jobs):
    (x_ref, sh_ref, sc_ref, g1_ref, w_ref, gq_ref, gk_ref) = refs[:7]
    _run_cast_jobs(refs[7:7 + n_jobs], refs[len(refs) - n_jobs:])
    rest = refs[7 + n_jobs:len(refs) - n_jobs]
    if prompt:
        (qkv_ref, gates_ref, ka_ref, va_ref, kb_ref, vb_ref, kc_ref, vc_ref, kd_ref, vd_ref) = rest[-10:]
    else:
        (ca_ref, sa_ref, cb_ref, sb_ref, qkv_ref, gates_ref) = rest

    h = (_rms(x_ref[...], g1_ref[...], NORM_EPS) * (1.0 + sc_ref[...]) + sh_ref[...]).astype(BF16)
    ones = jnp.ones((h.shape[0], LANES), BF16)

    def rope_a(v):
        return v if prompt else _rope(v, ca_ref[...], sa_ref[...], 16)

    def rope_b(v):
        return v if prompt else _rope(v, cb_ref[...], sb_ref[...], 8)

    def put(col, v):
        qkv_ref[:, col:col + v.shape[1]] = v.astype(BF16)

    def put_values(col, v):
        for c in range(v.shape[1] // LANES):
            put(col + c * VAUG, v[:, c * LANES:(c + 1) * LANES])
            put(col + c * VAUG + LANES, ones)

    def put_heads(ref, v):
        for bi in range(v.shape[0] // SEQ):
            for pr in range(v.shape[1] // LANES):
                t = v[bi * SEQ:(bi + 1) * SEQ, pr * LANES:(pr + 1) * LANES].T
                for hh in range(2):
                    piece = t[hh * HEAD_DIM:(hh + 1) * HEAD_DIM]
                    if first:
                        ref[bi, 0, 2 * pr + hh] = piece
                        for later in range(1, DEPTH):
                            ref[bi, later, 2 * pr + hh] = jnp.zeros_like(piece)
                    else:
                        ref[bi, 2 * pr + hh] = piece

    scale = HEAD_DIM ** -0.5
    acc_a = _dot(h, w_ref[:, W_A:W_A + 512])
    acc = _dot(h, w_ref[:, W_B:W_B + 768])
    scale_b = B_HALF ** -0.5
    for c in range(2):
        put(B_Q + c * 128, rope_b(acc[:, c * 128:(c + 1) * 128]) * scale_b)
        put(B_K + c * 128, rope_b(acc[:, 256 + c * 128:256 + (c + 1) * 128]))
    put_values(B_V, acc[:, 512:768])
    if prompt:
        put_heads(kb_ref, acc[:, 256:512])
        put_heads(vb_ref, acc[:, 512:768])
    acc = _dot(h, w_ref[:, W_C:W_C + 512])
    q_e, q_o = _interleave_heads(acc[:, 0:128], acc[:, 128:256])
    put(C_QE, rope_a(q_e) * scale)
    put(C_QO, rope_a(q_o) * scale)
    put(C_K, rope_a(acc[:, 256:384]))
    put_values(C_V, acc[:, 384:512])
    if prompt:
        put_heads(kc_ref, acc[:, 256:384])
        put_heads(vc_ref, acc[:, 384:512])
    acc = _dot(h, w_ref[:, W_D:W_D + 768])
    put(D_Q, acc[:, 0:256] * scale)
    put(D_K, acc[:, 256:512])
    put_values(D_V, acc[:, 512:768])
    if prompt:
        put_heads(kd_ref, acc[:, 256:512])
        put_heads(vd_ref, acc[:, 512:768])
    gq = gq_ref[...]
    q_e, q_o = _interleave_heads(acc_a[:, 0:128], acc_a[:, 128:256])
    put(A_QE, rope_a(_head_rms(q_e, gq, NORM_EPS)) * scale)
    put(A_QO, rope_a(_head_rms(q_o, gq, NORM_EPS)) * scale)
    k_a = _head_rms(acc_a[:, 256:384], gk_ref[...], NORM_EPS)
    put(A_K, rope_a(k_a))
    put_values(A_V, acc_a[:, 384:512])
    if prompt:
        put_heads(ka_ref, k_a)
        put_heads(va_ref, acc_a[:, 384:512])
    for j in range(GATE_W // 512):
        g = _dot(h, w_ref[:, W_GATES + j * 512:W_GATES + (j + 1) * 512])
        gates_ref[:, j * 512:(j + 1) * 512] = (jnp.tanh(0.5 * g) + 1.0).astype(BF16)


def _pre_call(l, x, mod, g1, w_in, gq, gk, rope_tabs, kv_prev, jobs, *, prompt):
    rows = x.shape[0]
    tm = 512
    tq = tm
    bpb = tm // SEQ
    aliases = {}
    job_in, job_args, job_out, job_shape = _cast_job_specs(jobs, rows // tm, lambda i: i)
    if prompt:
        mod_row = lambda i: CTX_ROW
    else:
        mod_row = lambda i: i // (DEC_SEQ // tm)
    in_specs = [
        pl.BlockSpec((tm, D_MODEL), lambda i: (i, 0)),
        pl.BlockSpec((None, None, 1, D_MODEL), lambda i: (l, mod_row(i), 0, 0)),
        pl.BlockSpec((None, None, 1, D_MODEL), lambda i: (l, mod_row(i), 0, 1)),
        pl.BlockSpec((None, 1, D_MODEL), lambda i: (l, 0, 0)),
        pl.BlockSpec((None, D_MODEL, IN_COLS), lambda i: (0, 0, 0), pipeline_mode=pl.Buffered(1)),
        pl.BlockSpec((None, 1, LANES), lambda i: (l, 0, 0)),
        pl.BlockSpec((None, 1, LANES), lambda i: (l, 0, 0)),
        *job_in,
    ]
    args = [x, mod, mod, g1, w_in, gq, gk, *job_args]
    out_shape = [jax.ShapeDtypeStruct((rows, QKV_W), BF16), jax.ShapeDtypeStruct((rows, GATE_W), BF16)]
    out_specs = [pl.BlockSpec((tm, QKV_W), lambda i: (i, 0)), pl.BlockSpec((tm, GATE_W), lambda i: (i, 0))]
    if prompt:
        for j, nh in enumerate((2, 2, 4, 4, 2, 2, 4, 4)):
            out_shape.append(jax.ShapeDtypeStruct((BATCH, DEPTH, nh, HEAD_DIM, SEQ), F32))
            if kv_prev is None:
                out_specs.append(pl.BlockSpec((bpb, DEPTH, nh, HEAD_DIM, SEQ), lambda i: (i, 0, 0, 0, 0)))
            else:
                out_specs.append(pl.BlockSpec((bpb, None, nh, HEAD_DIM, SEQ), lambda i: (i, l, 0, 0, 0)))
                aliases[len(args)] = 2 + j
                in_specs.append(pl.BlockSpec(memory_space=pl.ANY))
                args.append(kv_prev[j])
    else:
        nt = DEC_SEQ // tq
        for t in rope_tabs:
            in_specs.append(pl.BlockSpec((tq, LANES), lambda i: (i % nt, 0)))
            args.append(t)
    n_main = len(out_shape)
    outs = pl.pallas_call(
        functools.partial(_pre_kernel, prompt=prompt, first=kv_prev is None, n_jobs=len(jobs)),
        out_shape=[*out_shape, *job_shape],
        grid=(rows // tm,),
        in_specs=in_specs,
        out_specs=[*out_specs, *job_out],
        input_output_aliases=aliases,
        compiler_params=pltpu.CompilerParams(vmem_limit_bytes=VMEM_LIMIT),
        name="pre_prompt" if prompt else "pre_latent",
    )(*args)
    return outs[:n_main], outs[n_main:]


class _CastJob(NamedTuple):
    src: jax.Array
    layer: int
    chunks: int


def _cast_job_specs(jobs, n_steps, linear_step):
    in_specs, args, out_specs, out_shape = [], [], [], []
    for job in jobs:
        _, rows, cols = job.src.shape
        chunk_rows = rows // job.chunks
        per = n_steps // job.chunks

        def chunk_map(*idx, layer, per=per):
            return (layer, linear_step(*idx) // per, 0)

        in_specs.append(pl.BlockSpec((None, chunk_rows, cols), functools.partial(chunk_map, layer=job.layer)))
        args.append(job.src)
        out_specs.append(pl.BlockSpec((None, chunk_rows, cols), functools.partial(chunk_map, layer=0)))
        out_shape.append(jax.ShapeDtypeStruct((1, rows, cols), BF16))
    return in_specs, args, out_specs, out_shape


def _run_cast_jobs(in_refs, out_refs):
    for i_ref, o_ref in zip(in_refs, out_refs, strict=True):
        o_ref[...] = i_ref[...].astype(BF16)


class _Transposed:
    def __init__(self, a):
        self.a = a


def _scores(qs, keys, biases):
    out = []
    for k, b in zip(keys, biases):
        s = _dot(qs, k.a) if isinstance(k, _Transposed) else _dot_nt(qs, k)
        out.append(s if b is None else s + b)
    return out


def _row_max(s_list, sink):
    m = None
    for s in s_list:
        mi = jnp.max(s, axis=-1, keepdims=True)
        m = mi if m is None else jnp.maximum(m, mi)
    return m if sink is None else jnp.maximum(m, sink)


def _softmax_pv(s_list, vaugs, sink=None):
    m = _row_max(s_list, sink)
    r = None
    for s, v in zip(s_list, vaugs):
        p = jnp.exp((s - m).astype(BF16))
        ri = _dot_nt(p, v.a) if isinstance(v, _Transposed) else _dot(p, v)
        r = ri if r is None else r + ri
    den = r[:, LANES:]
    if sink is not None:
        den = den + jnp.exp(sink - m)
    return r[:, :LANES] / den


def _gqa(qe, qo, keys, vaugs, biases, sinks):
    mq = qe.shape[0]
    oe = jnp.zeros((mq, LANES), F32)
    oo = jnp.zeros((mq, LANES), F32)
    for g in range(2):
        lo, hi = HEAD_DIM * g, HEAD_DIM * (g + 1)
        qs = jnp.concatenate([_mask_q(qe, lo, hi), _mask_q(qo, lo, hi)], axis=0)
        sink = None
        if sinks is not None:
            row = lax.broadcasted_iota(jnp.int32, (2 * mq, 1), 0)
            sink = jnp.where(row < mq, sinks[2 * g], sinks[2 * g + 1])
        r = _softmax_pv(_scores(qs, keys, biases), vaugs, sink)
        msk = _lane_mask((mq, LANES), lo, hi)
        oe = jnp.where(msk, r[:mq], oe)
        oo = jnp.where(msk, r[mq:], oo)
    return _interleave_heads(oe, oo)


def _diff(q, keys, vaug_fn, lam):
    mq = q.shape[0]
    out = [jnp.zeros((mq, LANES), F32), jnp.zeros((mq, LANES), F32)]
    for hd in range(4):
        lo = HEAD_DIM * hd
        qs = jnp.concatenate([_mask_q(q, lo, lo + B_HALF), _mask_q(q, lo + B_HALF, lo + HEAD_DIM)], axis=0)
        o = _softmax_pv(_scores(qs, keys, [None] * len(keys)), vaug_fn(hd // 2))
        o = o[:mq] - lam * o[mq:]
        plo = HEAD_DIM * (hd % 2)
        out[hd // 2] = jnp.where(_lane_mask(o.shape, plo, plo + HEAD_DIM), o, out[hd // 2])
    return out


def _mha(q, keys, vaug_fn, bias_fn):
    mq = q.shape[0]
    out = [jnp.zeros((mq, LANES), F32), jnp.zeros((mq, LANES), F32)]
    for hd in range(4):
        lo = HEAD_DIM * hd
        qs = _mask_q(q, lo, lo + HEAD_DIM)
        o = _softmax_pv(_scores(qs, keys, bias_fn(hd)), vaug_fn(hd // 2))
        plo = HEAD_DIM * (hd % 2)
        out[hd // 2] = jnp.where(_lane_mask(o.shape, plo, plo + HEAD_DIM), o, out[hd // 2])
    return out


def _lambda(lam_ref, lam_init):
    lp = lam_ref[...]
    a = jnp.sum(lp[0:1, :] * lp[1:2, :], axis=-1, keepdims=True)
    b = jnp.sum(lp[2:3, :] * lp[3:4, :], axis=-1, keepdims=True)
    return jnp.exp(a) - jnp.exp(b) + lam_init


def _store_branches(br_ref, oa, ob, oc, od, bd_ref, gsub_ref, lam_init):
    bd = bd_ref[...]
    ob = [_head_rms_mxu(o, bd, gsub_ref[...], SUBLN_EPS) * (1.0 - lam_init) for o in ob]
    for j, o in enumerate((*oa, *ob, *oc, *od)):
        br_ref[:, j * LANES:(j + 1) * LANES] = o.astype(BF16)


def _attn_prompt_kernel(*refs, l, lam_init, n_jobs):
    qkv_ref, sink_ref, lam_ref, gsub_ref, bd_ref = refs[:5]
    br_ref = refs[5 + n_jobs]
    _run_cast_jobs(refs[5:5 + n_jobs], refs[6 + n_jobs:])
    sinks = [sink_ref[l, i] for i in range(4)]
    lam = _lambda(lam_ref, lam_init)
    for bi in range(qkv_ref.shape[0] // SEQ):
        r0 = bi * SEQ

        def cols(c, w):
            return qkv_ref[r0:r0 + SEQ, c:c + w]

        def qcols(c, w):
            return cols(c, w).astype(F32)

        oa = _gqa(qcols(A_QE, 128), qcols(A_QO, 128), [cols(A_K, 128)], [cols(A_V, VAUG)], [None], None)
        ob = _diff(qcols(B_Q, 256), [cols(B_K, 256)], lambda pr: [cols(B_V + pr * VAUG, VAUG)], lam)
        oc = _gqa(qcols(C_QE, 128), qcols(C_QO, 128), [cols(C_K, 128)], [cols(C_V, VAUG)], [None], sinks)
        od = _mha(qcols(D_Q, 256), [cols(D_K, 256)], lambda pr: [cols(D_V + pr * VAUG, VAUG)], lambda hd: [None])
        _store_branches(br_ref.at[r0:r0 + SEQ], oa, ob, oc, od, bd_ref, gsub_ref, lam_init)


def _attn_prompt_call(l, lam_init, qkv, sink_c, lam_b, gsub, bd128, jobs):
    rows = qkv.shape[0]
    tm = 4 * SEQ
    job_in, job_args, job_out, job_shape = _cast_job_specs(jobs, rows // tm, lambda b: b)
    outs = pl.pallas_call(
        functools.partial(_attn_prompt_kernel, l=l, lam_init=lam_init, n_jobs=len(jobs)),
        out_shape=[jax.ShapeDtypeStruct((rows, D_MODEL), BF16), *job_shape],
        grid=(rows // tm,),
        in_specs=[
            pl.BlockSpec((tm, QKV_W), lambda b: (b, 0)),
            pl.BlockSpec(memory_space=pltpu.SMEM),
            pl.BlockSpec((None, 4, B_HALF), lambda b: (l, 0, 0)),
            pl.BlockSpec((None, 1, LANES), lambda b: (l, 0, 0)),
            pl.BlockSpec((LANES, LANES), lambda b: (0, 0)),
            *job_in,
        ],
        out_specs=[pl.BlockSpec((tm, D_MODEL), lambda b: (b, 0)), *job_out],
        compiler_params=pltpu.CompilerParams(vmem_limit_bytes=VMEM_LIMIT),
        name="attn_prompt",
    )(qkv, sink_c, lam_b, gsub, bd128, *job_args)
    return outs[0], outs[1:]


def _attn_latent_kernel(*refs, l, lam_init, n_jobs):
    (kv_ref, xak_ref, xav_ref, xbk_ref, xbv_ref, xck_ref, xcv_ref, xdk_ref, xdv_ref,
     bias_ref, sink_ref, lam_ref, gsub_ref, bd_ref) = refs[:14]
    br_ref = refs[14 + n_jobs]
    _run_cast_jobs(refs[14:14 + n_jobs], refs[15 + n_jobs:])
    sinks = [sink_ref[l, i] for i in range(4)]
    lam = _lambda(lam_ref, lam_init)
    nsub = br_ref.shape[0] // ATT_BLK
    for sub in range(nsub):
        n = pl.program_id(1) * nsub + sub
        q0 = pl.multiple_of(n * ATT_BLK, ATT_BLK)

        def q(c, w):
            return kv_ref[pl.ds(q0, ATT_BLK), c:c + w].astype(F32)

        def lat(c, w):
            return kv_ref[:, c:c + w]

        def ctx_k(ref):
            return _Transposed(ref[...].reshape(ref.shape[0] * HEAD_DIM, PAST_LEN).astype(BF16))

        def ctx_v(ref, pr):
            vt = ref[2 * pr:2 * pr + 2].reshape(LANES, PAST_LEN).astype(BF16)
            return _Transposed(jnp.concatenate([vt, jnp.ones((LANES, PAST_LEN), BF16)], axis=0))

        oa = _gqa(q(A_QE, 128), q(A_QO, 128), [ctx_k(xak_ref), lat(A_K, 128)], [ctx_v(xav_ref, 0), lat(A_V, VAUG)],
                  [None, None], None)
        ob = _diff(q(B_Q, 256), [ctx_k(xbk_ref), lat(B_K, 256)],
                   lambda pr: [ctx_v(xbv_ref, pr), lat(B_V + pr * VAUG, VAUG)], lam)
        start_c = pl.multiple_of(jnp.clip(q0 - C_WINDOW, 0, DEC_SEQ - C_WIN), C_WINDOW)
        rowq = lax.broadcasted_iota(jnp.int32, (2 * ATT_BLK, C_WIN), 0) & (ATT_BLK - 1)
        colk = lax.broadcasted_iota(jnp.int32, (2 * ATT_BLK, C_WIN), 1)
        band = jnp.where(jnp.abs(rowq - colk + (q0 - start_c)) <= C_WINDOW, 0.0, NEG_INF)
        oc = _gqa(q(C_QE, 128), q(C_QO, 128),
                  [ctx_k(xck_ref), kv_ref[pl.ds(start_c, C_WIN), C_K:C_K + 128]],
                  [ctx_v(xcv_ref, 0), kv_ref[pl.ds(start_c, C_WIN), C_V:C_V + VAUG]],
                  [None, band], sinks)
        start_d = pl.multiple_of(jnp.where(n >= 2, NA_WIN_START[2] * GRID_W, 0), ATT_BLK)
        kwin = kv_ref[pl.ds(start_d, NA_WIN), D_K:D_K + 256]
        od = _mha(q(D_Q, 256), [ctx_k(xdk_ref), kwin],
                  lambda pr: [ctx_v(xdv_ref, pr),
                              kv_ref[pl.ds(start_d, NA_WIN), D_V + pr * VAUG:D_V + (pr + 1) * VAUG]],
                  lambda hd: [None, bias_ref[hd, sub]])
        _store_branches(br_ref.at[sub * ATT_BLK:(sub + 1) * ATT_BLK], oa, ob, oc, od, bd_ref, gsub_ref, lam_init)


def _attn_latent_call(l, lam_init, qkv, caches, dbias, sink_c, lam_b, gsub, bd128, jobs):
    rows = qkv.shape[0]
    nblk = DEC_SEQ // ATT_BLK
    nsub = 1
    steps_per_b = nblk // nsub
    cache_specs = [pl.BlockSpec((None, None, t.shape[2], HEAD_DIM, PAST_LEN), lambda b, n: (b, l, 0, 0, 0))
                   for t in caches]
    job_in, job_args, job_out, job_shape = _cast_job_specs(jobs, DEC_BATCH * steps_per_b,
                                                           lambda b, n: b * steps_per_b + n)
    outs = pl.pallas_call(
        functools.partial(_attn_latent_kernel, l=l, lam_init=lam_init, n_jobs=len(jobs)),
        out_shape=[jax.ShapeDtypeStruct((rows, D_MODEL), BF16), *job_shape],
        grid=(DEC_BATCH, steps_per_b),
        in_specs=[
            pl.BlockSpec((DEC_SEQ, QKV_W), lambda b, n: (b, 0)),
            *cache_specs,
            pl.BlockSpec((None, 4, nsub, ATT_BLK, NA_WIN), lambda b, n: (l, 0, n, 0, 0)),
            pl.BlockSpec(memory_space=pltpu.SMEM),
            pl.BlockSpec((None, 4, B_HALF), lambda b, n: (l, 0, 0)),
            pl.BlockSpec((None, 1, LANES), lambda b, n: (l, 0, 0)),
            pl.BlockSpec((LANES, LANES), lambda b, n: (0, 0)),
            *job_in,
        ],
        out_specs=[pl.BlockSpec((nsub * ATT_BLK, D_MODEL), lambda b, n: (b * steps_per_b + n, 0)), *job_out],
        compiler_params=pltpu.CompilerParams(vmem_limit_bytes=VMEM_LIMIT),
        name="attn_latent",
    )(qkv, *caches, dbias, sink_c, lam_b, gsub, bd128, *job_args)
    return outs[0], outs[1:]


MXU_TILE = 256
FFN_CHUNKS = ((0, 6 * MXU_TILE), (6 * MXU_TILE, D_FF))


def _post_kernel(x_ref, br_ref, gates_ref, gt1_ref, sh2_ref, sc2_ref, gt2_ref, g2_ref,
                 wb_ref, wo_ref, wfi_ref, wfo_ref, gf_ref, o_ref, *, final):
    merged = None
    for k in range(4):
        proj = _dot(br_ref[:, k * 256:(k + 1) * 256], wb_ref[k])
        t = gates_ref[:, k * D_MODEL:(k + 1) * D_MODEL].astype(F32) * proj
        merged = t if merged is None else merged + t
    x1 = x_ref[...] + (0.5 * gt1_ref[...]) * _dot(merged.astype(BF16), wo_ref[...])
    h2 = (_rms(x1, g2_ref[...], NORM_EPS) * (1.0 + sc2_ref[...]) + sh2_ref[...]).astype(BF16)
    acc = None
    for c0, c1 in FFN_CHUNKS:
        a = _dot(h2, wfi_ref[:, c0:c1])
        u = _dot(h2, wfi_ref[:, D_FF + c0:D_FF + c1])
        g = ((a * _sigmoid(a)) * u).astype(BF16)
        t = _dot(g, wfo_ref[c0:c1, :])
        acc = t if acc is None else acc + t
    xo = x1 + gt2_ref[...] * acc
    if final:
        xo = _rms(xo, gf_ref[...], NORM_EPS)
    o_ref[...] = xo


def _post_call(l, x, br, gates, mod, g2, wb, wo, wfi, wfo, gf, *, prompt, final):
    rows = x.shape[0]
    tm = 512
    if prompt:
        mod_row = lambda i: CTX_ROW
    else:
        mod_row = lambda i: i // (DEC_SEQ // tm)

    def mod_spec(chunk):
        return pl.BlockSpec((None, None, 1, D_MODEL), lambda i: (l, mod_row(i), 0, chunk))

    def resident(shape, layer):
        nd = len(shape)
        return pl.BlockSpec((None,) + shape, lambda i: (layer,) + (0,) * nd, pipeline_mode=pl.Buffered(1))

    return pl.pallas_call(
        functools.partial(_post_kernel, final=final),
        out_shape=jax.ShapeDtypeStruct((rows, D_MODEL), F32),
        grid=(rows // tm,),
        in_specs=[
            pl.BlockSpec((tm, D_MODEL), lambda i: (i, 0)),
            pl.BlockSpec((tm, D_MODEL), lambda i: (i, 0)),
            pl.BlockSpec((tm, GATE_W), lambda i: (i, 0)),
            mod_spec(2), mod_spec(3), mod_spec(4), mod_spec(5),
            pl.BlockSpec((None, 1, D_MODEL), lambda i: (l, 0, 0)),
            resident((4, 256, D_MODEL), 0),
            resident((D_MODEL, D_MODEL), 0),
            resident((D_MODEL, 2 * D_FF), 0),
            resident((D_FF, D_MODEL), 0),
            pl.BlockSpec((1, D_MODEL), lambda i: (0, 0)),
        ],
        out_specs=pl.BlockSpec((tm, D_MODEL), lambda i: (i, 0)),
        compiler_params=pltpu.CompilerParams(vmem_limit_bytes=VMEM_LIMIT),
        name="post_prompt" if prompt else "post_latent",
    )(x, br, gates, mod, mod, mod, mod, g2, wb, wo, wfi, wfo, gf)


def _rope_tables():
    t = np.arange(DEC_SEQ)
    row = (t // GRID_W).astype(np.float32)[:, None]
    col = (t % GRID_W).astype(np.float32)[:, None]
    tabs = []
    for d in (HEAD_DIM, B_HALF):
        quarter = d // 4
        inv = np.power(np.float32(ROPE_THETA), -np.arange(quarter, dtype=np.float32) / np.float32(quarter))
        ar, ac = row * inv, col * inv
        cos = np.concatenate([np.cos(ar), np.cos(ar), np.cos(ac), np.cos(ac)], axis=-1)
        sin = np.concatenate([-np.sin(ar), np.sin(ar), -np.sin(ac), np.sin(ac)], axis=-1)
        reps = LANES // d
        tabs += [jnp.asarray(np.tile(cos, (1, reps)), F32), jnp.asarray(np.tile(sin, (1, reps)), F32)]
    return tabs


def kernel(x_prompt, x_sample, cache_a_k, cache_a_v, cache_b_k, cache_b_v, cache_c_k, cache_c_v, cache_d_k, cache_d_v, c, c_ctx, w_ada, b_ada, g_norm1, w_in, g_q_a, g_k_a, lam_b, g_subln_b, sink_c, rpb_d, w_branch, w_out, g_norm2, w_ffn_in, w_ffn_out, g_final):
    wfi_l = wfo_l = wb_l = wo_l = None
    w_branch_2d = w_branch.reshape(DEPTH, 4 * 256, D_MODEL)

    def proj_jobs(layer, chunks):
        return [_CastJob(w_branch_2d, layer, chunks), _CastJob(w_out, layer, chunks)]

    def ffn_jobs(layer, chunks_in, chunks_out):
        return [_CastJob(w_ffn_in, layer, chunks_in), _CastJob(w_ffn_out, layer, chunks_out)]

    gq = jnp.tile(g_q_a, (1, 2)).reshape(DEPTH, 1, LANES)
    gk = jnp.tile(g_k_a, (1, 2)).reshape(DEPTH, 1, LANES)
    gsub = jnp.tile(g_subln_b, (1, 2)).reshape(DEPTH, 1, LANES)
    g1 = g_norm1.reshape(DEPTH, 1, D_MODEL)
    g2 = g_norm2.reshape(DEPTH, 1, D_MODEL)
    gf = g_final.reshape(1, D_MODEL)
    head_of_lane = np.arange(LANES) // HEAD_DIM
    bd128 = jnp.asarray((head_of_lane[:, None] == head_of_lane[None, :]).astype(np.float32) / HEAD_DIM, BF16)
    rope_tabs = _rope_tables()
    caches = tuple(jnp.swapaxes(t, 3, 4) for t in (cache_a_k, cache_a_v, cache_b_k, cache_b_v,
                                                   cache_c_k, cache_c_v, cache_d_k, cache_d_v))

    cond = jnp.concatenate([c, c_ctx[None, :], jnp.zeros((MOD_ROWS - DEC_BATCH - 1, D_MODEL), F32)], axis=0)
    mod, dbias, (w_in_l,) = _setup_call(cond, w_ada, b_ada, rpb_d, [_CastJob(w_in, 0, DEPTH * 4)])

    xp = x_prompt.reshape(BATCH * SEQ, D_MODEL)
    xs = x_sample.reshape(DEC_BATCH * DEC_SEQ, D_MODEL)
    new_kv = None
    for l in range(DEPTH):
        lam_init = 0.8 - 0.6 * math.exp(-0.3 * l)
        final = l == DEPTH - 1
        jobs = ffn_jobs(0, 16, 16) if l == 0 else []
        outs, cast = _pre_call(l, xp, mod, g1, w_in_l, gq, gk, None, new_kv, jobs, prompt=True)
        qkv_p, gates_p, new_kv = outs[0], outs[1], outs[2:]
        if cast:
            wfi_l, wfo_l = cast
        br_p, cast = _attn_prompt_call(l, lam_init, qkv_p, sink_c, lam_b, gsub, bd128, proj_jobs(0, 8) if l == 0 else [])
        if cast:
            wb_l, wo_l = cast[0].reshape(1, 4, 256, D_MODEL), cast[1]
        xp = _post_call(l, xp, br_p, gates_p, mod, g2, wb_l, wo_l, wfi_l, wfo_l, gf, prompt=True, final=final)

        jobs = [_CastJob(w_in, l + 1, 16), *ffn_jobs(l + 1, 16, 16)] if l + 1 < DEPTH else []
        (qkv_s, gates_s), cast = _pre_call(l, xs, mod, g1, w_in_l, gq, gk, rope_tabs, None, jobs, prompt=False)
        br_s, cast_proj = _attn_latent_call(l, lam_init, qkv_s, caches, dbias, sink_c, lam_b, gsub, bd128,
                                            proj_jobs(l + 1, 32) if l + 1 < DEPTH else [])
        xs = _post_call(l, xs, br_s, gates_s, mod, g2, wb_l, wo_l, wfi_l, wfo_l, gf, prompt=False, final=final)
        if cast:
            w_in_l, wfi_l, wfo_l = cast
        if cast_proj:
            wb_l, wo_l = cast_proj[0].reshape(1, 4, 256, D_MODEL), cast_proj[1]

    y_prompt = xp.reshape(BATCH, SEQ, D_MODEL)
    y_sample = xs.reshape(DEC_BATCH, DEC_SEQ, D_MODEL)
    return (y_prompt, y_sample, *(jnp.swapaxes(t, 3, 4) for t in new_kv))
```
